```python
import math
import jax
import jax.numpy as jnp
from jax import lax
import numpy as np

D_MODEL = 1024
BATCH = 8
SEQ = 4096
DEPTH = 2

RMS_EPS = 1e-6
ROPE_THETA = 500000.0
Q_BLOCK = 128
A_HEADS = 8
A_KV_HEADS = 2
A_HEAD_DIM = 64
A_ROT = A_HEAD_DIM // 4
IDX_HEADS = 8
IDX_DIM = 32
IDX_ROT = IDX_DIM // 4
IDX_TOPK_MAX = 256
SC_W = 512
SC_KERNEL = 3
GM_W = 512
GM_GROUPS = 4
GM_GROUP_W = GM_W // GM_GROUPS
GM_CHUNK = 128
M_DINNER = 512
M_HEAD_DIM = 64
M_HEADS = M_DINNER // M_HEAD_DIM
M_GROUPS = 2
M_STATE = 64
M_CONV = 4
M_CHUNK = 128
M_BC = M_GROUPS * M_STATE
M_XBC = M_DINNER + 2 * M_BC
N_BRANCH = 4
BRANCH_W = 512
N_EXPERTS = 16
N_EXPERT_GROUPS = 4
EXPERTS_PER_GROUP = N_EXPERTS // N_EXPERT_GROUPS
TOP_K = 2
D_FF_EXPERT = 512

A_Q = A_HEADS * A_HEAD_DIM
A_KV = A_KV_HEADS * A_HEAD_DIM
IDX_Q = IDX_HEADS * IDX_DIM
IN_SPLITS = (A_Q, A_KV, A_KV, IDX_Q, IDX_DIM, IDX_HEADS,
             SC_W, SC_W, SC_W,
             2 * GM_W,
             M_DINNER, M_XBC, M_HEADS,
             N_BRANCH * D_MODEL)
IN_COLS = sum(IN_SPLITS)

kernel_name = "hybrid_dsa_conv_gmlp_ssd_moe"


def rms_norm(x, g):
    xf = x.astype(jnp.float32)
    y = xf * lax.rsqrt(jnp.mean(xf * xf, axis=-1, keepdims=True) + RMS_EPS)
    return y.astype(x.dtype) * g


def partial_rope(x, positions, rot_dim):
    half = rot_dim // 2
    inv_freq = jnp.exp(-math.log(ROPE_THETA) * jnp.arange(half, dtype=jnp.float32) * 2.0 / rot_dim)
    ang = positions.astype(jnp.float32)[..., None] * inv_freq
    cos = jnp.cos(ang)[:, :, None, :].astype(x.dtype)
    sin = jnp.sin(ang)[:, :, None, :].astype(x.dtype)
    x1 = x[..., :half]
    x2 = x[..., half:rot_dim]
    return jnp.concatenate([x1 * cos - x2 * sin, x2 * cos + x1 * sin, x[..., rot_dim:]], axis=-1)


def causal_depthwise_conv(u, w):
    k_w, ch = w.shape
    return lax.conv_general_dilated(
        u, w[:, None, :].astype(u.dtype), window_strides=(1,), padding=[(k_w - 1, 0)],
        dimension_numbers=('NWC', 'WIO', 'NWC'), feature_group_count=ch)


def dsa_attention(q, k, v, qi, ki, wi):
    bsz, seq = q.shape[0], q.shape[1]
    topk = min(IDX_TOPK_MAX, seq // 4)
    nb = seq // Q_BLOCK
    rep = A_HEADS // A_KV_HEADS
    key_pos = jnp.arange(seq, dtype=jnp.int32)

    def to_blocks(a):
        return jnp.moveaxis(a.reshape((bsz, nb, Q_BLOCK) + a.shape[2:]), 1, 0)

    def one_block(args):
        qb, qib, wib, start = args
        q_pos = start + jnp.arange(Q_BLOCK, dtype=jnp.int32)
        causal = key_pos[None, :] <= q_pos[:, None]
        dots = jnp.einsum('bthd,bsd->bths', qib, ki).astype(jnp.float32) * (IDX_DIM ** -0.5)
        idx_score = jnp.einsum('bths,bth->bts', jax.nn.relu(dots), wib.astype(jnp.float32))
        idx_score = jnp.where(causal[None], idx_score, -jnp.inf)
        _, sel = lax.top_k(idx_score, topk)
        kg = jax.vmap(lambda kk, ii: kk[ii])(k, sel)
        vg = jax.vmap(lambda vv, ii: vv[ii])(v, sel)
        valid = sel <= q_pos[None, :, None]
        qg = qb.reshape(bsz, Q_BLOCK, A_KV_HEADS, rep, A_HEAD_DIM)
        logits = jnp.einsum('btgrd,btkgd->btgrk', qg, kg).astype(jnp.float32) * (A_HEAD_DIM ** -0.5)
        logits = jnp.where(valid[:, :, None, None, :], logits, -jnp.inf)
        p = jax.nn.softmax(logits, axis=-1).astype(vg.dtype)
        o = jnp.einsum('btgrk,btkgd->btgrd', p, vg)
        return o.reshape(bsz, Q_BLOCK, A_Q)

    starts = jnp.arange(nb, dtype=jnp.int32) * Q_BLOCK
    out = lax.map(one_block, (to_blocks(q), to_blocks(qi), to_blocks(wi), starts))
    return jnp.moveaxis(out, 0, 1).reshape(bsz, seq, A_Q)


def ssd_scan(xs, dt, a_coef, bm, cm):
    bsz, seq, nh, hp = xs.shape
    nc = seq // M_CHUNK
    rep = nh // bm.shape[2]
    bh = jnp.repeat(bm, rep, axis=2).reshape(bsz, nc, M_CHUNK, nh, M_STATE)
    ch = jnp.repeat(cm, rep, axis=2).reshape(bsz, nc, M_CHUNK, nh, M_STATE)
    xdt = (xs * dt[..., None]).reshape(bsz, nc, M_CHUNK, nh, hp)
    acs = jnp.cumsum((dt * a_coef).reshape(bsz, nc, M_CHUNK, nh), axis=2)
    acs_h = jnp.swapaxes(acs, 2, 3)
    diff = acs_h[..., :, None] - acs_h[..., None, :]
    tri = jnp.tril(jnp.ones((M_CHUNK, M_CHUNK), dtype=bool))
    decay = jnp.exp(jnp.where(tri, diff, -jnp.inf))
    scores = jnp.einsum('bclhn,bcshn->bchls', ch, bh) * decay
    y_diag = jnp.einsum('bchls,bcshp->bclhp', scores, xdt)
    decay_states = jnp.exp(acs[:, :, -1:, :] - acs)
    states = jnp.einsum('bclhn,bclh,bclhp->bchpn', bh, decay_states, xdt)
    chunk_decay = jnp.exp(acs[:, :, -1, :])

    def step(carry, inp):
        st, dec = inp
        return carry * dec[..., None, None] + st, carry

    init = jnp.zeros((bsz, nh, hp, M_STATE), dtype=states.dtype)
    _, prev = lax.scan(step, init, (jnp.moveaxis(states, 1, 0), jnp.moveaxis(chunk_decay, 1, 0)))
    prev = jnp.moveaxis(prev, 0, 1)
    y_off = jnp.einsum('bclhn,bchpn,bclh->bclhp', ch, prev, jnp.exp(acs))
    return (y_diag + y_off).reshape(bsz, seq, nh, hp).astype(xs.dtype)


def hybrid_mixer(h, positions, w_in, q_norm_g, k_norm_g, sc_conv_w, g_v_norm_g, g_spatial_w,
                 g_spatial_b, m_conv_w, m_conv_b, m_dt_bias, m_a_log, m_d, m_norm_g, w_branch, w_out):
    bsz, seq, _ = h.shape
    points = [int(p) for p in np.cumsum(IN_SPLITS)[:-1]]
    proj = h @ w_in
    (q, k, v, qi, ki, wi, sc_b, sc_c, sc_x, g_uv, m_z, m_xbc, m_dt, gates) = jnp.split(proj, points, axis=-1)

    q = partial_rope(rms_norm(q.reshape(bsz, seq, A_HEADS, A_HEAD_DIM), q_norm_g), positions, A_ROT)
    k = partial_rope(rms_norm(k.reshape(bsz, seq, A_KV_HEADS, A_HEAD_DIM), k_norm_g), positions, A_ROT)
    v = v.reshape(bsz, seq, A_KV_HEADS, A_HEAD_DIM)
    qi = partial_rope(qi.reshape(bsz, seq, IDX_HEADS, IDX_DIM), positions, IDX_ROT)
    ki = partial_rope(ki.reshape(bsz, seq, 1, IDX_DIM), positions, IDX_ROT)[:, :, 0]
    wi = wi * (IDX_HEADS ** -0.5)
    o_a = dsa_attention(q, k, v, qi, ki, wi)

    o_b = sc_b * causal_depthwise_conv(sc_c * sc_x, sc_conv_w)

    g_uv = jax.nn.gelu(g_uv, approximate=False)
    u, gv = jnp.split(g_uv, 2, axis=-1)
    gv = rms_norm(gv, g_v_norm_g)
    vb = gv.reshape(bsz, seq // GM_CHUNK, GM_CHUNK, GM_GROUPS, GM_GROUP_W)
    tri = jnp.tril(jnp.ones((GM_CHUNK, GM_CHUNK), dtype=bool))
    ws = jnp.where(tri[None], g_spatial_w, 0.0).astype(vb.dtype)
    mixed = jnp.einsum('gts,bcsgd->bctgd', ws, vb) + g_spatial_b.T[None, None, :, :, None]
    o_c = u * mixed.reshape(bsz, seq, GM_W)

    xbc = jax.nn.silu(causal_depthwise_conv(m_xbc, m_conv_w) + m_conv_b)
    xs, bm, cm = jnp.split(xbc, [M_DINNER, M_DINNER + M_BC], axis=-1)
    xs = xs.reshape(bsz, seq, M_HEADS, M_HEAD_DIM)
    bm = bm.reshape(bsz, seq, M_GROUPS, M_STATE)
    cm = cm.reshape(bsz, seq, M_GROUPS, M_STATE)
    dt = jax.nn.softplus(m_dt.astype(jnp.float32) + m_dt_bias.astype(jnp.float32))
    a_coef = -jnp.exp(m_a_log.astype(jnp.float32))
    y = ssd_scan(xs, dt, a_coef, bm, cm) + m_d[:, None] * xs
    o_d = rms_norm(y.reshape(bsz, seq, M_DINNER) * jax.nn.silu(m_z), m_norm_g)

    gates = jax.nn.sigmoid(gates.reshape(bsz, seq, N_BRANCH, D_MODEL))
    branches = (o_a, o_b, o_c, o_d)
    merged = gates[:, :, 0] * (branches[0] @ w_branch[0])
    for i in range(1, N_BRANCH):
        merged = merged + gates[:, :, i] * (branches[i] @ w_branch[i])
    return merged @ w_out


def grouped_moe(h, router_w, router_bias, w_gate, w_up, w_down):
    bsz, seq, dm = h.shape
    t = h.reshape(-1, dm)
    ntok = t.shape[0]
    scores = jax.nn.sigmoid((t @ router_w).astype(jnp.float32))
    sel = scores + router_bias.astype(jnp.float32)
    grp_score = lax.top_k(sel.reshape(ntok, N_EXPERT_GROUPS, EXPERTS_PER_GROUP), 2)[0].sum(-1)
    best = jnp.argmax(grp_score, axis=-1)
    in_grp = (jnp.arange(N_EXPERTS) // EXPERTS_PER_GROUP)[None, :] == best[:, None]
    _, idx = lax.top_k(jnp.where(in_grp, sel, -jnp.inf), TOP_K)
    wts = jnp.take_along_axis(scores, idx, axis=-1)
    wts = wts / jnp.sum(wts, axis=-1, keepdims=True)
    combine = jnp.sum(jax.nn.one_hot(idx, N_EXPERTS, dtype=jnp.float32) * wts[..., None], axis=1).astype(t.dtype)
    out = jnp.zeros_like(t)
    for e in range(N_EXPERTS):
        hid = jax.nn.silu(t @ w_gate[e]) * (t @ w_up[e])
        out = out + combine[:, e:e + 1] * (hid @ w_down[e])
    return out.reshape(bsz, seq, dm)


def setup_inputs(seed: int = 0) -> dict:
    key = jax.random.key(seed)
    ks = iter(jax.random.split(key, 32))

    def nrm(shape, scale):
        return jax.random.normal(next(ks), shape, jnp.float32) * scale

    L, D = DEPTH, D_MODEL
    x = nrm((BATCH, SEQ, D), 1.0)
    c = nrm((BATCH, D), 1.0)
    positions = jnp.tile(jnp.arange(SEQ, dtype=jnp.int32)[None, :], (BATCH, 1))
    ada_w = nrm((L, D, 6 * D), 0.5 * D ** -0.5)
    ada_b = nrm((L, 6 * D), 0.01)
    norm1_g = 1.0 + nrm((L, D), 0.01)
    w_in = nrm((L, D, IN_COLS), D ** -0.5)
    q_norm_g = 1.0 + nrm((L, A_HEAD_DIM), 0.01)
    k_norm_g = 1.0 + nrm((L, A_HEAD_DIM), 0.01)
    sc_conv_w = nrm((L, SC_KERNEL, SC_W), SC_KERNEL ** -0.5)
    g_v_norm_g = 1.0 + nrm((L, GM_W), 0.01)
    g_spatial_w = nrm((L, GM_GROUPS, GM_CHUNK, GM_CHUNK), GM_CHUNK ** -0.5)
    g_spatial_b = 1.0 + nrm((L, GM_GROUPS, GM_CHUNK), 0.01)
    m_conv_w = nrm((L, M_CONV, M_XBC), M_CONV ** -0.5)
    m_conv_b = nrm((L, M_XBC), 0.01)
    dt0 = jnp.exp(jax.random.uniform(next(ks), (L, M_HEADS), jnp.float32, math.log(1e-3), math.log(1e-1)))
    m_dt_bias = dt0 + jnp.log(-jnp.expm1(-dt0))
    m_a_log = jnp.log(jax.random.uniform(next(ks), (L, M_HEADS), jnp.float32, 1.0, 16.0))
    m_d = 1.0 + nrm((L, M_HEADS), 0.01)
    m_norm_g = 1.0 + nrm((L, M_DINNER), 0.01)
    w_branch = nrm((L, N_BRANCH, BRANCH_W, D), BRANCH_W ** -0.5)
    w_out = nrm((L, D, D), D ** -0.5)
    norm2_g = 1.0 + nrm((L, D), 0.01)
    router_w = nrm((D, N_EXPERTS), D ** -0.5)
    router_bias = nrm((N_EXPERTS,), 0.01)
    exp_w_gate = nrm((L, N_EXPERTS, D, D_FF_EXPERT), D ** -0.5)
    exp_w_up = nrm((L, N_EXPERTS, D, D_FF_EXPERT), D ** -0.5)
    exp_w_down = nrm((L, N_EXPERTS, D_FF_EXPERT, D), D_FF_EXPERT ** -0.5)
    return {"x": x, "c": c, "positions": positions, "ada_w": ada_w, "ada_b": ada_b,
            "norm1_g": norm1_g, "w_in": w_in, "q_norm_g": q_norm_g, "k_norm_g": k_norm_g,
            "sc_conv_w": sc_conv_w, "g_v_norm_g": g_v_norm_g, "g_spatial_w": g_spatial_w,
            "g_spatial_b": g_spatial_b, "m_conv_w": m_conv_w, "m_conv_b": m_conv_b,
            "m_dt_bias": m_dt_bias, "m_a_log": m_a_log, "m_d": m_d, "m_norm_g": m_norm_g,
            "w_branch": w_branch, "w_out": w_out, "norm2_g": norm2_g, "router_w": router_w,
            "router_bias": router_bias, "exp_w_gate": exp_w_gate, "exp_w_up": exp_w_up,
            "exp_w_down": exp_w_down}


def reference(x, c, positions, ada_w, ada_b, norm1_g, w_in, q_norm_g, k_norm_g, sc_conv_w,
              g_v_norm_g, g_spatial_w, g_spatial_b, m_conv_w, m_conv_b, m_dt_bias, m_a_log, m_d,
              m_norm_g, w_branch, w_out, norm2_g, router_w, router_bias, exp_w_gate, exp_w_up,
              exp_w_down):
    c_act = jax.nn.silu(c)
    for l in range(DEPTH):
        mod = c_act @ ada_w[l] + ada_b[l]
        shift1, scale1, gate1, shift2, scale2, gate2 = jnp.split(mod, 6, axis=-1)
        h = rms_norm(x, norm1_g[l]) * (1.0 + scale1[:, None, :]) + shift1[:, None, :]
        mix = hybrid_mixer(h, positions, w_in[l], q_norm_g[l], k_norm_g[l], sc_conv_w[l],
                           g_v_norm_g[l], g_spatial_w[l], g_spatial_b[l], m_conv_w[l], m_conv_b[l],
                           m_dt_bias[l], m_a_log[l], m_d[l], m_norm_g[l], w_branch[l], w_out[l])
        x = x + gate1[:, None, :] * mix
        h = rms_norm(x, norm2_g[l]) * (1.0 + scale2[:, None, :]) + shift2[:, None, :]
        x = x + gate2[:, None, :] * grouped_moe(h, router_w, router_bias, exp_w_gate[l],
                                                exp_w_up[l], exp_w_down[l])
    return x
```

```python
import functools
import math

import jax
import jax.numpy as jnp
import numpy as np
from jax import lax
from jax.experimental import pallas as pl
from jax.experimental.pallas import tpu as pltpu

F32 = jnp.float32
BF16 = jnp.bfloat16
I32 = jnp.int32

D_MODEL = 1024
RMS_EPS = 1e-6
ROPE_THETA = 500000.0
Q_BLOCK = 128
A_HEADS = 8
A_KV_HEADS = 2
A_HEAD_DIM = 64
A_ROT = A_HEAD_DIM // 4
A_REP = A_HEADS // A_KV_HEADS
IDX_HEADS = 8
IDX_DIM = 32
IDX_ROT = IDX_DIM // 4
IDX_TOPK_MAX = 256
SC_W = 512
SC_KERNEL = 3
GM_W = 512
GM_GROUPS = 4
GM_GROUP_W = GM_W // GM_GROUPS
GM_CHUNK = 128
M_DINNER = 512
M_HEAD_DIM = 64
M_HEADS = M_DINNER // M_HEAD_DIM
M_GROUPS = 2
M_STATE = 64
M_CONV = 4
M_CHUNK = 128
M_BC = M_GROUPS * M_STATE
M_XBC = M_DINNER + 2 * M_BC
N_BRANCH = 4
BRANCH_W = 512
N_EXPERTS = 16
N_EXPERT_GROUPS = 4
EXPERTS_PER_GROUP = N_EXPERTS // N_EXPERT_GROUPS
D_FF_EXPERT = 512
A_Q = A_HEADS * A_HEAD_DIM
A_KV = A_KV_HEADS * A_HEAD_DIM
IDX_Q = IDX_HEADS * IDX_DIM

LANES = 128
SUBLANES = 8
VMEM_LIMIT = 56 * 1024 * 1024

P_GATES = 0
P_GUV = 4096
P_Q = 5120
P_SCB = 5632
P_SCC = 6144
P_SCX = 6656
P_MZ = 7168
P_XBC = 7680
P_K = 8448
P_V = 8576
P_COLS = 8704
S_QI = 0
S_KW = 256
S_DT = 384
S_COLS = 512

INT_MIN = -2147483648
NEG_INF = float("-inf")
NEG_INF_KEY = -2139095041


def _dot(a, b):
    return jnp.dot(a, b, preferred_element_type=F32)


def _dot_nt(a, b):
    return lax.dot_general(a, b, (((1,), (1,)), ((), ())), preferred_element_type=F32)


def _dot_f32(a, b):
    return jnp.dot(a, b, preferred_element_type=F32, precision=lax.Precision.HIGHEST)


def _split(x):
    hi = x.astype(BF16)
    lo = (x - hi.astype(F32)).astype(BF16)
    return hi, lo


def _dot3(a_hi, a_lo, b_hi, b_lo):
    return _dot(a_hi, b_hi) + _dot(a_lo, b_hi) + _dot(a_hi, b_lo)


def _sigmoid(x):
    return 1.0 / (1.0 + jnp.exp(-x))


def _silu(x):
    return x * _sigmoid(x)


def _softplus(x):
    return jnp.maximum(x, 0.0) + jnp.log1p(jnp.exp(-jnp.abs(x)))


def _gelu(x):
    return 0.5 * x * (1.0 + lax.erf(x * np.float32(np.sqrt(0.5))))


def _rms(x, g_row):
    ms = jnp.mean(x * x, axis=-1, keepdims=True)
    return x * lax.rsqrt(ms + RMS_EPS) * g_row


def _params(*sem):
    return pltpu.CompilerParams(dimension_semantics=sem, vmem_limit_bytes=VMEM_LIMIT)


def _mod_kernel(c_ref, w_ref, b_ref, o_ref):
    o_ref[0] = _dot_f32(_silu(c_ref[...]), w_ref[0]) + b_ref[0]


def _modulation(c, ada_w, ada_b):
    depth, d, n = ada_w.shape
    bsz = c.shape[0]
    tn = 768
    return pl.pallas_call(
        _mod_kernel,
        grid=(depth, n // tn),
        in_specs=[pl.BlockSpec((bsz, d), lambda l, j: (0, 0)),
                  pl.BlockSpec((1, d, tn), lambda l, j: (l, 0, j)),
                  pl.BlockSpec((1, 1, tn), lambda l, j: (l, 0, j))],
        out_specs=pl.BlockSpec((1, bsz, tn), lambda l, j: (l, 0, j)),
        out_shape=jax.ShapeDtypeStruct((depth, bsz, n), F32),
        compiler_params=_params("arbitrary", "arbitrary"),
        name="modulation",
    )(c, ada_w, ada_b.reshape(depth, 1, n))


def _inproj_kernel(x_ref, g_ref, sc_ref, sh_ref, w_ref, wsh_ref, wsl_ref, o_ref, os_ref, hh_ref, hl_ref):
    @pl.when(pl.program_id(1) == 0)
    def _():
        h = _rms(x_ref[...], g_ref[...]) * (1.0 + sc_ref[0]) + sh_ref[0]
        hh, hl = _split(h)
        hh_ref[...] = hh
        hl_ref[...] = hl
        os_ref[...] = _dot3(hh, hl, wsh_ref[...], wsl_ref[...])

    o_ref[...] = _dot(hh_ref[...], w_ref[...]).astype(BF16)


def _inproj(x2, seq, g, scale, shift, w_main, ws_hi, ws_lo):
    ntok, d = x2.shape
    tm = min(1024, seq)
    tn = 512
    bsz = scale.shape[0]
    row = lambda i, j: (i, 0)
    per_b = lambda i, j: ((i * tm) // seq, 0, 0)
    return pl.pallas_call(
        _inproj_kernel,
        grid=(ntok // tm, P_COLS // tn),
        in_specs=[pl.BlockSpec((tm, d), row),
                  pl.BlockSpec((1, d), lambda i, j: (0, 0)),
                  pl.BlockSpec((1, 1, d), per_b),
                  pl.BlockSpec((1, 1, d), per_b),
                  pl.BlockSpec((d, tn), lambda i, j: (0, j)),
                  pl.BlockSpec((d, S_COLS), lambda i, j: (0, 0)),
                  pl.BlockSpec((d, S_COLS), lambda i, j: (0, 0))],
        out_specs=[pl.BlockSpec((tm, tn), lambda i, j: (i, j)),
                   pl.BlockSpec((tm, S_COLS), row)],
        out_shape=[jax.ShapeDtypeStruct((ntok, P_COLS), BF16),
                   jax.ShapeDtypeStruct((ntok, S_COLS), F32)],
        scratch_shapes=[pltpu.VMEM((tm, d), BF16), pltpu.VMEM((tm, d), BF16)],
        compiler_params=_params("arbitrary", "arbitrary"),
        name="inproj",
    )(x2, g.reshape(1, d), scale.reshape(bsz, 1, d), shift.reshape(bsz, 1, d), w_main, ws_hi, ws_lo)


def _rope_tables(pos_f, period, rot, width):
    half = rot // 2
    lane = lax.broadcasted_iota(I32, (1, width), 1) % period
    k = (lane % half).astype(F32)
    inv_freq = jnp.exp(np.float32(-math.log(ROPE_THETA)) * k * 2.0 / rot)
    inv_freq = jnp.where(lane < rot, inv_freq, 0.0)
    ang = pos_f * inv_freq
    cos = jnp.cos(ang)
    sin = jnp.sin(ang)
    sin_lo = jnp.where(lane < half, -sin, 0.0)
    sin_hi = jnp.where((lane >= half) & (lane < rot), sin, 0.0)
    return cos, sin_lo, sin_hi


def _rope(x, tables, half):
    cos, sin_lo, sin_hi = tables
    width = x.shape[-1]
    return (x * cos + pltpu.roll(x, width - half, 1) * sin_lo + pltpu.roll(x, half, 1) * sin_hi)


def _prep_kernel(q_ref, k_ref, qi_ref, kw_ref, pos_ref, qg_ref, kg_ref, bdq_ref, bdk_ref,
                 pq_ref, pk_ref, pqi1_ref, pqi2_ref, pki1_ref, pki2_ref,
                 qo_ref, ko_ref, qio_ref, kio_ref):
    pos_f = pos_ref[...].astype(F32)
    att_tab = _rope_tables(pos_f, A_HEAD_DIM, A_ROT, LANES)
    idx_tab = _rope_tables(pos_f, IDX_DIM, IDX_ROT, LANES)
    kw_tab = _rope_tables(pos_f, LANES, IDX_ROT, LANES)

    def head_norm(x, bd_ref, g_ref):
        sq_hi, sq_lo = _split(x * x)
        ms = (_dot(sq_hi, bd_ref[...]) + _dot(sq_lo, bd_ref[...])) * np.float32(1.0 / A_HEAD_DIM)
        return x * lax.rsqrt(ms + RMS_EPS) * g_ref[...]

    def rope_wide(x, tab, half):
        pieces = [_rope(x[:, c * LANES:(c + 1) * LANES], tab, half) for c in range(x.shape[-1] // LANES)]
        return pieces[0] if len(pieces) == 1 else jnp.concatenate(pieces, axis=-1)

    q = rope_wide(head_norm(q_ref[...].astype(F32), bdq_ref, qg_ref), att_tab, A_ROT // 2)
    q = (q * np.float32(A_HEAD_DIM ** -0.5)).astype(BF16)
    qo_ref[...] = _dot(q, pq_ref[...]).astype(BF16)
    k = rope_wide(head_norm(k_ref[...].astype(F32), bdk_ref, kg_ref), att_tab, A_ROT // 2).astype(BF16)
    ko_ref[...] = _dot(k, pk_ref[...]).astype(BF16)

    qi_hi, qi_lo = _split(rope_wide(qi_ref[...], idx_tab, IDX_ROT // 2))
    qio_ref[...] = (_dot(qi_hi, pqi1_ref[...]) + _dot(qi_lo, pqi2_ref[...])).astype(BF16)
    ki_hi, ki_lo = _split(_rope(kw_ref[...], kw_tab, IDX_ROT // 2))
    kio_ref[...] = (_dot(ki_hi, pki1_ref[...]) + _dot(ki_lo, pki2_ref[...])).astype(BF16)


@functools.lru_cache(maxsize=None)
def _prep_constants():
    def block_diag(width):
        i = np.arange(width)
        return (i[:, None] // A_HEAD_DIM == i[None, :] // A_HEAD_DIM).astype(np.float32)

    def pad_heads(n_heads):
        p = np.zeros((n_heads * A_HEAD_DIM, n_heads * LANES), np.float32)
        for h in range(n_heads):
            for d in range(A_HEAD_DIM):
                p[h * A_HEAD_DIM + d, h * LANES + d] = 1.0
        return p

    pqi1 = np.zeros((IDX_Q, IDX_HEADS * LANES), np.float32)
    pqi2 = np.zeros((IDX_Q, IDX_HEADS * LANES), np.float32)
    for h in range(IDX_HEADS):
        for d in range(IDX_DIM):
            pqi1[h * IDX_DIM + d, h * LANES + d] = 1.0
            pqi1[h * IDX_DIM + d, h * LANES + IDX_DIM + d] = 1.0
            pqi2[h * IDX_DIM + d, h * LANES + 2 * IDX_DIM + d] = 1.0
    pki1 = np.zeros((LANES, LANES), np.float32)
    pki2 = np.zeros((LANES, LANES), np.float32)
    for d in range(IDX_DIM):
        pki1[d, d] = 1.0
        pki1[d, 2 * IDX_DIM + d] = 1.0
        pki2[d, IDX_DIM + d] = 1.0
    mats = (block_diag(A_Q), block_diag(A_KV), pad_heads(A_HEADS), pad_heads(A_KV_HEADS), pqi1, pqi2, pki1, pki2)
    return tuple(np.asarray(m) for m in mats)


def _prep(proj, small, pos2, seq, q_g, k_g):
    ntok = proj.shape[0]
    tm = min(512, seq)
    consts = [jnp.asarray(m, BF16) for m in _prep_constants()]
    row = lambda w, c: pl.BlockSpec((tm, w), lambda i: (i, c))
    full = lambda a: pl.BlockSpec(a.shape, lambda i: (0, 0))
    qg = jnp.tile(q_g, A_HEADS).reshape(1, A_Q)
    kg = jnp.tile(k_g, A_KV_HEADS).reshape(1, A_KV)
    return pl.pallas_call(
        _prep_kernel,
        grid=(ntok // tm,),
        in_specs=[row(A_Q, P_Q // A_Q), row(A_KV, P_K // A_KV), row(IDX_Q, S_QI // IDX_Q), row(LANES, S_KW // LANES),
                  row(1, 0), full(qg), full(kg)] + [full(m) for m in consts],
        out_specs=[row(A_HEADS * LANES, 0), row(A_KV_HEADS * LANES, 0), row(IDX_HEADS * LANES, 0), row(LANES, 0)],
        out_shape=[jax.ShapeDtypeStruct((ntok, A_HEADS * LANES), BF16),
                   jax.ShapeDtypeStruct((ntok, A_KV_HEADS * LANES), BF16),
                   jax.ShapeDtypeStruct((ntok, IDX_HEADS * LANES), BF16),
                   jax.ShapeDtypeStruct((ntok, LANES), BF16)],
        compiler_params=_params("arbitrary"),
        name="attn_prep",
    )(proj, proj, small, small, pos2, qg, kg, *consts)


def _sortable(x):
    bits = pltpu.bitcast(jnp.where(x == 0.0, 0.0, x), I32)
    return jnp.where(bits < 0, bits ^ jnp.int32(0x7FFFFFFF), bits)


def _count_rows(pred_fn, nchunk):
    def body(c, acc):
        ones = jnp.where(pred_fn(c), jnp.int32(1), jnp.int32(0))
        return acc + jnp.sum(ones.reshape(Q_BLOCK // SUBLANES, SUBLANES, LANES), axis=0)
    acc = lax.fori_loop(0, nchunk, body, jnp.zeros((SUBLANES, LANES), I32))
    return jnp.sum(acc, axis=0, keepdims=True)


def _dsa_kernel(qi_ref, kw_ref, q_ref, ki_ref, k_ref, v_ref, o_ref, key_ref, acc_ref, *, topk, idx_bits):
    qb = pl.program_id(1)
    nchunk = qb + 1
    row_i = lax.broadcasted_iota(I32, (Q_BLOCK, LANES), 0)
    lane_i = lax.broadcasted_iota(I32, (Q_BLOCK, LANES), 1)

    w_rows = jnp.transpose(kw_ref[...])[IDX_DIM:IDX_DIM + IDX_HEADS, :]
    w_rows = w_rows * np.float32(IDX_HEADS ** -0.5) * np.float32(IDX_DIM ** -0.5)

    def chunk_rows(c):
        return pl.ds(pl.multiple_of(c * Q_BLOCK, Q_BLOCK), Q_BLOCK)

    def causal(c):
        return (c * Q_BLOCK + row_i) <= (qb * Q_BLOCK + lane_i)

    def score_body(c, carry):
        ki = ki_ref[chunk_rows(c), :]
        s = jnp.zeros((Q_BLOCK, LANES), F32)
        for h in range(IDX_HEADS):
            d = _dot_nt(ki, qi_ref[:, h * LANES:(h + 1) * LANES])
            s = s + jnp.maximum(d, 0.0) * w_rows[h:h + 1, :]
        key_ref[chunk_rows(c), :] = _sortable(jnp.where(causal(c), s, NEG_INF))
        return carry
    lax.fori_loop(0, nchunk, score_body, 0)

    def bit_body(i, cand):
        trial = cand | lax.shift_left(jnp.int32(1), 31 - i)
        trial_s = trial ^ jnp.int32(INT_MIN)
        cnt = _count_rows(lambda c: key_ref[chunk_rows(c), :] >= trial_s, nchunk)
        return jnp.where(cnt >= topk, trial, cand)
    cand = lax.fori_loop(0, 32, bit_body, jnp.zeros((1, LANES), I32))
    thr = cand ^ jnp.int32(INT_MIN)

    n_gt = _count_rows(lambda c: key_ref[chunk_rows(c), :] > thr, nchunk)
    n_eq = _count_rows(lambda c: key_ref[chunk_rows(c), :] == thr, nchunk)
    need = topk - n_gt
    neg_inf_key = jnp.int32(NEG_INF_KEY)
    excess = (n_eq > need) & (thr > neg_inf_key)
    acc_ref[0:1, :] = jnp.full((1, LANES), 2 ** 30, I32)

    @pl.when(jnp.max(jnp.where(excess, 1, 0)) > 0)
    def _():
        def idx_body(i, lim):
            trial = lim | lax.shift_left(jnp.int32(1), idx_bits - 1 - i)
            cnt = _count_rows(
                lambda c: (key_ref[chunk_rows(c), :] == thr) & ((c * Q_BLOCK + row_i) < trial), nchunk)
            return jnp.where(cnt < need, trial, lim)
        lim = lax.fori_loop(0, idx_bits, idx_body, jnp.zeros((1, LANES), I32))
        acc_ref[0:1, :] = jnp.where(excess, lim, 2 ** 30)
    tie_lim = acc_ref[0:1, :]

    m0 = jnp.full((A_HEADS, LANES), -1e30, F32)
    l0 = jnp.zeros((A_HEADS, LANES), F32)
    o0 = jnp.zeros((A_HEADS * A_HEAD_DIM, LANES), F32)

    def attn_body(c, carry):
        m_all, l_all, o_all = carry
        keys = key_ref[chunk_rows(c), :]
        kpos = c * Q_BLOCK + row_i
        sel = ((keys > thr) | ((keys == thr) & (kpos <= tie_lim))) & causal(c)
        kc = k_ref[chunk_rows(c), :]
        vc = v_ref[chunk_rows(c), :]
        m_new, l_new, o_new = [], [], []
        for h in range(A_HEADS):
            g = h // A_REP
            s = _dot_nt(kc[:, g * LANES:(g + 1) * LANES], q_ref[:, h * LANES:(h + 1) * LANES])
            s = jnp.where(sel, s, NEG_INF)
            m_old = m_all[h:h + 1, :]
            m_h = jnp.maximum(m_old, jnp.max(s, axis=0, keepdims=True))
            alpha = jnp.exp(m_old - m_h)
            p = jnp.exp(s - m_h)
            l_new.append(l_all[h:h + 1, :] * alpha + jnp.sum(p, axis=0, keepdims=True))
            vg = vc[:, g * A_HEAD_DIM:(g + 1) * A_HEAD_DIM]
            pv = lax.dot_general(vg, p.astype(BF16), (((0,), (0,)), ((), ())), preferred_element_type=F32)
            o_new.append(o_all[h * A_HEAD_DIM:(h + 1) * A_HEAD_DIM, :] * alpha + pv)
            m_new.append(m_h)
        return (jnp.concatenate(m_new, axis=0), jnp.concatenate(l_new, axis=0), jnp.concatenate(o_new, axis=0))

    _, l_fin, o_fin = lax.fori_loop(0, nchunk, attn_body, (m0, l0, o0))
    outs = []
    for h in range(A_HEADS):
        outs.append(o_fin[h * A_HEAD_DIM:(h + 1) * A_HEAD_DIM, :] / l_fin[h:h + 1, :])
    o_ref[...] = jnp.transpose(jnp.concatenate(outs, axis=0)).astype(BF16)


def _dsa(qi_cat, small, q_pad, ki_cat, k_pad, proj, bsz, seq):
    nb = seq // Q_BLOCK
    topk = min(IDX_TOPK_MAX, seq // 4)
    idx_bits = max(1, int(math.ceil(math.log2(seq))))
    qrow = lambda w, c: pl.BlockSpec((Q_BLOCK, w), lambda b, j: (b * nb + j, c))
    brow = lambda w, c: pl.BlockSpec((seq, w), lambda b, j: (b, c))
    return pl.pallas_call(
        functools.partial(_dsa_kernel, topk=topk, idx_bits=idx_bits),
        grid=(bsz, nb),
        in_specs=[qrow(IDX_HEADS * LANES, 0), qrow(LANES, S_KW // LANES), qrow(A_HEADS * LANES, 0),
                  brow(LANES, 0), brow(A_KV_HEADS * LANES, 0), brow(A_KV, P_V // A_KV)],
        out_specs=qrow(A_Q, 0),
        out_shape=jax.ShapeDtypeStruct((bsz * seq, A_Q), BF16),
        scratch_shapes=[pltpu.VMEM((seq, LANES), I32), pltpu.VMEM((SUBLANES, LANES), I32)],
        compiler_params=_params("arbitrary", "arbitrary"),
        name="sparse_attention",
    )(qi_cat, small, q_pad, ki_cat, k_pad, proj)


HALO = SUBLANES


def _causal_conv(ext_ref, u, w_ref, taps, rows):
    ext_ref[HALO:HALO + rows, :] = u
    y = u * w_ref[taps - 1:taps, :]
    for j in range(taps - 1):
        shift = taps - 1 - j
        y = y + ext_ref[HALO - shift:HALO - shift + rows, :] * w_ref[j:j + 1, :]
    ext_ref[0:HALO, :] = ext_ref[rows:rows + HALO, :]
    return y


def _mix_kernel(x_ref, g0_ref, g1_ref, g2_ref, g3_ref, guv_ref, scb_ref, scc_ref, scx_ref, mz_ref, xbc_ref,
                dt_ref, oa_ref, gate_ref, scw_ref, gvg_ref, ws_ref, wsb_ref, mcw_ref, mcb_ref, dtb_ref, alog_ref,
                md_ref, mng_ref, wb_ref, wo_ref, o_ref, sc_ext, xbc_ext, state_ref, y_ref, *, rows):
    @pl.when(pl.program_id(1) == 0)
    def _():
        sc_ext[0:HALO, :] = jnp.zeros((HALO, SC_W), F32)
        xbc_ext[0:HALO, :] = jnp.zeros((HALO, M_XBC), F32)
        state_ref[...] = jnp.zeros(state_ref.shape, F32)

    nchunk = rows // M_CHUNK
    row_i = lax.broadcasted_iota(I32, (M_CHUNK, M_CHUNK), 0)
    col_i = lax.broadcasted_iota(I32, (M_CHUNK, M_CHUNK), 1)
    tri = row_i >= col_i

    o_b = scb_ref[...].astype(F32) * _causal_conv(
        sc_ext, scc_ref[...].astype(F32) * scx_ref[...].astype(F32), scw_ref, SC_KERNEL, rows)

    guv = _gelu(guv_ref[...].astype(F32))
    gu = guv[:, :GM_W]
    gv = _rms(guv[:, GM_W:], gvg_ref[...]).astype(BF16)
    mixed_rows = []
    for ci in range(nchunk):
        r0 = ci * GM_CHUNK
        cols = []
        for gi in range(GM_GROUPS):
            vb = gv[r0:r0 + GM_CHUNK, gi * GM_GROUP_W:(gi + 1) * GM_GROUP_W]
            cols.append(_dot(ws_ref[gi], vb) + wsb_ref[:, gi:gi + 1])
        mixed_rows.append(jnp.concatenate(cols, axis=-1))
    o_c = gu * jnp.concatenate(mixed_rows, axis=0)

    xbc = _silu(_causal_conv(xbc_ext, xbc_ref[...].astype(F32), mcw_ref, M_CONV, rows) + mcb_ref[...])
    head_lane = lax.broadcasted_iota(I32, (1, LANES), 1) < M_HEADS
    dt = jnp.where(head_lane, _softplus(dt_ref[...] + dtb_ref[...]), 0.0)
    a_row = jnp.where(head_lane, -jnp.exp(alog_ref[...]), 0.0)
    tri_f = jnp.where(tri, 1.0, 0.0).astype(F32)
    for ci in range(nchunk):
        r0 = ci * M_CHUNK
        dtc = dt[r0:r0 + M_CHUNK, :]
        acs = _dot_f32(tri_f, dtc * a_row)
        acs_t = jnp.transpose(acs)
        b_t = jnp.transpose(xbc[r0:r0 + M_CHUNK, M_DINNER:M_DINNER + M_BC]).astype(BF16)
        cm = xbc[r0:r0 + M_CHUNK, M_DINNER + M_BC:M_XBC].astype(BF16)
        for g in range(M_GROUPS):
            bg_t = b_t[g * M_STATE:(g + 1) * M_STATE, :]
            cg = cm[:, g * M_STATE:(g + 1) * M_STATE]
            cb = _dot(cg, bg_t)
            for h in range(g * (M_HEADS // M_GROUPS), (g + 1) * (M_HEADS // M_GROUPS)):
                col = acs[:, h:h + 1]
                row = acs_t[h:h + 1, :]
                last = acs[M_CHUNK - 1:M_CHUNK, h:h + 1]
                decay = jnp.exp(jnp.where(tri, col - row, NEG_INF))
                xs_h = xbc[r0:r0 + M_CHUNK, h * M_HEAD_DIM:(h + 1) * M_HEAD_DIM]
                xdt = xs_h * dtc[:, h:h + 1]
                y = _dot((cb * decay).astype(BF16), xdt.astype(BF16))
                prev_t = state_ref[h]
                y = y + _dot(cg, prev_t.astype(BF16)) * jnp.exp(col)
                st_t = _dot(bg_t, (xdt * jnp.exp(last - col)).astype(BF16))
                state_ref[h] = prev_t * jnp.exp(last) + st_t
                y_ref[r0:r0 + M_CHUNK, h * M_HEAD_DIM:(h + 1) * M_HEAD_DIM] = y + md_ref[:, h:h + 1] * xs_h
    o_d = _rms(y_ref[...] * _silu(mz_ref[...].astype(F32)), mng_ref[...])

    branches = (oa_ref[...], o_b.astype(BF16), o_c.astype(BF16), o_d.astype(BF16))
    gate_refs = (g0_ref, g1_ref, g2_ref, g3_ref)
    merged = None
    for i in range(N_BRANCH):
        term = _sigmoid(gate_refs[i][...].astype(F32)) * _dot(branches[i], wb_ref[i])
        merged = term if merged is None else merged + term
    o_ref[...] = x_ref[...] + gate_ref[0] * _dot(merged.astype(BF16), wo_ref[...])


def _mix(x2, proj, small, o_a, gate1, lw, bsz, seq):
    d = x2.shape[1]
    rows = min(256, seq)
    ns = seq // rows
    tok = lambda w, c: pl.BlockSpec((rows, w), lambda b, s: (b * ns + s, c))
    full = lambda a: pl.BlockSpec(a.shape, lambda b, s: (0,) * a.ndim)
    weights = [lw["sc_conv_w"], lw["g_v_norm_g"], lw["ws"], lw["ws_b"], lw["m_conv_w"], lw["m_conv_b"],
               lw["m_dt_bias"], lw["m_a_log"], lw["m_d"], lw["m_norm_g"], lw["w_branch"], lw["w_out"]]
    return pl.pallas_call(
        functools.partial(_mix_kernel, rows=rows),
        grid=(bsz, ns),
        in_specs=[tok(d, 0)] + [tok(D_MODEL, i) for i in range(N_BRANCH)]
        + [tok(2 * GM_W, P_GUV // (2 * GM_W)), tok(SC_W, P_SCB // SC_W), tok(SC_W, P_SCC // SC_W),
           tok(SC_W, P_SCX // SC_W), tok(M_DINNER, P_MZ // M_DINNER), tok(M_XBC, P_XBC // M_XBC),
           tok(LANES, S_DT // LANES), tok(A_Q, 0),
           pl.BlockSpec((1, 1, d), lambda b, s: (b, 0, 0))]
        + [full(w) for w in weights],
        out_specs=tok(d, 0),
        out_shape=jax.ShapeDtypeStruct(x2.shape, F32),
        scratch_shapes=[pltpu.VMEM((HALO + rows, SC_W), F32), pltpu.VMEM((HALO + rows, M_XBC), F32),
                        pltpu.VMEM((M_HEADS, M_STATE, M_HEAD_DIM), F32), pltpu.VMEM((rows, M_DINNER), F32)],
        compiler_params=_params("arbitrary", "arbitrary"),
        name="mixers_merge",
    )(x2, proj, proj, proj, proj, proj, proj, proj, proj, proj, proj, small, o_a,
      gate1.reshape(bsz, 1, d), *weights)


def _swap_lanes(x, lane, dist):
    up = pltpu.roll(x, LANES - dist, 1)
    down = pltpu.roll(x, dist, 1)
    return jnp.where((lane % (2 * dist)) < dist, up, down)


def _route(scores, bias_row):
    lane = lax.broadcasted_iota(I32, scores.shape, 1)
    valid = lane < N_EXPERTS
    sel = jnp.where(valid, scores + bias_row, NEG_INF)
    p1 = _swap_lanes(sel, lane, 1)
    hi1, lo1 = jnp.maximum(sel, p1), jnp.minimum(sel, p1)
    hi2, lo2 = _swap_lanes(hi1, lane, 2), _swap_lanes(lo1, lane, 2)
    grp = jnp.maximum(hi1, hi2) + jnp.maximum(jnp.minimum(hi1, hi2), jnp.maximum(lo1, lo2))
    grp = jnp.where(valid, grp, NEG_INF)
    best_val = jnp.max(grp, axis=-1, keepdims=True)
    big = jnp.int32(LANES)
    best = jnp.min(jnp.where(grp == best_val, lane // EXPERTS_PER_GROUP, big), axis=-1, keepdims=True)
    masked = jnp.where((lane // EXPERTS_PER_GROUP == best) & valid, sel, NEG_INF)
    v1 = jnp.max(masked, axis=-1, keepdims=True)
    i1 = jnp.min(jnp.where(masked == v1, lane, big), axis=-1, keepdims=True)
    masked2 = jnp.where(lane == i1, NEG_INF, masked)
    v2 = jnp.max(masked2, axis=-1, keepdims=True)
    i2 = jnp.min(jnp.where(masked2 == v2, lane, big), axis=-1, keepdims=True)
    w1 = jnp.sum(jnp.where(lane == i1, scores, 0.0), axis=-1, keepdims=True)
    w2 = jnp.sum(jnp.where(lane == i2, scores, 0.0), axis=-1, keepdims=True)
    tot = w1 + w2
    return jnp.where(lane == i1, w1 / tot, 0.0) + jnp.where(lane == i2, w2 / tot, 0.0)


def _moe_kernel(x_ref, g_ref, sc_ref, sh_ref, gate_ref, rwh_ref, rwl_ref, rb_ref, wgu_ref, wd_ref, o_ref,
                h_ref, comb_ref, acc_ref):
    e = pl.program_id(1)

    @pl.when(e == 0)
    def _():
        h = _rms(x_ref[...], g_ref[...]) * (1.0 + sc_ref[0]) + sh_ref[0]
        hh, hl = _split(h)
        h_ref[...] = hh
        logits = _dot3(hh, hl, rwh_ref[...], rwl_ref[...])
        comb_ref[...] = _route(_sigmoid(logits), rb_ref[...])
        acc_ref[...] = jnp.zeros(acc_ref.shape, F32)

    lane = lax.broadcasted_iota(I32, comb_ref.shape, 1)
    w_e = jnp.sum(jnp.where(lane == e, comb_ref[...], 0.0), axis=-1, keepdims=True)
    gu = _dot(h_ref[...], wgu_ref[0])
    hid = _silu(gu[:, :D_FF_EXPERT]) * gu[:, D_FF_EXPERT:]
    acc_ref[...] += w_e * _dot(hid.astype(BF16), wd_ref[0])

    @pl.when(e == pl.num_programs(1) - 1)
    def _():
        o_ref[...] = x_ref[...] + gate_ref[0] * acc_ref[...]


def _moe(x2, seq, g, scale, shift, gate, rw_hi, rw_lo, rb, wgu, wd):
    ntok, d = x2.shape
    bsz = scale.shape[0]
    tm = min(1024, seq)
    row = lambda i, e: (i, 0)
    per_b = lambda i, e: ((i * tm) // seq, 0, 0)
    vec = pl.BlockSpec((1, 1, d), per_b)
    const2 = lambda a: pl.BlockSpec(a.shape, lambda i, e: (0, 0))
    return pl.pallas_call(
        _moe_kernel,
        grid=(ntok // tm, N_EXPERTS),
        in_specs=[pl.BlockSpec((tm, d), row), pl.BlockSpec((1, d), lambda i, e: (0, 0)), vec, vec, vec,
                  const2(rw_hi), const2(rw_lo), const2(rb),
                  pl.BlockSpec((1, d, 2 * D_FF_EXPERT), lambda i, e: (e, 0, 0)),
                  pl.BlockSpec((1, D_FF_EXPERT, d), lambda i, e: (e, 0, 0))],
        out_specs=pl.BlockSpec((tm, d), row),
        out_shape=jax.ShapeDtypeStruct(x2.shape, F32),
        scratch_shapes=[pltpu.VMEM((tm, d), BF16), pltpu.VMEM((tm, LANES), F32), pltpu.VMEM((tm, d), F32)],
        compiler_params=_params("arbitrary", "arbitrary"),
        name="moe",
    )(x2, g.reshape(1, d), scale.reshape(bsz, 1, d), shift.reshape(bsz, 1, d), gate.reshape(bsz, 1, d),
      rw_hi, rw_lo, rb, wgu, wd)


def _pad_lanes(v, width=LANES):
    v = v.reshape(1, -1)
    return jnp.pad(v, ((0, 0), (0, width - v.shape[1])))


def _layer_weights(w_in, sc_conv_w, g_v_norm_g, g_spatial_w, g_spatial_b, m_conv_w, m_conv_b, m_dt_bias,
                   m_a_log, m_d, m_norm_g, w_branch, w_out):
    splits = (A_Q, A_KV, A_KV, IDX_Q, IDX_DIM, IDX_HEADS, SC_W, SC_W, SC_W, 2 * GM_W, M_DINNER, M_XBC, M_HEADS,
              N_BRANCH * D_MODEL)
    pts = [int(p) for p in np.cumsum(splits)[:-1]]
    (q, k, v, qi, ki, wi, sc_b, sc_c, sc_x, g_uv, m_z, m_xbc, m_dt, gates) = jnp.split(w_in, pts, axis=-1)
    d = w_in.shape[0]
    w_main = jnp.concatenate([gates, g_uv, q, sc_b, sc_c, sc_x, m_z, m_xbc, k, v], axis=-1).astype(BF16)
    w_small = jnp.concatenate(
        [qi, ki, wi, jnp.zeros((d, LANES - IDX_DIM - IDX_HEADS), F32), m_dt, jnp.zeros((d, LANES - M_HEADS), F32)],
        axis=-1)
    ws_hi = w_small.astype(BF16)
    ws_lo = (w_small - ws_hi.astype(F32)).astype(BF16)
    tri = np.tril(np.ones((GM_CHUNK, GM_CHUNK), dtype=bool))
    return dict(
        w_main=w_main, ws_hi=ws_hi, ws_lo=ws_lo,
        sc_conv_w=sc_conv_w, g_v_norm_g=g_v_norm_g.reshape(1, GM_W),
        ws=jnp.where(tri[None], g_spatial_w, 0.0).astype(BF16), ws_b=g_spatial_b.T,
        m_conv_w=m_conv_w, m_conv_b=m_conv_b.reshape(1, M_XBC),
        m_dt_bias=_pad_lanes(m_dt_bias), m_a_log=_pad_lanes(m_a_log), m_d=_pad_lanes(m_d),
        m_norm_g=m_norm_g.reshape(1, M_DINNER), w_branch=w_branch.astype(BF16), w_out=w_out.astype(BF16))


def kernel(x, c, positions, ada_w, ada_b, norm1_g, w_in, q_norm_g, k_norm_g, sc_conv_w, g_v_norm_g, g_spatial_w,
           g_spatial_b, m_conv_w, m_conv_b, m_dt_bias, m_a_log, m_d, m_norm_g, w_branch, w_out, norm2_g, router_w,
           router_bias, exp_w_gate, exp_w_up, exp_w_down):
    bsz, seq, d = x.shape
    depth = ada_w.shape[0]
    mod = _modulation(c, ada_w, ada_b)
    rw = jnp.pad(router_w, ((0, 0), (0, LANES - N_EXPERTS)))
    rw_hi = rw.astype(BF16)
    rw_lo = (rw - rw_hi.astype(F32)).astype(BF16)
    rb = _pad_lanes(router_bias)
    pos2 = positions.reshape(bsz * seq, 1)
    x2 = x.reshape(bsz * seq, d)
    for l in range(depth):
        shift1, scale1, gate1, shift2, scale2, gate2 = jnp.split(mod[l], 6, axis=-1)
        lw = _layer_weights(w_in[l], sc_conv_w[l], g_v_norm_g[l], g_spatial_w[l], g_spatial_b[l], m_conv_w[l],
                            m_conv_b[l], m_dt_bias[l], m_a_log[l], m_d[l], m_norm_g[l], w_branch[l], w_out[l])
        proj, small = _inproj(x2, seq, norm1_g[l], scale1, shift1, lw["w_main"], lw["ws_hi"], lw["ws_lo"])
        q_pad, k_pad, qi_cat, ki_cat = _prep(proj, small, pos2, seq, q_norm_g[l], k_norm_g[l])
        o_a = _dsa(qi_cat, small, q_pad, ki_cat, k_pad, proj, bsz, seq)
        x2 = _mix(x2, proj, small, o_a, gate1, lw, bsz, seq)
        wgu = jnp.concatenate([exp_w_gate[l], exp_w_up[l]], axis=-1).astype(BF16)
        x2 = _moe(x2, seq, norm2_g[l], scale2, shift2, gate2, rw_hi, rw_lo, rb, wgu, exp_w_down[l].astype(BF16))
    return x2.reshape(bsz, seq, d)
```

```python
import functools
import math

import jax
import jax.numpy as jnp
import numpy as np
from jax import lax
from jax.experimental import pallas as pl
from jax.experimental.pallas import tpu as pltpu

F32 = jnp.float32
BF16 = jnp.bfloat16
I32 = jnp.int32

D_MODEL = 1024
RMS_EPS = 1e-6
ROPE_THETA = 500000.0
Q_BLOCK = 128
A_HEADS = 8
A_KV_HEADS = 2
A_HEAD_DIM = 64
A_ROT = A_HEAD_DIM // 4
A_REP = A_HEADS // A_KV_HEADS
IDX_HEADS = 8
IDX_DIM = 32
IDX_ROT = IDX_DIM // 4
IDX_TOPK_MAX = 256
SC_W = 512
SC_KERNEL = 3
GM_W = 512
GM_GROUPS = 4
GM_GROUP_W = GM_W // GM_GROUPS
GM_CHUNK = 128
M_DINNER = 512
M_HEAD_DIM = 64
M_HEADS = M_DINNER // M_HEAD_DIM
M_GROUPS = 2
M_STATE = 64
M_CONV = 4
M_CHUNK = 128
M_BC = M_GROUPS * M_STATE
M_XBC = M_DINNER + 2 * M_BC
N_BRANCH = 4
BRANCH_W = 512
N_EXPERTS = 16
N_EXPERT_GROUPS = 4
EXPERTS_PER_GROUP = N_EXPERTS // N_EXPERT_GROUPS
D_FF_EXPERT = 512
A_Q = A_HEADS * A_HEAD_DIM
A_KV = A_KV_HEADS * A_HEAD_DIM
IDX_Q = IDX_HEADS * IDX_DIM

LANES = 128
SUBLANES = 8
VMEM_LIMIT = 56 * 1024 * 1024

P_GATES = 0
P_GUV = 4096
P_Q = 5120
P_SCB = 5632
P_SCC = 6144
P_SCX = 6656
P_MZ = 7168
P_XBC = 7680
P_K = 8448
P_V = 8576
P_COLS = 8704
S_QI = 0
S_KW = 256
S_DT = 384
S_COLS = 512

INT_MIN = -2147483648
NEG_INF = float("-inf")
NEG_INF_KEY = -2139095041


def _dot(a, b):
    return jnp.dot(a, b, preferred_element_type=F32)


def _dot_nt(a, b):
    return lax.dot_general(a, b, (((1,), (1,)), ((), ())), preferred_element_type=F32)


def _dot_f32(a, b):
    return jnp.dot(a, b, preferred_element_type=F32, precision=lax.Precision.HIGHEST)


def _split(x):
    hi = x.astype(BF16)
    lo = (x - hi.astype(F32)).astype(BF16)
    return hi, lo


def _dot3(a_hi, a_lo, b_hi, b_lo):
    return _dot(a_hi, b_hi) + _dot(a_lo, b_hi) + _dot(a_hi, b_lo)


def _sigmoid(x):
    return 1.0 / (1.0 + jnp.exp(-x))


def _silu(x):
    return x * _sigmoid(x)


def _softplus(x):
    return jnp.maximum(x, 0.0) + jnp.log1p(jnp.exp(-jnp.abs(x)))


def _gelu(x):
    return 0.5 * x * (1.0 + lax.erf(x * np.float32(np.sqrt(0.5))))


def _rms(x, g_row):
    ms = jnp.mean(x * x, axis=-1, keepdims=True)
    return x * lax.rsqrt(ms + RMS_EPS) * g_row


def _params(*sem):
    return pltpu.CompilerParams(dimension_semantics=sem, vmem_limit_bytes=VMEM_LIMIT)


def _mod_kernel(c_ref, w_ref, b_ref, o_ref):
    o_ref[0] = _dot_f32(_silu(c_ref[...]), w_ref[0]) + b_ref[0]


def _modulation(c, ada_w, ada_b):
    depth, d, n = ada_w.shape
    bsz = c.shape[0]
    tn = 768
    return pl.pallas_call(
        _mod_kernel,
        grid=(depth, n // tn),
        in_specs=[pl.BlockSpec((bsz, d), lambda l, j: (0, 0)),
                  pl.BlockSpec((1, d, tn), lambda l, j: (l, 0, j)),
                  pl.BlockSpec((1, 1, tn), lambda l, j: (l, 0, j))],
        out_specs=pl.BlockSpec((1, bsz, tn), lambda l, j: (l, 0, j)),
        out_shape=jax.ShapeDtypeStruct((depth, bsz, n), F32),
        compiler_params=_params("arbitrary", "arbitrary"),
        name="modulation",
    )(c, ada_w, ada_b.reshape(depth, 1, n))


def _inproj_kernel(x_ref, g_ref, sc_ref, sh_ref, w_ref, wsh_ref, wsl_ref, o_ref, os_ref, hh_ref, hl_ref):
    @pl.when(pl.program_id(1) == 0)
    def _():
        h = _rms(x_ref[...], g_ref[...]) * (1.0 + sc_ref[0]) + sh_ref[0]
        hh, hl = _split(h)
        hh_ref[...] = hh
        hl_ref[...] = hl
        os_ref[...] = _dot3(hh, hl, wsh_ref[...], wsl_ref[...])

    o_ref[...] = _dot(hh_ref[...], w_ref[...]).astype(BF16)


def _inproj(x2, seq, g, scale, shift, w_main, ws_hi, ws_lo):
    ntok, d = x2.shape
    tm = min(1024, seq)
    tn = 512
    bsz = scale.shape[0]
    row = lambda i, j: (i, 0)
    per_b = lambda i, j: ((i * tm) // seq, 0, 0)
    return pl.pallas_call(
        _inproj_kernel,
        grid=(ntok // tm, P_COLS // tn),
        in_specs=[pl.BlockSpec((tm, d), row),
                  pl.BlockSpec((1, d), lambda i, j: (0, 0)),
                  pl.BlockSpec((1, 1, d), per_b),
                  pl.BlockSpec((1, 1, d), per_b),
                  pl.BlockSpec((d, tn), lambda i, j: (0, j)),
                  pl.BlockSpec((d, S_COLS), lambda i, j: (0, 0)),
                  pl.BlockSpec((d, S_COLS), lambda i, j: (0, 0))],
        out_specs=[pl.BlockSpec((tm, tn), lambda i, j: (i, j)),
                   pl.BlockSpec((tm, S_COLS), row)],
        out_shape=[jax.ShapeDtypeStruct((ntok, P_COLS), BF16),
                   jax.ShapeDtypeStruct((ntok, S_COLS), F32)],
        scratch_shapes=[pltpu.VMEM((tm, d), BF16), pltpu.VMEM((tm, d), BF16)],
        compiler_params=_params("arbitrary", "arbitrary"),
        name="inproj",
    )(x2, g.reshape(1, d), scale.reshape(bsz, 1, d), shift.reshape(bsz, 1, d), w_main, ws_hi, ws_lo)


def _rope_tables(pos_f, period, rot, width):
    half = rot // 2
    lane = lax.broadcasted_iota(I32, (1, width), 1) % period
    k = (lane % half).astype(F32)
    inv_freq = jnp.exp(np.float32(-math.log(ROPE_THETA)) * k * 2.0 / rot)
    inv_freq = jnp.where(lane < rot, inv_freq, 0.0)
    ang = pos_f * inv_freq
    cos = jnp.cos(ang)
    sin = jnp.sin(ang)
    sin_lo = jnp.where(lane < half, -sin, 0.0)
    sin_hi = jnp.where((lane >= half) & (lane < rot), sin, 0.0)
    return cos, sin_lo, sin_hi


def _rope(x, tables, half):
    cos, sin_lo, sin_hi = tables
    width = x.shape[-1]
    return (x * cos + pltpu.roll(x, width - half, 1) * sin_lo + pltpu.roll(x, half, 1) * sin_hi)


def _prep_kernel(q_ref, k_ref, v_ref, qi_ref, kw_ref, pos_ref, qg_ref, kg_ref, bdq_ref, bdk_ref,
                 pq_ref, pk_ref, pqi1_ref, pqi2_ref, pki1_ref, pki2_ref,
                 qo_ref, ko_ref, vto_ref, qio_ref, kio_ref):
    pos_f = pos_ref[...].astype(F32)
    att_tab = _rope_tables(pos_f, A_HEAD_DIM, A_ROT, LANES)
    idx_tab = _rope_tables(pos_f, IDX_DIM, IDX_ROT, LANES)
    kw_tab = _rope_tables(pos_f, LANES, IDX_ROT, LANES)

    def head_norm(x, bd_ref, g_ref):
        sq_hi, sq_lo = _split(x * x)
        ms = (_dot(sq_hi, bd_ref[...]) + _dot(sq_lo, bd_ref[...])) * np.float32(1.0 / A_HEAD_DIM)
        return x * lax.rsqrt(ms + RMS_EPS) * g_ref[...]

    def rope_wide(x, tab, half):
        pieces = [_rope(x[:, c * LANES:(c + 1) * LANES], tab, half) for c in range(x.shape[-1] // LANES)]
        return pieces[0] if len(pieces) == 1 else jnp.concatenate(pieces, axis=-1)

    def store_head_rows(o_ref, x):
        for blk in range(x.shape[0] // Q_BLOCK):
            for h in range(x.shape[1] // LANES):
                r0 = (blk * (x.shape[1] // LANES) + h) * Q_BLOCK
                o_ref[r0:r0 + Q_BLOCK, :] = x[blk * Q_BLOCK:(blk + 1) * Q_BLOCK, h * LANES:(h + 1) * LANES]

    q = rope_wide(head_norm(q_ref[...].astype(F32), bdq_ref, qg_ref), att_tab, A_ROT // 2)
    q = (q * np.float32(A_HEAD_DIM ** -0.5)).astype(BF16)
    store_head_rows(qo_ref, _dot(q, pq_ref[...]).astype(BF16))
    k = rope_wide(head_norm(k_ref[...].astype(F32), bdk_ref, kg_ref), att_tab, A_ROT // 2).astype(BF16)
    ko_ref[...] = _dot(k, pk_ref[...]).astype(BF16)
    v_pad = _dot(v_ref[...], pk_ref[...])
    ones_lane = (lax.broadcasted_iota(I32, (1, A_KV_HEADS * LANES), 1) % LANES) == A_HEAD_DIM
    vto_ref[...] = jnp.transpose(jnp.where(ones_lane, 1.0, v_pad)).astype(BF16)

    qi_hi, qi_lo = _split(rope_wide(qi_ref[...], idx_tab, IDX_ROT // 2))
    store_head_rows(qio_ref, (_dot(qi_hi, pqi1_ref[...]) + _dot(qi_lo, pqi2_ref[...])).astype(BF16))
    ki_hi, ki_lo = _split(_rope(kw_ref[...], kw_tab, IDX_ROT // 2))
    kio_ref[...] = (_dot(ki_hi, pki1_ref[...]) + _dot(ki_lo, pki2_ref[...])).astype(BF16)


@functools.lru_cache(maxsize=None)
def _prep_constants():
    def block_diag(width):
        i = np.arange(width)
        return (i[:, None] // A_HEAD_DIM == i[None, :] // A_HEAD_DIM).astype(np.float32)

    def pad_heads(n_heads):
        p = np.zeros((n_heads * A_HEAD_DIM, n_heads * LANES), np.float32)
        for h in range(n_heads):
            for d in range(A_HEAD_DIM):
                p[h * A_HEAD_DIM + d, h * LANES + d] = 1.0
        return p

    pqi1 = np.zeros((IDX_Q, IDX_HEADS * LANES), np.float32)
    pqi2 = np.zeros((IDX_Q, IDX_HEADS * LANES), np.float32)
    for h in range(IDX_HEADS):
        for d in range(IDX_DIM):
            pqi1[h * IDX_DIM + d, h * LANES + d] = 1.0
            pqi1[h * IDX_DIM + d, h * LANES + IDX_DIM + d] = 1.0
            pqi2[h * IDX_DIM + d, h * LANES + 2 * IDX_DIM + d] = 1.0
    pki1 = np.zeros((LANES, LANES), np.float32)
    pki2 = np.zeros((LANES, LANES), np.float32)
    for d in range(IDX_DIM):
        pki1[d, d] = 1.0
        pki1[d, 2 * IDX_DIM + d] = 1.0
        pki2[d, IDX_DIM + d] = 1.0
    mats = (block_diag(A_Q), block_diag(A_KV), pad_heads(A_HEADS), pad_heads(A_KV_HEADS), pqi1, pqi2, pki1, pki2)
    return tuple(np.asarray(m) for m in mats)


def _prep(proj, small, pos2, seq, q_g, k_g):
    ntok = proj.shape[0]
    tm = min(512, seq)
    consts = [jnp.asarray(m, BF16) for m in _prep_constants()]
    row = lambda w, c: pl.BlockSpec((tm, w), lambda i: (i, c))
    full = lambda a: pl.BlockSpec(a.shape, lambda i: (0, 0))
    qg = jnp.tile(q_g, A_HEADS).reshape(1, A_Q)
    kg = jnp.tile(k_g, A_KV_HEADS).reshape(1, A_KV)
    return pl.pallas_call(
        _prep_kernel,
        grid=(ntok // tm,),
        in_specs=[row(A_Q, P_Q // A_Q), row(A_KV, P_K // A_KV), row(A_KV, P_V // A_KV), row(IDX_Q, S_QI // IDX_Q),
                  row(LANES, S_KW // LANES), row(1, 0), full(qg), full(kg)] + [full(m) for m in consts],
        out_specs=[pl.BlockSpec((tm * A_HEADS, LANES), lambda i: (i, 0)), row(A_KV_HEADS * LANES, 0),
                   pl.BlockSpec((A_KV_HEADS * LANES, tm), lambda i: (0, i)),
                   pl.BlockSpec((tm * IDX_HEADS, LANES), lambda i: (i, 0)), row(LANES, 0)],
        out_shape=[jax.ShapeDtypeStruct((ntok * A_HEADS, LANES), BF16),
                   jax.ShapeDtypeStruct((ntok, A_KV_HEADS * LANES), BF16),
                   jax.ShapeDtypeStruct((A_KV_HEADS * LANES, ntok), BF16),
                   jax.ShapeDtypeStruct((ntok * IDX_HEADS, LANES), BF16),
                   jax.ShapeDtypeStruct((ntok, LANES), BF16)],
        compiler_params=_params("arbitrary"),
        name="attn_prep",
    )(proj, proj, proj, small, small, pos2, qg, kg, *consts)


def _sortable(x):
    bits = pltpu.bitcast(jnp.where(x == 0.0, 0.0, x), I32)
    return jnp.where(bits < 0, bits ^ jnp.int32(0x7FFFFFFF), bits)


KEY_STEP = 512
KEY_SUBS = KEY_STEP // Q_BLOCK


def _key_row0(step, sub):
    return pl.multiple_of(step * KEY_STEP + sub * Q_BLOCK, Q_BLOCK)


def _count_rows(pred_fn, nstep):
    def body(c, acc):
        for sub in range(KEY_SUBS):
            ones = jnp.where(pred_fn(_key_row0(c, sub)), jnp.int32(1), jnp.int32(0))
            acc = acc + jnp.sum(ones.reshape(Q_BLOCK // SUBLANES, SUBLANES, LANES), axis=0)
        return acc
    acc = lax.fori_loop(0, nstep, body, jnp.zeros((SUBLANES, LANES), I32))
    return jnp.sum(acc, axis=0, keepdims=True)


PAIR = 2 * LANES
N_PAIRS = A_HEADS // 2


def _dsa_kernel(qi_ref, kw_ref, q_ref, ki_ref, k_ref, vt_ref, o_ref, key_ref, s_ref, lim_ref, oacc_ref,
                *, topk, idx_bits):
    qb = pl.program_id(1)
    nstep = lax.div(qb, jnp.int32(KEY_SUBS)) + 1
    row_i = lax.broadcasted_iota(I32, (Q_BLOCK, LANES), 0)
    q_pos = qb * Q_BLOCK + lax.broadcasted_iota(I32, (Q_BLOCK, LANES), 1)

    w_rows = jnp.transpose(kw_ref[...])[IDX_DIM:IDX_DIM + IDX_HEADS, :]
    w_rows = w_rows * np.float32(IDX_HEADS ** -0.5) * np.float32(IDX_DIM ** -0.5)

    def rows(r0):
        return pl.ds(r0, Q_BLOCK)

    def score_body(c, carry):
        for sub in range(KEY_SUBS):
            r0 = _key_row0(c, sub)
            ki = ki_ref[rows(r0), :]
            s = jnp.zeros((Q_BLOCK, LANES), F32)
            for pair in range(IDX_HEADS // 2):
                d = _dot_nt(ki, qi_ref[pair * PAIR:(pair + 1) * PAIR, :])
                for j in range(2):
                    h = 2 * pair + j
                    s = s + jnp.maximum(d[:, j * LANES:(j + 1) * LANES], 0.0) * w_rows[h:h + 1, :]
            key_ref[rows(r0), :] = _sortable(jnp.where((r0 + row_i) <= q_pos, s, NEG_INF))
        return carry
    lax.fori_loop(0, nstep, score_body, 0)

    def bit_body(i, cand):
        trial = cand | lax.shift_left(jnp.int32(1), 31 - i)
        trial_s = trial ^ jnp.int32(INT_MIN)
        cnt = _count_rows(lambda r0: key_ref[rows(r0), :] >= trial_s, nstep)
        return jnp.where(cnt >= topk, trial, cand)
    cand = lax.fori_loop(0, 32, bit_body, jnp.zeros((1, LANES), I32))
    thr = cand ^ jnp.int32(INT_MIN)

    n_gt = _count_rows(lambda r0: key_ref[rows(r0), :] > thr, nstep)
    n_eq = _count_rows(lambda r0: key_ref[rows(r0), :] == thr, nstep)
    need = topk - n_gt
    excess = (n_eq > need) & (thr > jnp.int32(NEG_INF_KEY))
    lim_ref[0:1, :] = jnp.full((1, LANES), 2 ** 30, I32)

    @pl.when(jnp.max(jnp.where(excess, 1, 0)) > 0)
    def _():
        def idx_body(i, lim):
            trial = lim | lax.shift_left(jnp.int32(1), idx_bits - 1 - i)
            cnt = _count_rows(lambda r0: (key_ref[rows(r0), :] == thr) & ((r0 + row_i) < trial), nstep)
            return jnp.where(cnt < need, trial, lim)
        lim = lax.fori_loop(0, idx_bits, idx_body, jnp.zeros((1, LANES), I32))
        lim_ref[0:1, :] = jnp.where(excess, lim, 2 ** 30)
    tie_lim = lim_ref[0:1, :]

    def max_body(c, m_all):
        m_all = list(m_all)
        for sub in range(KEY_SUBS):
            r0 = _key_row0(c, sub)
            keys = key_ref[rows(r0), :]
            kpos = r0 + row_i
            sel = ((keys > thr) | ((keys == thr) & (kpos <= tie_lim))) & (kpos <= q_pos)
            bias = jnp.where(sel, 0.0, NEG_INF)
            bias2 = jnp.concatenate([bias, bias], axis=-1)
            for pair in range(N_PAIRS):
                g = (2 * pair) // A_REP
                s = _dot_nt(k_ref[rows(r0), g * LANES:(g + 1) * LANES], q_ref[pair * PAIR:(pair + 1) * PAIR, :]) + bias2
                s_ref[rows(r0), pair * PAIR:(pair + 1) * PAIR] = s
                m_all[pair] = jnp.maximum(m_all[pair], jnp.max(s, axis=0, keepdims=True))
        return tuple(m_all)
    m_fin = lax.fori_loop(0, nstep, max_body, tuple(jnp.full((1, PAIR), -1e30, F32) for _ in range(N_PAIRS)))

    oacc_ref[...] = jnp.zeros(oacc_ref.shape, F32)

    def pv_body(c, carry):
        cols = pl.ds(pl.multiple_of(c * KEY_STEP, KEY_STEP), KEY_STEP)
        for pair in range(N_PAIRS):
            g = (2 * pair) // A_REP
            p = jnp.exp(s_ref[cols, pair * PAIR:(pair + 1) * PAIR] - m_fin[pair]).astype(BF16)
            oacc_ref[pair] += _dot(vt_ref[g * LANES:(g + 1) * LANES, cols], p)
        return carry
    lax.fori_loop(0, nstep, pv_body, 0)

    outs = []
    for pair in range(N_PAIRS):
        o_p = oacc_ref[pair]
        for j in range(2):
            o_h = o_p[:, j * LANES:(j + 1) * LANES]
            outs.append(o_h[0:A_HEAD_DIM, :] / o_h[A_HEAD_DIM:A_HEAD_DIM + 1, :])
    o_ref[...] = jnp.transpose(jnp.concatenate(outs, axis=0)).astype(BF16)


def _dsa(qi_cat, small, q_pad, ki_cat, k_pad, v_t, bsz, seq):
    assert seq % KEY_STEP == 0
    nb = seq // Q_BLOCK
    topk = min(IDX_TOPK_MAX, seq // 4)
    idx_bits = max(1, int(math.ceil(math.log2(seq))))
    qrow = lambda w, c: pl.BlockSpec((Q_BLOCK, w), lambda b, j: (b * nb + j, c))
    brow = lambda w: pl.BlockSpec((seq, w), lambda b, j: (b, 0))
    return pl.pallas_call(
        functools.partial(_dsa_kernel, topk=topk, idx_bits=idx_bits),
        grid=(bsz, nb),
        in_specs=[pl.BlockSpec((IDX_HEADS * Q_BLOCK, LANES), lambda b, j: (b * nb + j, 0)),
                  qrow(LANES, S_KW // LANES),
                  pl.BlockSpec((A_HEADS * Q_BLOCK, LANES), lambda b, j: (b * nb + j, 0)),
                  brow(LANES), brow(A_KV_HEADS * LANES),
                  pl.BlockSpec((A_KV_HEADS * LANES, seq), lambda b, j: (0, b))],
        out_specs=qrow(A_Q, 0),
        out_shape=jax.ShapeDtypeStruct((bsz * seq, A_Q), BF16),
        scratch_shapes=[pltpu.VMEM((seq, LANES), I32), pltpu.VMEM((seq, A_HEADS * LANES), F32),
                        pltpu.VMEM((SUBLANES, LANES), I32), pltpu.VMEM((N_PAIRS, LANES, PAIR), F32)],
        compiler_params=_params("arbitrary", "arbitrary"),
        name="sparse_attention",
    )(qi_cat, small, q_pad, ki_cat, k_pad, v_t)


HALO = SUBLANES


def _causal_conv(ext_ref, u, w_ref, taps, rows):
    ext_ref[HALO:HALO + rows, :] = u
    y = u * w_ref[taps - 1:taps, :]
    for j in range(taps - 1):
        shift = taps - 1 - j
        y = y + ext_ref[HALO - shift:HALO - shift + rows, :] * w_ref[j:j + 1, :]
    ext_ref[0:HALO, :] = ext_ref[rows:rows + HALO, :]
    return y


def _mix_kernel(x_ref, g0_ref, g1_ref, g2_ref, g3_ref, guv_ref, scb_ref, scc_ref, scx_ref, mz_ref, xbc_ref,
                dt_ref, oa_ref, gate_ref, scw_ref, gvg_ref, ws_ref, wsb_ref, mcw_ref, mcb_ref, dtb_ref, alog_ref,
                md_ref, mng_ref, wb_ref, wo_ref, o_ref, sc_ext, xbc_ext, state_ref, y_ref, *, rows):
    @pl.when(pl.program_id(1) == 0)
    def _():
        sc_ext[0:HALO, :] = jnp.zeros((HALO, SC_W), F32)
        xbc_ext[0:HALO, :] = jnp.zeros((HALO, M_XBC), F32)
        state_ref[...] = jnp.zeros(state_ref.shape, F32)

    nchunk = rows // M_CHUNK
    row_i = lax.broadcasted_iota(I32, (M_CHUNK, M_CHUNK), 0)
    col_i = lax.broadcasted_iota(I32, (M_CHUNK, M_CHUNK), 1)
    tri = row_i >= col_i

    o_b = scb_ref[...].astype(F32) * _causal_conv(
        sc_ext, scc_ref[...].astype(F32) * scx_ref[...].astype(F32), scw_ref, SC_KERNEL, rows)

    guv = _gelu(guv_ref[...].astype(F32))
    gu = guv[:, :GM_W]
    gv = _rms(guv[:, GM_W:], gvg_ref[...]).astype(BF16)
    mixed_rows = []
    for ci in range(nchunk):
        r0 = ci * GM_CHUNK
        cols = []
        for gi in range(GM_GROUPS):
            vb = gv[r0:r0 + GM_CHUNK, gi * GM_GROUP_W:(gi + 1) * GM_GROUP_W]
            cols.append(_dot(ws_ref[gi], vb) + wsb_ref[:, gi:gi + 1])
        mixed_rows.append(jnp.concatenate(cols, axis=-1))
    o_c = gu * jnp.concatenate(mixed_rows, axis=0)

    xbc = _silu(_causal_conv(xbc_ext, xbc_ref[...].astype(F32), mcw_ref, M_CONV, rows) + mcb_ref[...])
    head_lane = lax.broadcasted_iota(I32, (1, LANES), 1) < M_HEADS
    dt = jnp.where(head_lane, _softplus(dt_ref[...] + dtb_ref[...]), 0.0)
    a_row = jnp.where(head_lane, -jnp.exp(alog_ref[...]), 0.0)
    tri_f = jnp.where(tri, 1.0, 0.0).astype(F32)
    for ci in range(nchunk):
        r0 = ci * M_CHUNK
        dtc = dt[r0:r0 + M_CHUNK, :]
        acs = _dot_f32(tri_f, dtc * a_row)
        acs_t = jnp.transpose(acs)
        b_t = jnp.transpose(xbc[r0:r0 + M_CHUNK, M_DINNER:M_DINNER + M_BC]).astype(BF16)
        cm = xbc[r0:r0 + M_CHUNK, M_DINNER + M_BC:M_XBC].astype(BF16)
        for g in range(M_GROUPS):
            bg_t = b_t[g * M_STATE:(g + 1) * M_STATE, :]
            cg = cm[:, g * M_STATE:(g + 1) * M_STATE]
            cb = _dot(cg, bg_t)
            for h in range(g * (M_HEADS // M_GROUPS), (g + 1) * (M_HEADS // M_GROUPS)):
                col = acs[:, h:h + 1]
                row = acs_t[h:h + 1, :]
                last = acs[M_CHUNK - 1:M_CHUNK, h:h + 1]
                decay = jnp.exp(jnp.where(tri, col - row, NEG_INF))
                xs_h = xbc[r0:r0 + M_CHUNK, h * M_HEAD_DIM:(h + 1) * M_HEAD_DIM]
                xdt = xs_h * dtc[:, h:h + 1]
                y = _dot((cb * decay).astype(BF16), xdt.astype(BF16))
                prev_t = state_ref[h]
                y = y + _dot(cg, prev_t.astype(BF16)) * jnp.exp(col)
                st_t = _dot(bg_t, (xdt * jnp.exp(last - col)).astype(BF16))
                state_ref[h] = prev_t * jnp.exp(last) + st_t
                y_ref[r0:r0 + M_CHUNK, h * M_HEAD_DIM:(h + 1) * M_HEAD_DIM] = y + md_ref[:, h:h + 1] * xs_h
    o_d = _rms(y_ref[...] * _silu(mz_ref[...].astype(F32)), mng_ref[...])

    branches = (oa_ref[...], o_b.astype(BF16), o_c.astype(BF16), o_d.astype(BF16))
    gate_refs = (g0_ref, g1_ref, g2_ref, g3_ref)
    merged = None
    for i in range(N_BRANCH):
        term = _sigmoid(gate_refs[i][...].astype(F32)) * _dot(branches[i], wb_ref[i])
        merged = term if merged is None else merged + term
    o_ref[...] = x_ref[...] + gate_ref[0] * _dot(merged.astype(BF16), wo_ref[...])


def _mix(x2, proj, small, o_a, gate1, lw, bsz, seq):
    d = x2.shape[1]
    rows = min(256, seq)
    ns = seq // rows
    tok = lambda w, c: pl.BlockSpec((rows, w), lambda b, s: (b * ns + s, c))
    full = lambda a: pl.BlockSpec(a.shape, lambda b, s: (0,) * a.ndim)
    weights = [lw["sc_conv_w"], lw["g_v_norm_g"], lw["ws"], lw["ws_b"], lw["m_conv_w"], lw["m_conv_b"],
               lw["m_dt_bias"], lw["m_a_log"], lw["m_d"], lw["m_norm_g"], lw["w_branch"], lw["w_out"]]
    return pl.pallas_call(
        functools.partial(_mix_kernel, rows=rows),
        grid=(bsz, ns),
        in_specs=[tok(d, 0)] + [tok(D_MODEL, i) for i in range(N_BRANCH)]
        + [tok(2 * GM_W, P_GUV // (2 * GM_W)), tok(SC_W, P_SCB // SC_W), tok(SC_W, P_SCC // SC_W),
           tok(SC_W, P_SCX // SC_W), tok(M_DINNER, P_MZ // M_DINNER), tok(M_XBC, P_XBC // M_XBC),
           tok(LANES, S_DT // LANES), tok(A_Q, 0),
           pl.BlockSpec((1, 1, d), lambda b, s: (b, 0, 0))]
        + [full(w) for w in weights],
        out_specs=tok(d, 0),
        out_shape=jax.ShapeDtypeStruct(x2.shape, F32),
        scratch_shapes=[pltpu.VMEM((HALO + rows, SC_W), F32), pltpu.VMEM((HALO + rows, M_XBC), F32),
                        pltpu.VMEM((M_HEADS, M_STATE, M_HEAD_DIM), F32), pltpu.VMEM((rows, M_DINNER), F32)],
        compiler_params=_params("arbitrary", "arbitrary"),
        name="mixers_merge",
    )(x2, proj, proj, proj, proj, proj, proj, proj, proj, proj, proj, small, o_a,
      gate1.reshape(bsz, 1, d), *weights)


def _swap_lanes(x, lane, dist):
    up = pltpu.roll(x, LANES - dist, 1)
    down = pltpu.roll(x, dist, 1)
    return jnp.where((lane % (2 * dist)) < dist, up, down)


def _route(scores, bias_row):
    lane = lax.broadcasted_iota(I32, scores.shape, 1)
    valid = lane < N_EXPERTS
    sel = jnp.where(valid, scores + bias_row, NEG_INF)
    p1 = _swap_lanes(sel, lane, 1)
    hi1, lo1 = jnp.maximum(sel, p1), jnp.minimum(sel, p1)
    hi2, lo2 = _swap_lanes(hi1, lane, 2), _swap_lanes(lo1, lane, 2)
    grp = jnp.maximum(hi1, hi2) + jnp.maximum(jnp.minimum(hi1, hi2), jnp.maximum(lo1, lo2))
    grp = jnp.where(valid, grp, NEG_INF)
    best_val = jnp.max(grp, axis=-1, keepdims=True)
    big = jnp.int32(LANES)
    best = jnp.min(jnp.where(grp == best_val, lane // EXPERTS_PER_GROUP, big), axis=-1, keepdims=True)
    masked = jnp.where((lane // EXPERTS_PER_GROUP == best) & valid, sel, NEG_INF)
    v1 = jnp.max(masked, axis=-1, keepdims=True)
    i1 = jnp.min(jnp.where(masked == v1, lane, big), axis=-1, keepdims=True)
    masked2 = jnp.where(lane == i1, NEG_INF, masked)
    v2 = jnp.max(masked2, axis=-1, keepdims=True)
    i2 = jnp.min(jnp.where(masked2 == v2, lane, big), axis=-1, keepdims=True)
    w1 = jnp.sum(jnp.where(lane == i1, scores, 0.0), axis=-1, keepdims=True)
    w2 = jnp.sum(jnp.where(lane == i2, scores, 0.0), axis=-1, keepdims=True)
    tot = w1 + w2
    return jnp.where(lane == i1, w1 / tot, 0.0) + jnp.where(lane == i2, w2 / tot, 0.0)


def _moe_kernel(x_ref, g_ref, sc_ref, sh_ref, gate_ref, rwh_ref, rwl_ref, rb_ref, wgu_ref, wd_ref, o_ref,
                h_ref, comb_ref, acc_ref):
    e = pl.program_id(1)

    @pl.when(e == 0)
    def _():
        h = _rms(x_ref[...], g_ref[...]) * (1.0 + sc_ref[0]) + sh_ref[0]
        hh, hl = _split(h)
        h_ref[...] = hh
        logits = _dot3(hh, hl, rwh_ref[...], rwl_ref[...])
        comb_ref[...] = _route(_sigmoid(logits), rb_ref[...])
        acc_ref[...] = jnp.zeros(acc_ref.shape, F32)

    lane = lax.broadcasted_iota(I32, comb_ref.shape, 1)
    w_e = jnp.sum(jnp.where(lane == e, comb_ref[...], 0.0), axis=-1, keepdims=True)
    gu = _dot(h_ref[...], wgu_ref[0])
    hid = _silu(gu[:, :D_FF_EXPERT]) * gu[:, D_FF_EXPERT:]
    acc_ref[...] += w_e * _dot(hid.astype(BF16), wd_ref[0])

    @pl.when(e == pl.num_programs(1) - 1)
    def _():
        o_ref[...] = x_ref[...] + gate_ref[0] * acc_ref[...]


def _moe(x2, seq, g, scale, shift, gate, rw_hi, rw_lo, rb, wgu, wd):
    ntok, d = x2.shape
    bsz = scale.shape[0]
    tm = min(1024, seq)
    row = lambda i, e: (i, 0)
    per_b = lambda i, e: ((i * tm) // seq, 0, 0)
    vec = pl.BlockSpec((1, 1, d), per_b)
    const2 = lambda a: pl.BlockSpec(a.shape, lambda i, e: (0, 0))
    return pl.pallas_call(
        _moe_kernel,
        grid=(ntok // tm, N_EXPERTS),
        in_specs=[pl.BlockSpec((tm, d), row), pl.BlockSpec((1, d), lambda i, e: (0, 0)), vec, vec, vec,
                  const2(rw_hi), const2(rw_lo), const2(rb),
                  pl.BlockSpec((1, d, 2 * D_FF_EXPERT), lambda i, e: (e, 0, 0)),
                  pl.BlockSpec((1, D_FF_EXPERT, d), lambda i, e: (e, 0, 0))],
        out_specs=pl.BlockSpec((tm, d), row),
        out_shape=jax.ShapeDtypeStruct(x2.shape, F32),
        scratch_shapes=[pltpu.VMEM((tm, d), BF16), pltpu.VMEM((tm, LANES), F32), pltpu.VMEM((tm, d), F32)],
        compiler_params=_params("arbitrary", "arbitrary"),
        name="moe",
    )(x2, g.reshape(1, d), scale.reshape(bsz, 1, d), shift.reshape(bsz, 1, d), gate.reshape(bsz, 1, d),
      rw_hi, rw_lo, rb, wgu, wd)


def _pad_lanes(v, width=LANES):
    v = v.reshape(1, -1)
    return jnp.pad(v, ((0, 0), (0, width - v.shape[1])))


def _layer_weights(w_in, sc_conv_w, g_v_norm_g, g_spatial_w, g_spatial_b, m_conv_w, m_conv_b, m_dt_bias,
                   m_a_log, m_d, m_norm_g, w_branch, w_out):
    splits = (A_Q, A_KV, A_KV, IDX_Q, IDX_DIM, IDX_HEADS, SC_W, SC_W, SC_W, 2 * GM_W, M_DINNER, M_XBC, M_HEADS,
              N_BRANCH * D_MODEL)
    pts = [int(p) for p in np.cumsum(splits)[:-1]]
    (q, k, v, qi, ki, wi, sc_b, sc_c, sc_x, g_uv, m_z, m_xbc, m_dt, gates) = jnp.split(w_in, pts, axis=-1)
    d = w_in.shape[0]
    w_main = jnp.concatenate([gates, g_uv, q, sc_b, sc_c, sc_x, m_z, m_xbc, k, v], axis=-1).astype(BF16)
    w_small = jnp.concatenate(
        [qi, ki, wi, jnp.zeros((d, LANES - IDX_DIM - IDX_HEADS), F32), m_dt, jnp.zeros((d, LANES - M_HEADS), F32)],
        axis=-1)
    ws_hi = w_small.astype(BF16)
    ws_lo = (w_small - ws_hi.astype(F32)).astype(BF16)
    tri = np.tril(np.ones((GM_CHUNK, GM_CHUNK), dtype=bool))
    return dict(
        w_main=w_main, ws_hi=ws_hi, ws_lo=ws_lo,
        sc_conv_w=sc_conv_w, g_v_norm_g=g_v_norm_g.reshape(1, GM_W),
        ws=jnp.where(tri[None], g_spatial_w, 0.0).astype(BF16), ws_b=g_spatial_b.T,
        m_conv_w=m_conv_w, m_conv_b=m_conv_b.reshape(1, M_XBC),
        m_dt_bias=_pad_lanes(m_dt_bias), m_a_log=_pad_lanes(m_a_log), m_d=_pad_lanes(m_d),
        m_norm_g=m_norm_g.reshape(1, M_DINNER), w_branch=w_branch.astype(BF16), w_out=w_out.astype(BF16))


def kernel(x, c, positions, ada_w, ada_b, norm1_g, w_in, q_norm_g, k_norm_g, sc_conv_w, g_v_norm_g, g_spatial_w,
           g_spatial_b, m_conv_w, m_conv_b, m_dt_bias, m_a_log, m_d, m_norm_g, w_branch, w_out, norm2_g, router_w,
           router_bias, exp_w_gate, exp_w_up, exp_w_down):
    bsz, seq, d = x.shape
    depth = ada_w.shape[0]
    mod = _modulation(c, ada_w, ada_b)
    rw = jnp.pad(router_w, ((0, 0), (0, LANES - N_EXPERTS)))
    rw_hi = rw.astype(BF16)
    rw_lo = (rw - rw_hi.astype(F32)).astype(BF16)
    rb = _pad_lanes(router_bias)
    pos2 = positions.reshape(bsz * seq, 1)
    x2 = x.reshape(bsz * seq, d)
    for l in range(depth):
        shift1, scale1, gate1, shift2, scale2, gate2 = jnp.split(mod[l], 6, axis=-1)
        lw = _layer_weights(w_in[l], sc_conv_w[l], g_v_norm_g[l], g_spatial_w[l], g_spatial_b[l], m_conv_w[l],
                            m_conv_b[l], m_dt_bias[l], m_a_log[l], m_d[l], m_norm_g[l], w_branch[l], w_out[l])
        proj, small = _inproj(x2, seq, norm1_g[l], scale1, shift1, lw["w_main"], lw["ws_hi"], lw["ws_lo"])
        q_pad, k_pad, v_t, qi_cat, ki_cat = _prep(proj, small, pos2, seq, q_norm_g[l], k_norm_g[l])
        o_a = _dsa(qi_cat, small, q_pad, ki_cat, k_pad, v_t, bsz, seq)
        x2 = _mix(x2, proj, small, o_a, gate1, lw, bsz, seq)
        wgu = jnp.concatenate([exp_w_gate[l], exp_w_up[l]], axis=-1).astype(BF16)
        x2 = _moe(x2, seq, norm2_g[l], scale2, shift2, gate2, rw_hi, rw_lo, rb, wgu, exp_w_down[l].astype(BF16))
    return x2.reshape(bsz, seq, d)
```

```python
import functools
import math

import jax
import jax.numpy as jnp
import numpy as np
from jax import lax
from jax.experimental import pallas as pl
from jax.experimental.pallas import tpu as pltpu

F32 = jnp.float32
BF16 = jnp.bfloat16
I32 = jnp.int32

D_MODEL = 1024
RMS_EPS = 1e-6
ROPE_THETA = 500000.0
Q_BLOCK = 128
A_HEADS = 8
A_KV_HEADS = 2
A_HEAD_DIM = 64
A_ROT = A_HEAD_DIM // 4
A_REP = A_HEADS // A_KV_HEADS
IDX_HEADS = 8
IDX_DIM = 32
IDX_ROT = IDX_DIM // 4
IDX_TOPK_MAX = 256
SC_W = 512
SC_KERNEL = 3
GM_W = 512
GM_GROUPS = 4
GM_GROUP_W = GM_W // GM_GROUPS
GM_CHUNK = 128
M_DINNER = 512
M_HEAD_DIM = 64
M_HEADS = M_DINNER // M_HEAD_DIM
M_GROUPS = 2
M_STATE = 64
M_CONV = 4
M_CHUNK = 128
M_BC = M_GROUPS * M_STATE
M_XBC = M_DINNER + 2 * M_BC
N_BRANCH = 4
BRANCH_W = 512
N_EXPERTS = 16
N_EXPERT_GROUPS = 4
EXPERTS_PER_GROUP = N_EXPERTS // N_EXPERT_GROUPS
D_FF_EXPERT = 512
A_Q = A_HEADS * A_HEAD_DIM
A_KV = A_KV_HEADS * A_HEAD_DIM
IDX_Q = IDX_HEADS * IDX_DIM

LANES = 128
SUBLANES = 8
VMEM_LIMIT = 56 * 1024 * 1024

P_GATES = 0
P_GUV = 4096
P_Q = 5120
P_SCB = 5632
P_SCC = 6144
P_SCX = 6656
P_MZ = 7168
P_XBC = 7680
P_K = 8448
P_V = 8576
P_COLS = 8704
S_QI = 0
S_KW = 256
S_DT = 384
S_COLS = 512

INT_MIN = -2147483648
NEG_INF = float("-inf")
NEG_INF_KEY = -2139095041


def _dot(a, b):
    return jnp.dot(a, b, preferred_element_type=F32)


def _dot_nt(a, b):
    return lax.dot_general(a, b, (((1,), (1,)), ((), ())), preferred_element_type=F32)


def _dot_f32(a, b):
    return jnp.dot(a, b, preferred_element_type=F32, precision=lax.Precision.HIGHEST)


def _split(x):
    hi = x.astype(BF16)
    lo = (x - hi.astype(F32)).astype(BF16)
    return hi, lo


def _dot3(a_hi, a_lo, b_hi, b_lo):
    return _dot(a_hi, b_hi) + _dot(a_lo, b_hi) + _dot(a_hi, b_lo)


def _sigmoid(x):
    return 1.0 / (1.0 + jnp.exp(-x))


def _silu(x):
    return x * _sigmoid(x)


def _softplus(x):
    return jnp.maximum(x, 0.0) + jnp.log1p(jnp.exp(-jnp.abs(x)))


def _gelu(x):
    return 0.5 * x * (1.0 + lax.erf(x * np.float32(np.sqrt(0.5))))


def _rms(x, g_row):
    ms = jnp.mean(x * x, axis=-1, keepdims=True)
    return x * lax.rsqrt(ms + RMS_EPS) * g_row


def _params(*sem):
    return pltpu.CompilerParams(dimension_semantics=sem, vmem_limit_bytes=VMEM_LIMIT)


def _mod_kernel(c_ref, w_ref, b_ref, o_ref):
    o_ref[0] = _dot_f32(_silu(c_ref[...]), w_ref[0]) + b_ref[0]


def _modulation(c, ada_w, ada_b):
    depth, d, n = ada_w.shape
    bsz = c.shape[0]
    tn = 768
    return pl.pallas_call(
        _mod_kernel,
        grid=(depth, n // tn),
        in_specs=[pl.BlockSpec((bsz, d), lambda l, j: (0, 0)),
                  pl.BlockSpec((1, d, tn), lambda l, j: (l, 0, j)),
                  pl.BlockSpec((1, 1, tn), lambda l, j: (l, 0, j))],
        out_specs=pl.BlockSpec((1, bsz, tn), lambda l, j: (l, 0, j)),
        out_shape=jax.ShapeDtypeStruct((depth, bsz, n), F32),
        compiler_params=_params("arbitrary", "arbitrary"),
        name="modulation",
    )(c, ada_w, ada_b.reshape(depth, 1, n))


def _inproj_kernel(x_ref, g_ref, sc_ref, sh_ref, w_ref, wsh_ref, wsl_ref, o_ref, os_ref, hh_ref, hl_ref):
    @pl.when(pl.program_id(1) == 0)
    def _():
        h = _rms(x_ref[...], g_ref[...]) * (1.0 + sc_ref[0]) + sh_ref[0]
        hh, hl = _split(h)
        hh_ref[...] = hh
        hl_ref[...] = hl
        os_ref[...] = _dot3(hh, hl, wsh_ref[...], wsl_ref[...])

    o_ref[...] = _dot(hh_ref[...], w_ref[...]).astype(BF16)


def _inproj(x2, seq, g, scale, shift, w_main, ws_hi, ws_lo):
    ntok, d = x2.shape
    tm = min(1024, seq)
    tn = 512
    bsz = scale.shape[0]
    row = lambda i, j: (i, 0)
    per_b = lambda i, j: ((i * tm) // seq, 0, 0)
    return pl.pallas_call(
        _inproj_kernel,
        grid=(ntok // tm, P_COLS // tn),
        in_specs=[pl.BlockSpec((tm, d), row),
                  pl.BlockSpec((1, d), lambda i, j: (0, 0)),
                  pl.BlockSpec((1, 1, d), per_b),
                  pl.BlockSpec((1, 1, d), per_b),
                  pl.BlockSpec((d, tn), lambda i, j: (0, j)),
                  pl.BlockSpec((d, S_COLS), lambda i, j: (0, 0)),
                  pl.BlockSpec((d, S_COLS), lambda i, j: (0, 0))],
        out_specs=[pl.BlockSpec((tm, tn), lambda i, j: (i, j)),
                   pl.BlockSpec((tm, S_COLS), row)],
        out_shape=[jax.ShapeDtypeStruct((ntok, P_COLS), BF16),
                   jax.ShapeDtypeStruct((ntok, S_COLS), F32)],
        scratch_shapes=[pltpu.VMEM((tm, d), BF16), pltpu.VMEM((tm, d), BF16)],
        compiler_params=_params("arbitrary", "arbitrary"),
        name="inproj",
    )(x2, g.reshape(1, d), scale.reshape(bsz, 1, d), shift.reshape(bsz, 1, d), w_main, ws_hi, ws_lo)


def _rope_tables(pos_f, period, rot, width):
    half = rot // 2
    lane = lax.broadcasted_iota(I32, (1, width), 1) % period
    k = (lane % half).astype(F32)
    inv_freq = jnp.exp(np.float32(-math.log(ROPE_THETA)) * k * 2.0 / rot)
    inv_freq = jnp.where(lane < rot, inv_freq, 0.0)
    ang = pos_f * inv_freq
    cos = jnp.cos(ang)
    sin = jnp.sin(ang)
    sin_lo = jnp.where(lane < half, -sin, 0.0)
    sin_hi = jnp.where((lane >= half) & (lane < rot), sin, 0.0)
    return cos, sin_lo, sin_hi


def _rope(x, tables, half):
    cos, sin_lo, sin_hi = tables
    width = x.shape[-1]
    return (x * cos + pltpu.roll(x, width - half, 1) * sin_lo + pltpu.roll(x, half, 1) * sin_hi)


def _prep_kernel(q_ref, k_ref, v_ref, qi_ref, kw_ref, pos_ref, qg_ref, kg_ref, bdq_ref, bdk_ref,
                 pq_ref, pk_ref, pqi1_ref, pqi2_ref, pki1_ref, pki2_ref,
                 qo_ref, ko_ref, vto_ref, qio_ref, kio_ref):
    pos_f = pos_ref[...].astype(F32)
    att_tab = _rope_tables(pos_f, A_HEAD_DIM, A_ROT, LANES)
    idx_tab = _rope_tables(pos_f, IDX_DIM, IDX_ROT, LANES)
    kw_tab = _rope_tables(pos_f, LANES, IDX_ROT, LANES)

    def head_norm(x, bd_ref, g_ref):
        sq_hi, sq_lo = _split(x * x)
        ms = (_dot(sq_hi, bd_ref[...]) + _dot(sq_lo, bd_ref[...])) * np.float32(1.0 / A_HEAD_DIM)
        return x * lax.rsqrt(ms + RMS_EPS) * g_ref[...]

    def rope_wide(x, tab, half):
        pieces = [_rope(x[:, c * LANES:(c + 1) * LANES], tab, half) for c in range(x.shape[-1] // LANES)]
        return pieces[0] if len(pieces) == 1 else jnp.concatenate(pieces, axis=-1)

    def store_head_rows(o_ref, x):
        for blk in range(x.shape[0] // Q_BLOCK):
            for h in range(x.shape[1] // LANES):
                r0 = (blk * (x.shape[1] // LANES) + h) * Q_BLOCK
                o_ref[r0:r0 + Q_BLOCK, :] = x[blk * Q_BLOCK:(blk + 1) * Q_BLOCK, h * LANES:(h + 1) * LANES]

    q = rope_wide(head_norm(q_ref[...].astype(F32), bdq_ref, qg_ref), att_tab, A_ROT // 2)
    q = (q * np.float32(A_HEAD_DIM ** -0.5)).astype(BF16)
    store_head_rows(qo_ref, _dot(q, pq_ref[...]).astype(BF16))
    k = rope_wide(head_norm(k_ref[...].astype(F32), bdk_ref, kg_ref), att_tab, A_ROT // 2).astype(BF16)
    ko_ref[...] = _dot(k, pk_ref[...]).astype(BF16)
    v_pad = _dot(v_ref[...], pk_ref[...])
    ones_lane = (lax.broadcasted_iota(I32, (1, A_KV_HEADS * LANES), 1) % LANES) == A_HEAD_DIM
    vto_ref[...] = jnp.transpose(jnp.where(ones_lane, 1.0, v_pad)).astype(BF16)

    qi_hi, qi_lo = _split(rope_wide(qi_ref[...], idx_tab, IDX_ROT // 2))
    store_head_rows(qio_ref, (_dot(qi_hi, pqi1_ref[...]) + _dot(qi_lo, pqi2_ref[...])).astype(BF16))
    ki_hi, ki_lo = _split(_rope(kw_ref[...], kw_tab, IDX_ROT // 2))
    kio_ref[...] = (_dot(ki_hi, pki1_ref[...]) + _dot(ki_lo, pki2_ref[...])).astype(BF16)


@functools.lru_cache(maxsize=None)
def _prep_constants():
    def block_diag(width):
        i = np.arange(width)
        return (i[:, None] // A_HEAD_DIM == i[None, :] // A_HEAD_DIM).astype(np.float32)

    def pad_heads(n_heads):
        p = np.zeros((n_heads * A_HEAD_DIM, n_heads * LANES), np.float32)
        for h in range(n_heads):
            for d in range(A_HEAD_DIM):
                p[h * A_HEAD_DIM + d, h * LANES + d] = 1.0
        return p

    pqi1 = np.zeros((IDX_Q, IDX_HEADS * LANES), np.float32)
    pqi2 = np.zeros((IDX_Q, IDX_HEADS * LANES), np.float32)
    for h in range(IDX_HEADS):
        for d in range(IDX_DIM):
            pqi1[h * IDX_DIM + d, h * LANES + d] = 1.0
            pqi1[h * IDX_DIM + d, h * LANES + IDX_DIM + d] = 1.0
            pqi2[h * IDX_DIM + d, h * LANES + 2 * IDX_DIM + d] = 1.0
    pki1 = np.zeros((LANES, LANES), np.float32)
    pki2 = np.zeros((LANES, LANES), np.float32)
    for d in range(IDX_DIM):
        pki1[d, d] = 1.0
        pki1[d, 2 * IDX_DIM + d] = 1.0
        pki2[d, IDX_DIM + d] = 1.0
    mats = (block_diag(A_Q), block_diag(A_KV), pad_heads(A_HEADS), pad_heads(A_KV_HEADS), pqi1, pqi2, pki1, pki2)
    return tuple(np.asarray(m) for m in mats)


def _prep(proj, small, pos2, seq, q_g, k_g):
    ntok = proj.shape[0]
    tm = min(512, seq)
    consts = [jnp.asarray(m, BF16) for m in _prep_constants()]
    row = lambda w, c: pl.BlockSpec((tm, w), lambda i: (i, c))
    full = lambda a: pl.BlockSpec(a.shape, lambda i: (0, 0))
    qg = jnp.tile(q_g, A_HEADS).reshape(1, A_Q)
    kg = jnp.tile(k_g, A_KV_HEADS).reshape(1, A_KV)
    return pl.pallas_call(
        _prep_kernel,
        grid=(ntok // tm,),
        in_specs=[row(A_Q, P_Q // A_Q), row(A_KV, P_K // A_KV), row(A_KV, P_V // A_KV), row(IDX_Q, S_QI // IDX_Q),
                  row(LANES, S_KW // LANES), row(1, 0), full(qg), full(kg)] + [full(m) for m in consts],
        out_specs=[pl.BlockSpec((tm * A_HEADS, LANES), lambda i: (i, 0)), row(A_KV_HEADS * LANES, 0),
                   pl.BlockSpec((A_KV_HEADS * LANES, tm), lambda i: (0, i)),
                   pl.BlockSpec((tm * IDX_HEADS, LANES), lambda i: (i, 0)), row(LANES, 0)],
        out_shape=[jax.ShapeDtypeStruct((ntok * A_HEADS, LANES), BF16),
                   jax.ShapeDtypeStruct((ntok, A_KV_HEADS * LANES), BF16),
                   jax.ShapeDtypeStruct((A_KV_HEADS * LANES, ntok), BF16),
                   jax.ShapeDtypeStruct((ntok * IDX_HEADS, LANES), BF16),
                   jax.ShapeDtypeStruct((ntok, LANES), BF16)],
        compiler_params=_params("arbitrary"),
        name="attn_prep",
    )(proj, proj, proj, small, small, pos2, qg, kg, *consts)


def _sortable(x):
    bits = pltpu.bitcast(jnp.where(x == 0.0, 0.0, x), I32)
    return jnp.where(bits < 0, bits ^ jnp.int32(0x7FFFFFFF), bits)


KEY_STEP = 512
KEY_SUBS = KEY_STEP // Q_BLOCK


def _key_row0(step, sub):
    return pl.multiple_of(step * KEY_STEP + sub * Q_BLOCK, Q_BLOCK)


def _count_rows(pred_fn, nstep):
    def body(c, acc):
        for sub in range(KEY_SUBS):
            ones = jnp.where(pred_fn(_key_row0(c, sub)), jnp.int32(1), jnp.int32(0))
            acc = acc + jnp.sum(ones.reshape(Q_BLOCK // SUBLANES, SUBLANES, LANES), axis=0)
        return acc
    acc = lax.fori_loop(0, nstep, body, jnp.zeros((SUBLANES, LANES), I32))
    return jnp.sum(acc, axis=0, keepdims=True)


I16 = jnp.int16
PACK16 = 2 * SUBLANES
HALF_BIAS = 32768


def _count_rows16(pred_fn, nstep):
    def body(c, acc):
        for sub in range(KEY_SUBS):
            ones = jnp.where(pred_fn(_key_row0(c, sub)), jnp.int16(1), jnp.int16(0))
            parts = [ones[t * PACK16:(t + 1) * PACK16, :] for t in range(Q_BLOCK // PACK16)]
            while len(parts) > 1:
                parts = [a + b for a, b in zip(parts[0::2], parts[1::2])]
            acc = acc + parts[0]
        return acc
    acc = lax.fori_loop(0, nstep, body, jnp.zeros((PACK16, LANES), I16))
    return jnp.sum(acc.astype(I32), axis=0, keepdims=True)


def _kth_largest16(ref, rows, kth, nstep):
    def bit_body(i, cand):
        trial = cand | lax.shift_left(jnp.int32(1), 15 - i)
        t16 = (trial - HALF_BIAS).astype(I16)
        cnt = _count_rows16(lambda r0: ref[rows(r0), :] >= t16, nstep)
        return jnp.where(cnt >= kth, trial, cand)
    return lax.fori_loop(0, 16, bit_body, jnp.zeros((1, LANES), I32))


PAIR = 2 * LANES
N_PAIRS = A_HEADS // 2


def _dsa_kernel(qi_ref, kw_ref, q_ref, ki_ref, k_ref, vt_ref, o_ref, key_ref, hi_ref, lo_ref, s_ref, lim_ref, oacc_ref,
                *, topk, idx_bits):
    qb = pl.program_id(1)
    nstep = lax.div(qb, jnp.int32(KEY_SUBS)) + 1
    row_i = lax.broadcasted_iota(I32, (Q_BLOCK, LANES), 0)
    q_pos = qb * Q_BLOCK + lax.broadcasted_iota(I32, (Q_BLOCK, LANES), 1)

    w_rows = jnp.transpose(kw_ref[...])[IDX_DIM:IDX_DIM + IDX_HEADS, :]
    w_rows = w_rows * np.float32(IDX_HEADS ** -0.5) * np.float32(IDX_DIM ** -0.5)

    def rows(r0):
        return pl.ds(r0, Q_BLOCK)

    def score_body(c, carry):
        for sub in range(KEY_SUBS):
            r0 = _key_row0(c, sub)
            ki = ki_ref[rows(r0), :]
            s = jnp.zeros((Q_BLOCK, LANES), F32)
            for pair in range(IDX_HEADS // 2):
                d = _dot_nt(ki, qi_ref[pair * PAIR:(pair + 1) * PAIR, :])
                for j in range(2):
                    h = 2 * pair + j
                    s = s + jnp.maximum(d[:, j * LANES:(j + 1) * LANES], 0.0) * w_rows[h:h + 1, :]
            key = _sortable(jnp.where((r0 + row_i) <= q_pos, s, NEG_INF))
            key_ref[rows(r0), :] = key
            hi_ref[rows(r0), :] = lax.shift_right_arithmetic(key, 16).astype(I16)
            lo_ref[rows(r0), :] = ((key & 0xFFFF) - HALF_BIAS).astype(I16)
        return carry
    lax.fori_loop(0, nstep, score_body, 0)

    hi_sel = _kth_largest16(hi_ref, rows, topk, nstep) - HALF_BIAS
    hi_sel16 = hi_sel.astype(I16)
    kth_lo = topk - _count_rows16(lambda r0: hi_ref[rows(r0), :] > hi_sel16, nstep)

    def narrow_body(c, carry):
        for sub in range(KEY_SUBS):
            r0 = _key_row0(c, sub)
            lo_ref[rows(r0), :] = jnp.where(hi_ref[rows(r0), :] == hi_sel16, lo_ref[rows(r0), :],
                                            jnp.int16(-HALF_BIAS))
        return carry
    lax.fori_loop(0, nstep, narrow_body, 0)
    lo_sel = _kth_largest16(lo_ref, rows, kth_lo, nstep)
    thr = lax.shift_left(hi_sel, 16) | lo_sel

    n_gt = _count_rows(lambda r0: key_ref[rows(r0), :] > thr, nstep)
    n_eq = _count_rows(lambda r0: key_ref[rows(r0), :] == thr, nstep)
    need = topk - n_gt
    excess = (n_eq > need) & (thr > jnp.int32(NEG_INF_KEY))
    lim_ref[0:1, :] = jnp.full((1, LANES), 2 ** 30, I32)

    @pl.when(jnp.max(jnp.where(excess, 1, 0)) > 0)
    def _():
        def idx_body(i, lim):
            trial = lim | lax.shift_left(jnp.int32(1), idx_bits - 1 - i)
            cnt = _count_rows(lambda r0: (key_ref[rows(r0), :] == thr) & ((r0 + row_i) < trial), nstep)
            return jnp.where(cnt < need, trial, lim)
        lim = lax.fori_loop(0, idx_bits, idx_body, jnp.zeros((1, LANES), I32))
        lim_ref[0:1, :] = jnp.where(excess, lim, 2 ** 30)
    tie_lim = lim_ref[0:1, :]

    def max_body(c, m_all):
        m_all = list(m_all)
        for sub in range(KEY_SUBS):
            r0 = _key_row0(c, sub)
            keys = key_ref[rows(r0), :]
            kpos = r0 + row_i
            sel = ((keys > thr) | ((keys == thr) & (kpos <= tie_lim))) & (kpos <= q_pos)
            bias = jnp.where(sel, 0.0, NEG_INF)
            bias2 = jnp.concatenate([bias, bias], axis=-1)
            for pair in range(N_PAIRS):
                g = (2 * pair) // A_REP
                s = _dot_nt(k_ref[rows(r0), g * LANES:(g + 1) * LANES], q_ref[pair * PAIR:(pair + 1) * PAIR, :]) + bias2
                s_ref[rows(r0), pair * PAIR:(pair + 1) * PAIR] = s
                m_all[pair] = jnp.maximum(m_all[pair], jnp.max(s, axis=0, keepdims=True))
        return tuple(m_all)
    m_fin = lax.fori_loop(0, nstep, max_body, tuple(jnp.full((1, PAIR), -1e30, F32) for _ in range(N_PAIRS)))

    oacc_ref[...] = jnp.zeros(oacc_ref.shape, F32)

    def pv_body(c, carry):
        cols = pl.ds(pl.multiple_of(c * KEY_STEP, KEY_STEP), KEY_STEP)
        for pair in range(N_PAIRS):
            g = (2 * pair) // A_REP
            p = jnp.exp(s_ref[cols, pair * PAIR:(pair + 1) * PAIR] - m_fin[pair]).astype(BF16)
            oacc_ref[pair] += _dot(vt_ref[g * LANES:(g + 1) * LANES, cols], p)
        return carry
    lax.fori_loop(0, nstep, pv_body, 0)

    outs = []
    for pair in range(N_PAIRS):
        o_p = oacc_ref[pair]
        for j in range(2):
            o_h = o_p[:, j * LANES:(j + 1) * LANES]
            outs.append(o_h[0:A_HEAD_DIM, :] / o_h[A_HEAD_DIM:A_HEAD_DIM + 1, :])
    o_ref[...] = jnp.transpose(jnp.concatenate(outs, axis=0)).astype(BF16)


def _dsa(qi_cat, small, q_pad, ki_cat, k_pad, v_t, bsz, seq):
    assert seq % KEY_STEP == 0
    nb = seq // Q_BLOCK
    topk = min(IDX_TOPK_MAX, seq // 4)
    idx_bits = max(1, int(math.ceil(math.log2(seq))))
    qrow = lambda w, c: pl.BlockSpec((Q_BLOCK, w), lambda b, j: (b * nb + j, c))
    brow = lambda w: pl.BlockSpec((seq, w), lambda b, j: (b, 0))
    return pl.pallas_call(
        functools.partial(_dsa_kernel, topk=topk, idx_bits=idx_bits),
        grid=(bsz, nb),
        in_specs=[pl.BlockSpec((IDX_HEADS * Q_BLOCK, LANES), lambda b, j: (b * nb + j, 0)),
                  qrow(LANES, S_KW // LANES),
                  pl.BlockSpec((A_HEADS * Q_BLOCK, LANES), lambda b, j: (b * nb + j, 0)),
                  brow(LANES), brow(A_KV_HEADS * LANES),
                  pl.BlockSpec((A_KV_HEADS * LANES, seq), lambda b, j: (0, b))],
        out_specs=qrow(A_Q, 0),
        out_shape=jax.ShapeDtypeStruct((bsz * seq, A_Q), BF16),
        scratch_shapes=[pltpu.VMEM((seq, LANES), I32), pltpu.VMEM((seq, LANES), I16), pltpu.VMEM((seq, LANES), I16),
                        pltpu.VMEM((seq, A_HEADS * LANES), F32),
                        pltpu.VMEM((SUBLANES, LANES), I32), pltpu.VMEM((N_PAIRS, LANES, PAIR), F32)],
        compiler_params=_params("arbitrary", "arbitrary"),
        name="sparse_attention",
    )(qi_cat, small, q_pad, ki_cat, k_pad, v_t)


HALO = SUBLANES


def _causal_conv(ext_ref, u, w_ref, taps, rows):
    ext_ref[HALO:HALO + rows, :] = u
    y = u * w_ref[taps - 1:taps, :]
    for j in range(taps - 1):
        shift = taps - 1 - j
        y = y + ext_ref[HALO - shift:HALO - shift + rows, :] * w_ref[j:j + 1, :]
    ext_ref[0:HALO, :] = ext_ref[rows:rows + HALO, :]
    return y


def _mix_kernel(x_ref, g0_ref, g1_ref, g2_ref, g3_ref, guv_ref, scb_ref, scc_ref, scx_ref, mz_ref, xbc_ref,
                dt_ref, oa_ref, gate_ref, scw_ref, gvg_ref, ws_ref, wsb_ref, mcw_ref, mcb_ref, dtb_ref, alog_ref,
                md_ref, mng_ref, wb_ref, wo_ref, o_ref, sc_ext, xbc_ext, state_ref, y_ref, *, rows):
    @pl.when(pl.program_id(1) == 0)
    def _():
        sc_ext[0:HALO, :] = jnp.zeros((HALO, SC_W), F32)
        xbc_ext[0:HALO, :] = jnp.zeros((HALO, M_XBC), F32)
        state_ref[...] = jnp.zeros(state_ref.shape, F32)

    nchunk = rows // M_CHUNK
    row_i = lax.broadcasted_iota(I32, (M_CHUNK, M_CHUNK), 0)
    col_i = lax.broadcasted_iota(I32, (M_CHUNK, M_CHUNK), 1)
    tri = row_i >= col_i

    o_b = scb_ref[...].astype(F32) * _causal_conv(
        sc_ext, scc_ref[...].astype(F32) * scx_ref[...].astype(F32), scw_ref, SC_KERNEL, rows)

    guv = _gelu(guv_ref[...].astype(F32))
    gu = guv[:, :GM_W]
    gv = _rms(guv[:, GM_W:], gvg_ref[...]).astype(BF16)
    mixed_rows = []
    for ci in range(nchunk):
        r0 = ci * GM_CHUNK
        cols = []
        for gi in range(GM_GROUPS):
            vb = gv[r0:r0 + GM_CHUNK, gi * GM_GROUP_W:(gi + 1) * GM_GROUP_W]
            cols.append(_dot(ws_ref[gi], vb) + wsb_ref[:, gi:gi + 1])
        mixed_rows.append(jnp.concatenate(cols, axis=-1))
    o_c = gu * jnp.concatenate(mixed_rows, axis=0)

    xbc = _silu(_causal_conv(xbc_ext, xbc_ref[...].astype(F32), mcw_ref, M_CONV, rows) + mcb_ref[...])
    head_lane = lax.broadcasted_iota(I32, (1, LANES), 1) < M_HEADS
    dt = jnp.where(head_lane, _softplus(dt_ref[...] + dtb_ref[...]), 0.0)
    a_row = jnp.where(head_lane, -jnp.exp(alog_ref[...]), 0.0)
    tri_f = jnp.where(tri, 1.0, 0.0).astype(F32)
    for ci in range(nchunk):
        r0 = ci * M_CHUNK
        dtc = dt[r0:r0 + M_CHUNK, :]
        acs = _dot_f32(tri_f, dtc * a_row)
        acs_t = jnp.transpose(acs)
        b_t = jnp.transpose(xbc[r0:r0 + M_CHUNK, M_DINNER:M_DINNER + M_BC]).astype(BF16)
        cm = xbc[r0:r0 + M_CHUNK, M_DINNER + M_BC:M_XBC].astype(BF16)
        for g in range(M_GROUPS):
            bg_t = b_t[g * M_STATE:(g + 1) * M_STATE, :]
            cg = cm[:, g * M_STATE:(g + 1) * M_STATE]
            cb = _dot(cg, bg_t)
            for h in range(g * (M_HEADS // M_GROUPS), (g + 1) * (M_HEADS // M_GROUPS)):
                col = acs[:, h:h + 1]
                row = acs_t[h:h + 1, :]
                last = acs[M_CHUNK - 1:M_CHUNK, h:h + 1]
                decay = jnp.exp(jnp.where(tri, col - row, NEG_INF))
                xs_h = xbc[r0:r0 + M_CHUNK, h * M_HEAD_DIM:(h + 1) * M_HEAD_DIM]
                xdt = xs_h * dtc[:, h:h + 1]
                y = _dot((cb * decay).astype(BF16), xdt.astype(BF16))
                prev_t = state_ref[h]
                y = y + _dot(cg, prev_t.astype(BF16)) * jnp.exp(col)
                st_t = _dot(bg_t, (xdt * jnp.exp(last - col)).astype(BF16))
                state_ref[h] = prev_t * jnp.exp(last) + st_t
                y_ref[r0:r0 + M_CHUNK, h * M_HEAD_DIM:(h + 1) * M_HEAD_DIM] = y + md_ref[:, h:h + 1] * xs_h
    o_d = _rms(y_ref[...] * _silu(mz_ref[...].astype(F32)), mng_ref[...])

    branches = (oa_ref[...], o_b.astype(BF16), o_c.astype(BF16), o_d.astype(BF16))
    gate_refs = (g0_ref, g1_ref, g2_ref, g3_ref)
    merged = None
    for i in range(N_BRANCH):
        term = _sigmoid(gate_refs[i][...].astype(F32)) * _dot(branches[i], wb_ref[i])
        merged = term if merged is None else merged + term
    o_ref[...] = x_ref[...] + gate_ref[0] * _dot(merged.astype(BF16), wo_ref[...])


def _mix(x2, proj, small, o_a, gate1, lw, bsz, seq):
    d = x2.shape[1]
    rows = min(256, seq)
    ns = seq // rows
    tok = lambda w, c: pl.BlockSpec((rows, w), lambda b, s: (b * ns + s, c))
    full = lambda a: pl.BlockSpec(a.shape, lambda b, s: (0,) * a.ndim)
    weights = [lw["sc_conv_w"], lw["g_v_norm_g"], lw["ws"], lw["ws_b"], lw["m_conv_w"], lw["m_conv_b"],
               lw["m_dt_bias"], lw["m_a_log"], lw["m_d"], lw["m_norm_g"], lw["w_branch"], lw["w_out"]]
    return pl.pallas_call(
        functools.partial(_mix_kernel, rows=rows),
        grid=(bsz, ns),
        in_specs=[tok(d, 0)] + [tok(D_MODEL, i) for i in range(N_BRANCH)]
        + [tok(2 * GM_W, P_GUV // (2 * GM_W)), tok(SC_W, P_SCB // SC_W), tok(SC_W, P_SCC // SC_W),
           tok(SC_W, P_SCX // SC_W), tok(M_DINNER, P_MZ // M_DINNER), tok(M_XBC, P_XBC // M_XBC),
           tok(LANES, S_DT // LANES), tok(A_Q, 0),
           pl.BlockSpec((1, 1, d), lambda b, s: (b, 0, 0))]
        + [full(w) for w in weights],
        out_specs=tok(d, 0),
        out_shape=jax.ShapeDtypeStruct(x2.shape, F32),
        scratch_shapes=[pltpu.VMEM((HALO + rows, SC_W), F32), pltpu.VMEM((HALO + rows, M_XBC), F32),
                        pltpu.VMEM((M_HEADS, M_STATE, M_HEAD_DIM), F32), pltpu.VMEM((rows, M_DINNER), F32)],
        compiler_params=_params("arbitrary", "arbitrary"),
        name="mixers_merge",
    )(x2, proj, proj, proj, proj, proj, proj, proj, proj, proj, proj, small, o_a,
      gate1.reshape(bsz, 1, d), *weights)


def _swap_lanes(x, lane, dist):
    up = pltpu.roll(x, LANES - dist, 1)
    down = pltpu.roll(x, dist, 1)
    return jnp.where((lane % (2 * dist)) < dist, up, down)


def _route(scores, bias_row):
    lane = lax.broadcasted_iota(I32, scores.shape, 1)
    valid = lane < N_EXPERTS
    sel = jnp.where(valid, scores + bias_row, NEG_INF)
    p1 = _swap_lanes(sel, lane, 1)
    hi1, lo1 = jnp.maximum(sel, p1), jnp.minimum(sel, p1)
    hi2, lo2 = _swap_lanes(hi1, lane, 2), _swap_lanes(lo1, lane, 2)
    grp = jnp.maximum(hi1, hi2) + jnp.maximum(jnp.minimum(hi1, hi2), jnp.maximum(lo1, lo2))
    grp = jnp.where(valid, grp, NEG_INF)
    best_val = jnp.max(grp, axis=-1, keepdims=True)
    big = jnp.int32(LANES)
    best = jnp.min(jnp.where(grp == best_val, lane // EXPERTS_PER_GROUP, big), axis=-1, keepdims=True)
    masked = jnp.where((lane // EXPERTS_PER_GROUP == best) & valid, sel, NEG_INF)
    v1 = jnp.max(masked, axis=-1, keepdims=True)
    i1 = jnp.min(jnp.where(masked == v1, lane, big), axis=-1, keepdims=True)
    masked2 = jnp.where(lane == i1, NEG_INF, masked)
    v2 = jnp.max(masked2, axis=-1, keepdims=True)
    i2 = jnp.min(jnp.where(masked2 == v2, lane, big), axis=-1, keepdims=True)
    w1 = jnp.sum(jnp.where(lane == i1, scores, 0.0), axis=-1, keepdims=True)
    w2 = jnp.sum(jnp.where(lane == i2, scores, 0.0), axis=-1, keepdims=True)
    tot = w1 + w2
    return jnp.where(lane == i1, w1 / tot, 0.0) + jnp.where(lane == i2, w2 / tot, 0.0)


def _moe_kernel(x_ref, g_ref, sc_ref, sh_ref, gate_ref, rwh_ref, rwl_ref, rb_ref, wgu_ref, wd_ref, o_ref,
                h_ref, comb_ref, acc_ref):
    e = pl.program_id(1)

    @pl.when(e == 0)
    def _():
        h = _rms(x_ref[...], g_ref[...]) * (1.0 + sc_ref[0]) + sh_ref[0]
        hh, hl = _split(h)
        h_ref[...] = hh
        logits = _dot3(hh, hl, rwh_ref[...], rwl_ref[...])
        comb_ref[...] = _route(_sigmoid(logits), rb_ref[...])
        acc_ref[...] = jnp.zeros(acc_ref.shape, F32)

    lane = lax.broadcasted_iota(I32, comb_ref.shape, 1)
    w_e = jnp.sum(jnp.where(lane == e, comb_ref[...], 0.0), axis=-1, keepdims=True)
    gu = _dot(h_ref[...], wgu_ref[0])
    hid = _silu(gu[:, :D_FF_EXPERT]) * gu[:, D_FF_EXPERT:]
    acc_ref[...] += w_e * _dot(hid.astype(BF16), wd_ref[0])

    @pl.when(e == pl.num_programs(1) - 1)
    def _():
        o_ref[...] = x_ref[...] + gate_ref[0] * acc_ref[...]


def _moe(x2, seq, g, scale, shift, gate, rw_hi, rw_lo, rb, wgu, wd):
    ntok, d = x2.shape
    bsz = scale.shape[0]
    tm = min(1024, seq)
    row = lambda i, e: (i, 0)
    per_b = lambda i, e: ((i * tm) // seq, 0, 0)
    vec = pl.BlockSpec((1, 1, d), per_b)
    const2 = lambda a: pl.BlockSpec(a.shape, lambda i, e: (0, 0))
    return pl.pallas_call(
        _moe_kernel,
        grid=(ntok // tm, N_EXPERTS),
        in_specs=[pl.BlockSpec((tm, d), row), pl.BlockSpec((1, d), lambda i, e: (0, 0)), vec, vec, vec,
                  const2(rw_hi), const2(rw_lo), const2(rb),
                  pl.BlockSpec((1, d, 2 * D_FF_EXPERT), lambda i, e: (e, 0, 0)),
                  pl.BlockSpec((1, D_FF_EXPERT, d), lambda i, e: (e, 0, 0))],
        out_specs=pl.BlockSpec((tm, d), row),
        out_shape=jax.ShapeDtypeStruct(x2.shape, F32),
        scratch_shapes=[pltpu.VMEM((tm, d), BF16), pltpu.VMEM((tm, LANES), F32), pltpu.VMEM((tm, d), F32)],
        compiler_params=_params("arbitrary", "arbitrary"),
        name="moe",
    )(x2, g.reshape(1, d), scale.reshape(bsz, 1, d), shift.reshape(bsz, 1, d), gate.reshape(bsz, 1, d),
      rw_hi, rw_lo, rb, wgu, wd)


def _pad_lanes(v, width=LANES):
    v = v.reshape(1, -1)
    return jnp.pad(v, ((0, 0), (0, width - v.shape[1])))


def _layer_weights(w_in, sc_conv_w, g_v_norm_g, g_spatial_w, g_spatial_b, m_conv_w, m_conv_b, m_dt_bias,
                   m_a_log, m_d, m_norm_g, w_branch, w_out):
    splits = (A_Q, A_KV, A_KV, IDX_Q, IDX_DIM, IDX_HEADS, SC_W, SC_W, SC_W, 2 * GM_W, M_DINNER, M_XBC, M_HEADS,
              N_BRANCH * D_MODEL)
    pts = [int(p) for p in np.cumsum(splits)[:-1]]
    (q, k, v, qi, ki, wi, sc_b, sc_c, sc_x, g_uv, m_z, m_xbc, m_dt, gates) = jnp.split(w_in, pts, axis=-1)
    d = w_in.shape[0]
    w_main = jnp.concatenate([gates, g_uv, q, sc_b, sc_c, sc_x, m_z, m_xbc, k, v], axis=-1).astype(BF16)
    w_small = jnp.concatenate(
        [qi, ki, wi, jnp.zeros((d, LANES - IDX_DIM - IDX_HEADS), F32), m_dt, jnp.zeros((d, LANES - M_HEADS), F32)],
        axis=-1)
    ws_hi = w_small.astype(BF16)
    ws_lo = (w_small - ws_hi.astype(F32)).astype(BF16)
    tri = np.tril(np.ones((GM_CHUNK, GM_CHUNK), dtype=bool))
    return dict(
        w_main=w_main, ws_hi=ws_hi, ws_lo=ws_lo,
        sc_conv_w=sc_conv_w, g_v_norm_g=g_v_norm_g.reshape(1, GM_W),
        ws=jnp.where(tri[None], g_spatial_w, 0.0).astype(BF16), ws_b=g_spatial_b.T,
        m_conv_w=m_conv_w, m_conv_b=m_conv_b.reshape(1, M_XBC),
        m_dt_bias=_pad_lanes(m_dt_bias), m_a_log=_pad_lanes(m_a_log), m_d=_pad_lanes(m_d),
        m_norm_g=m_norm_g.reshape(1, M_DINNER), w_branch=w_branch.astype(BF16), w_out=w_out.astype(BF16))


def kernel(x, c, positions, ada_w, ada_b, norm1_g, w_in, q_norm_g, k_norm_g, sc_conv_w, g_v_norm_g, g_spatial_w,
           g_spatial_b, m_conv_w, m_conv_b, m_dt_bias, m_a_log, m_d, m_norm_g, w_branch, w_out, norm2_g, router_w,
           router_bias, exp_w_gate, exp_w_up, exp_w_down):
    bsz, seq, d = x.shape
    depth = ada_w.shape[0]
    mod = _modulation(c, ada_w, ada_b)
    rw = jnp.pad(router_w, ((0, 0), (0, LANES - N_EXPERTS)))
    rw_hi = rw.astype(BF16)
    rw_lo = (rw - rw_hi.astype(F32)).astype(BF16)
    rb = _pad_lanes(router_bias)
    pos2 = positions.reshape(bsz * seq, 1)
    x2 = x.reshape(bsz * seq, d)
    for l in range(depth):
        shift1, scale1, gate1, shift2, scale2, gate2 = jnp.split(mod[l], 6, axis=-1)
        lw = _layer_weights(w_in[l], sc_conv_w[l], g_v_norm_g[l], g_spatial_w[l], g_spatial_b[l], m_conv_w[l],
                            m_conv_b[l], m_dt_bias[l], m_a_log[l], m_d[l], m_norm_g[l], w_branch[l], w_out[l])
        proj, small = _inproj(x2, seq, norm1_g[l], scale1, shift1, lw["w_main"], lw["ws_hi"], lw["ws_lo"])
        q_pad, k_pad, v_t, qi_cat, ki_cat = _prep(proj, small, pos2, seq, q_norm_g[l], k_norm_g[l])
        o_a = _dsa(qi_cat, small, q_pad, ki_cat, k_pad, v_t, bsz, seq)
        x2 = _mix(x2, proj, small, o_a, gate1, lw, bsz, seq)
        wgu = jnp.concatenate([exp_w_gate[l], exp_w_up[l]], axis=-1).astype(BF16)
        x2 = _moe(x2, seq, norm2_g[l], scale2, shift2, gate2, rw_hi, rw_lo, rb, wgu, exp_w_down[l].astype(BF16))
    return x2.reshape(bsz, seq, d)
```

```python
import functools
import math

import jax
import jax.numpy as jnp
import numpy as np
from jax import lax
from jax.experimental import pallas as pl
from jax.experimental.pallas import tpu as pltpu

F32 = jnp.float32
BF16 = jnp.bfloat16
I32 = jnp.int32

D_MODEL = 1024
RMS_EPS = 1e-6
ROPE_THETA = 500000.0
Q_BLOCK = 128
A_HEADS = 8
A_KV_HEADS = 2
A_HEAD_DIM = 64
A_ROT = A_HEAD_DIM // 4
A_REP = A_HEADS // A_KV_HEADS
IDX_HEADS = 8
IDX_DIM = 32
IDX_ROT = IDX_DIM // 4
IDX_TOPK_MAX = 256
SC_W = 512
SC_KERNEL = 3
GM_W = 512
GM_GROUPS = 4
GM_GROUP_W = GM_W // GM_GROUPS
GM_CHUNK = 128
M_DINNER = 512
M_HEAD_DIM = 64
M_HEADS = M_DINNER // M_HEAD_DIM
M_GROUPS = 2
M_STATE = 64
M_CONV = 4
M_CHUNK = 128
M_BC = M_GROUPS * M_STATE
M_XBC = M_DINNER + 2 * M_BC
N_BRANCH = 4
BRANCH_W = 512
N_EXPERTS = 16
N_EXPERT_GROUPS = 4
EXPERTS_PER_GROUP = N_EXPERTS // N_EXPERT_GROUPS
D_FF_EXPERT = 512
A_Q = A_HEADS * A_HEAD_DIM
A_KV = A_KV_HEADS * A_HEAD_DIM
IDX_Q = IDX_HEADS * IDX_DIM

LANES = 128
SUBLANES = 8
VMEM_LIMIT = 56 * 1024 * 1024

P_GATES = 0
P_GUV = 4096
P_Q = 5120
P_SCB = 5632
P_SCC = 6144
P_SCX = 6656
P_MZ = 7168
P_XBC = 7680
P_K = 8448
P_V = 8576
P_COLS = 8704
S_QI = 0
S_KW = 256
S_DT = 384
S_COLS = 512

INT_MIN = -2147483648
NEG_INF = float("-inf")
NEG_INF_KEY = -2139095041


def _dot(a, b):
    return jnp.dot(a, b, preferred_element_type=F32)


def _dot_nt(a, b):
    return lax.dot_general(a, b, (((1,), (1,)), ((), ())), preferred_element_type=F32)


def _dot_f32(a, b):
    return jnp.dot(a, b, preferred_element_type=F32, precision=lax.Precision.HIGHEST)


def _split(x):
    hi = x.astype(BF16)
    lo = (x - hi.astype(F32)).astype(BF16)
    return hi, lo


def _dot3(a_hi, a_lo, b_hi, b_lo):
    return _dot(a_hi, b_hi) + _dot(a_lo, b_hi) + _dot(a_hi, b_lo)


def _sigmoid(x):
    return 1.0 / (1.0 + jnp.exp(-x))


def _silu(x):
    return x * _sigmoid(x)


def _softplus(x):
    return jnp.maximum(x, 0.0) + jnp.log1p(jnp.exp(-jnp.abs(x)))


def _gelu(x):
    return 0.5 * x * (1.0 + lax.erf(x * np.float32(np.sqrt(0.5))))


def _rms(x, g_row):
    ms = jnp.mean(x * x, axis=-1, keepdims=True)
    return x * lax.rsqrt(ms + RMS_EPS) * g_row


def _params(*sem):
    return pltpu.CompilerParams(dimension_semantics=sem, vmem_limit_bytes=VMEM_LIMIT)


def _mod_kernel(c_ref, w_ref, b_ref, o_ref):
    o_ref[0] = _dot_f32(_silu(c_ref[...]), w_ref[0]) + b_ref[0]


def _modulation(c, ada_w, ada_b):
    depth, d, n = ada_w.shape
    bsz = c.shape[0]
    tn = 768
    return pl.pallas_call(
        _mod_kernel,
        grid=(depth, n // tn),
        in_specs=[pl.BlockSpec((bsz, d), lambda l, j: (0, 0)),
                  pl.BlockSpec((1, d, tn), lambda l, j: (l, 0, j)),
                  pl.BlockSpec((1, 1, tn), lambda l, j: (l, 0, j))],
        out_specs=pl.BlockSpec((1, bsz, tn), lambda l, j: (l, 0, j)),
        out_shape=jax.ShapeDtypeStruct((depth, bsz, n), F32),
        compiler_params=_params("arbitrary", "arbitrary"),
        name="modulation",
    )(c, ada_w, ada_b.reshape(depth, 1, n))


def _inproj_kernel(x_ref, g_ref, sc_ref, sh_ref, w_ref, wsh_ref, wsl_ref, o_ref, os_ref, hh_ref, hl_ref):
    @pl.when(pl.program_id(1) == 0)
    def _():
        h = _rms(x_ref[...], g_ref[...]) * (1.0 + sc_ref[0]) + sh_ref[0]
        hh, hl = _split(h)
        hh_ref[...] = hh
        hl_ref[...] = hl
        os_ref[...] = _dot3(hh, hl, wsh_ref[...], wsl_ref[...])

    o_ref[...] = _dot(hh_ref[...], w_ref[...]).astype(BF16)


def _inproj(x2, seq, g, scale, shift, w_main, ws_hi, ws_lo):
    ntok, d = x2.shape
    tm = min(1024, seq)
    tn = 512
    bsz = scale.shape[0]
    row = lambda i, j: (i, 0)
    per_b = lambda i, j: ((i * tm) // seq, 0, 0)
    return pl.pallas_call(
        _inproj_kernel,
        grid=(ntok // tm, P_COLS // tn),
        in_specs=[pl.BlockSpec((tm, d), row),
                  pl.BlockSpec((1, d), lambda i, j: (0, 0)),
                  pl.BlockSpec((1, 1, d), per_b),
                  pl.BlockSpec((1, 1, d), per_b),
                  pl.BlockSpec((d, tn), lambda i, j: (0, j)),
                  pl.BlockSpec((d, S_COLS), lambda i, j: (0, 0)),
                  pl.BlockSpec((d, S_COLS), lambda i, j: (0, 0))],
        out_specs=[pl.BlockSpec((tm, tn), lambda i, j: (i, j)),
                   pl.BlockSpec((tm, S_COLS), row)],
        out_shape=[jax.ShapeDtypeStruct((ntok, P_COLS), BF16),
                   jax.ShapeDtypeStruct((ntok, S_COLS), F32)],
        scratch_shapes=[pltpu.VMEM((tm, d), BF16), pltpu.VMEM((tm, d), BF16)],
        compiler_params=_params("arbitrary", "arbitrary"),
        name="inproj",
    )(x2, g.reshape(1, d), scale.reshape(bsz, 1, d), shift.reshape(bsz, 1, d), w_main, ws_hi, ws_lo)


def _rope_tables(pos_f, period, rot, width):
    half = rot // 2
    lane = lax.broadcasted_iota(I32, (1, width), 1) % period
    k = (lane % half).astype(F32)
    inv_freq = jnp.exp(np.float32(-math.log(ROPE_THETA)) * k * 2.0 / rot)
    inv_freq = jnp.where(lane < rot, inv_freq, 0.0)
    ang = pos_f * inv_freq
    cos = jnp.cos(ang)
    sin = jnp.sin(ang)
    sin_lo = jnp.where(lane < half, -sin, 0.0)
    sin_hi = jnp.where((lane >= half) & (lane < rot), sin, 0.0)
    return cos, sin_lo, sin_hi


def _rope(x, tables, half):
    cos, sin_lo, sin_hi = tables
    width = x.shape[-1]
    return (x * cos + pltpu.roll(x, width - half, 1) * sin_lo + pltpu.roll(x, half, 1) * sin_hi)


def _prep_kernel(q_ref, k_ref, v_ref, qi_ref, kw_ref, pos_ref, qg_ref, kg_ref, bdq_ref, bdk_ref,
                 pq_ref, pk_ref, pqi1_ref, pqi2_ref, pki1_ref, pki2_ref,
                 qo_ref, ko_ref, vto_ref, qio_ref, kio_ref):
    pos_f = pos_ref[...].astype(F32)
    att_tab = _rope_tables(pos_f, A_HEAD_DIM, A_ROT, LANES)
    idx_tab = _rope_tables(pos_f, IDX_DIM, IDX_ROT, LANES)
    kw_tab = _rope_tables(pos_f, LANES, IDX_ROT, LANES)

    def head_norm(x, bd_ref, g_ref):
        sq_hi, sq_lo = _split(x * x)
        ms = (_dot(sq_hi, bd_ref[...]) + _dot(sq_lo, bd_ref[...])) * np.float32(1.0 / A_HEAD_DIM)
        return x * lax.rsqrt(ms + RMS_EPS) * g_ref[...]

    def rope_wide(x, tab, half):
        pieces = [_rope(x[:, c * LANES:(c + 1) * LANES], tab, half) for c in range(x.shape[-1] // LANES)]
        return pieces[0] if len(pieces) == 1 else jnp.concatenate(pieces, axis=-1)

    def store_head_rows(o_ref, x):
        for blk in range(x.shape[0] // Q_BLOCK):
            for h in range(x.shape[1] // LANES):
                r0 = (blk * (x.shape[1] // LANES) + h) * Q_BLOCK
                o_ref[r0:r0 + Q_BLOCK, :] = x[blk * Q_BLOCK:(blk + 1) * Q_BLOCK, h * LANES:(h + 1) * LANES]

    q = rope_wide(head_norm(q_ref[...].astype(F32), bdq_ref, qg_ref), att_tab, A_ROT // 2)
    q = (q * np.float32(A_HEAD_DIM ** -0.5)).astype(BF16)
    store_head_rows(qo_ref, _dot(q, pq_ref[...]).astype(BF16))
    k = rope_wide(head_norm(k_ref[...].astype(F32), bdk_ref, kg_ref), att_tab, A_ROT // 2).astype(BF16)
    ko_ref[...] = _dot(k, pk_ref[...]).astype(BF16)
    v_pad = _dot(v_ref[...], pk_ref[...])
    ones_lane = (lax.broadcasted_iota(I32, (1, A_KV_HEADS * LANES), 1) % LANES) == A_HEAD_DIM
    vto_ref[...] = jnp.transpose(jnp.where(ones_lane, 1.0, v_pad)).astype(BF16)

    qi_hi, qi_lo = _split(rope_wide(qi_ref[...], idx_tab, IDX_ROT // 2))
    store_head_rows(qio_ref, (_dot(qi_hi, pqi1_ref[...]) + _dot(qi_lo, pqi2_ref[...])).astype(BF16))
    ki_hi, ki_lo = _split(_rope(kw_ref[...], kw_tab, IDX_ROT // 2))
    kio_ref[...] = (_dot(ki_hi, pki1_ref[...]) + _dot(ki_lo, pki2_ref[...])).astype(BF16)


@functools.lru_cache(maxsize=None)
def _prep_constants():
    def block_diag(width):
        i = np.arange(width)
        return (i[:, None] // A_HEAD_DIM == i[None, :] // A_HEAD_DIM).astype(np.float32)

    def pad_heads(n_heads):
        p = np.zeros((n_heads * A_HEAD_DIM, n_heads * LANES), np.float32)
        for h in range(n_heads):
            for d in range(A_HEAD_DIM):
                p[h * A_HEAD_DIM + d, h * LANES + d] = 1.0
        return p

    pqi1 = np.zeros((IDX_Q, IDX_HEADS * LANES), np.float32)
    pqi2 = np.zeros((IDX_Q, IDX_HEADS * LANES), np.float32)
    for h in range(IDX_HEADS):
        for d in range(IDX_DIM):
            pqi1[h * IDX_DIM + d, h * LANES + d] = 1.0
            pqi1[h * IDX_DIM + d, h * LANES + IDX_DIM + d] = 1.0
            pqi2[h * IDX_DIM + d, h * LANES + 2 * IDX_DIM + d] = 1.0
    pki1 = np.zeros((LANES, LANES), np.float32)
    pki2 = np.zeros((LANES, LANES), np.float32)
    for d in range(IDX_DIM):
        pki1[d, d] = 1.0
        pki1[d, 2 * IDX_DIM + d] = 1.0
        pki2[d, IDX_DIM + d] = 1.0
    mats = (block_diag(A_Q), block_diag(A_KV), pad_heads(A_HEADS), pad_heads(A_KV_HEADS), pqi1, pqi2, pki1, pki2)
    return tuple(np.asarray(m) for m in mats)


def _prep(proj, small, pos2, seq, q_g, k_g):
    ntok = proj.shape[0]
    tm = min(512, seq)
    consts = [jnp.asarray(m, BF16) for m in _prep_constants()]
    row = lambda w, c: pl.BlockSpec((tm, w), lambda i: (i, c))
    full = lambda a: pl.BlockSpec(a.shape, lambda i: (0, 0))
    qg = jnp.tile(q_g, A_HEADS).reshape(1, A_Q)
    kg = jnp.tile(k_g, A_KV_HEADS).reshape(1, A_KV)
    return pl.pallas_call(
        _prep_kernel,
        grid=(ntok // tm,),
        in_specs=[row(A_Q, P_Q // A_Q), row(A_KV, P_K // A_KV), row(A_KV, P_V // A_KV), row(IDX_Q, S_QI // IDX_Q),
                  row(LANES, S_KW // LANES), row(1, 0), full(qg), full(kg)] + [full(m) for m in consts],
        out_specs=[pl.BlockSpec((tm * A_HEADS, LANES), lambda i: (i, 0)), row(A_KV_HEADS * LANES, 0),
                   pl.BlockSpec((A_KV_HEADS * LANES, tm), lambda i: (0, i)),
                   pl.BlockSpec((tm * IDX_HEADS, LANES), lambda i: (i, 0)), row(LANES, 0)],
        out_shape=[jax.ShapeDtypeStruct((ntok * A_HEADS, LANES), BF16),
                   jax.ShapeDtypeStruct((ntok, A_KV_HEADS * LANES), BF16),
                   jax.ShapeDtypeStruct((A_KV_HEADS * LANES, ntok), BF16),
                   jax.ShapeDtypeStruct((ntok * IDX_HEADS, LANES), BF16),
                   jax.ShapeDtypeStruct((ntok, LANES), BF16)],
        compiler_params=_params("arbitrary"),
        name="attn_prep",
    )(proj, proj, proj, small, small, pos2, qg, kg, *consts)


def _sortable(x):
    bits = pltpu.bitcast(jnp.where(x == 0.0, 0.0, x), I32)
    return jnp.where(bits < 0, bits ^ jnp.int32(0x7FFFFFFF), bits)


KEY_STEP = 512
KEY_SUBS = KEY_STEP // Q_BLOCK


def _key_row0(step, sub):
    return pl.multiple_of(step * KEY_STEP + sub * Q_BLOCK, Q_BLOCK)


def _count_rows(pred_fn, nstep):
    def body(c, acc):
        for sub in range(KEY_SUBS):
            ones = jnp.where(pred_fn(_key_row0(c, sub)), jnp.int32(1), jnp.int32(0))
            acc = acc + jnp.sum(ones.reshape(Q_BLOCK // SUBLANES, SUBLANES, LANES), axis=0)
        return acc
    acc = lax.fori_loop(0, nstep, body, jnp.zeros((SUBLANES, LANES), I32))
    return jnp.sum(acc, axis=0, keepdims=True)


PAIR = 2 * LANES
N_PAIRS = A_HEADS // 2


def _dsa_kernel(qi_ref, kw_ref, q_ref, ki_ref, k_ref, vt_ref, o_ref, key_ref, s_ref, lim_ref, oacc_ref,
                *, topk, idx_bits):
    qb = pl.program_id(1)
    nstep = lax.div(qb, jnp.int32(KEY_SUBS)) + 1
    row_i = lax.broadcasted_iota(I32, (Q_BLOCK, LANES), 0)
    q_pos = qb * Q_BLOCK + lax.broadcasted_iota(I32, (Q_BLOCK, LANES), 1)

    w_rows = jnp.transpose(kw_ref[...])[IDX_DIM:IDX_DIM + IDX_HEADS, :]
    w_rows = w_rows * np.float32(IDX_HEADS ** -0.5) * np.float32(IDX_DIM ** -0.5)

    def rows(r0):
        return pl.ds(r0, Q_BLOCK)

    def score_body(c, carry):
        for sub in range(KEY_SUBS):
            r0 = _key_row0(c, sub)
            ki = ki_ref[rows(r0), :]
            s = jnp.zeros((Q_BLOCK, LANES), F32)
            for pair in range(IDX_HEADS // 2):
                d = _dot_nt(ki, qi_ref[pair * PAIR:(pair + 1) * PAIR, :])
                for j in range(2):
                    h = 2 * pair + j
                    s = s + jnp.maximum(d[:, j * LANES:(j + 1) * LANES], 0.0) * w_rows[h:h + 1, :]
            key_ref[rows(r0), :] = _sortable(jnp.where((r0 + row_i) <= q_pos, s, NEG_INF))
        return carry
    lax.fori_loop(0, nstep, score_body, 0)

    def bit_body(i, cand):
        trial = cand | lax.shift_left(jnp.int32(1), 31 - i)
        trial_s = trial ^ jnp.int32(INT_MIN)
        cnt = _count_rows(lambda r0: key_ref[rows(r0), :] >= trial_s, nstep)
        return jnp.where(cnt >= topk, trial, cand)
    cand = lax.fori_loop(0, 32, bit_body, jnp.zeros((1, LANES), I32))
    thr = cand ^ jnp.int32(INT_MIN)

    n_gt = _count_rows(lambda r0: key_ref[rows(r0), :] > thr, nstep)
    n_eq = _count_rows(lambda r0: key_ref[rows(r0), :] == thr, nstep)
    need = topk - n_gt
    excess = (n_eq > need) & (thr > jnp.int32(NEG_INF_KEY))
    lim_ref[0:1, :] = jnp.full((1, LANES), 2 ** 30, I32)

    @pl.when(jnp.max(jnp.where(excess, 1, 0)) > 0)
    def _():
        def idx_body(i, lim):
            trial = lim | lax.shift_left(jnp.int32(1), idx_bits - 1 - i)
            cnt = _count_rows(lambda r0: (key_ref[rows(r0), :] == thr) & ((r0 + row_i) < trial), nstep)
            return jnp.where(cnt < need, trial, lim)
        lim = lax.fori_loop(0, idx_bits, idx_body, jnp.zeros((1, LANES), I32))
        lim_ref[0:1, :] = jnp.where(excess, lim, 2 ** 30)
    tie_lim = lim_ref[0:1, :]

    def max_body(c, m_all):
        m_all = list(m_all)
        for sub in range(KEY_SUBS):
            r0 = _key_row0(c, sub)
            keys = key_ref[rows(r0), :]
            kpos = r0 + row_i
            sel = ((keys > thr) | ((keys == thr) & (kpos <= tie_lim))) & (kpos <= q_pos)
            bias = jnp.where(sel, 0.0, NEG_INF)
            bias2 = jnp.concatenate([bias, bias], axis=-1)
            for pair in range(N_PAIRS):
                g = (2 * pair) // A_REP
                s = _dot_nt(k_ref[rows(r0), g * LANES:(g + 1) * LANES], q_ref[pair * PAIR:(pair + 1) * PAIR, :]) + bias2
                s_ref[rows(r0), pair * PAIR:(pair + 1) * PAIR] = s
                m_all[pair] = jnp.maximum(m_all[pair], jnp.max(s, axis=0, keepdims=True))
        return tuple(m_all)
    m_fin = lax.fori_loop(0, nstep, max_body, tuple(jnp.full((1, PAIR), -1e30, F32) for _ in range(N_PAIRS)))

    oacc_ref[...] = jnp.zeros(oacc_ref.shape, F32)

    def pv_body(c, carry):
        cols = pl.ds(pl.multiple_of(c * KEY_STEP, KEY_STEP), KEY_STEP)
        for pair in range(N_PAIRS):
            g = (2 * pair) // A_REP
            p = jnp.exp(s_ref[cols, pair * PAIR:(pair + 1) * PAIR] - m_fin[pair]).astype(BF16)
            oacc_ref[pair] += _dot(vt_ref[g * LANES:(g + 1) * LANES, cols], p)
        return carry
    lax.fori_loop(0, nstep, pv_body, 0)

    outs = []
    for pair in range(N_PAIRS):
        o_p = oacc_ref[pair]
        for j in range(2):
            o_h = o_p[:, j * LANES:(j + 1) * LANES]
            outs.append(o_h[0:A_HEAD_DIM, :] / o_h[A_HEAD_DIM:A_HEAD_DIM + 1, :])
    o_ref[...] = jnp.transpose(jnp.concatenate(outs, axis=0)).astype(BF16)


def _dsa(qi_cat, small, q_pad, ki_cat, k_pad, v_t, bsz, seq):
    assert seq % KEY_STEP == 0
    nb = seq // Q_BLOCK
    topk = min(IDX_TOPK_MAX, seq // 4)
    idx_bits = max(1, int(math.ceil(math.log2(seq))))
    qrow = lambda w, c: pl.BlockSpec((Q_BLOCK, w), lambda b, j: (b * nb + j, c))
    brow = lambda w: pl.BlockSpec((seq, w), lambda b, j: (b, 0))
    return pl.pallas_call(
        functools.partial(_dsa_kernel, topk=topk, idx_bits=idx_bits),
        grid=(bsz, nb),
        in_specs=[pl.BlockSpec((IDX_HEADS * Q_BLOCK, LANES), lambda b, j: (b * nb + j, 0)),
                  qrow(LANES, S_KW // LANES),
                  pl.BlockSpec((A_HEADS * Q_BLOCK, LANES), lambda b, j: (b * nb + j, 0)),
                  brow(LANES), brow(A_KV_HEADS * LANES),
                  pl.BlockSpec((A_KV_HEADS * LANES, seq), lambda b, j: (0, b))],
        out_specs=qrow(A_Q, 0),
        out_shape=jax.ShapeDtypeStruct((bsz * seq, A_Q), BF16),
        scratch_shapes=[pltpu.VMEM((seq, LANES), I32), pltpu.VMEM((seq, A_HEADS * LANES), F32),
                        pltpu.VMEM((SUBLANES, LANES), I32), pltpu.VMEM((N_PAIRS, LANES, PAIR), F32)],
        compiler_params=_params("arbitrary", "arbitrary"),
        name="sparse_attention",
    )(qi_cat, small, q_pad, ki_cat, k_pad, v_t)


HALO = SUBLANES


def _causal_conv(ext_ref, u, w_ref, taps, rows):
    ext_ref[HALO:HALO + rows, :] = u
    y = u * w_ref[taps - 1:taps, :]
    for j in range(taps - 1):
        shift = taps - 1 - j
        y = y + ext_ref[HALO - shift:HALO - shift + rows, :] * w_ref[j:j + 1, :]
    ext_ref[0:HALO, :] = ext_ref[rows:rows + HALO, :]
    return y


def _mix_kernel(x_ref, g0_ref, g1_ref, g2_ref, g3_ref, guv_ref, scb_ref, scc_ref, scx_ref, mz_ref, xbc_ref,
                dt_ref, oa_ref, gate_ref, scw_ref, gvg_ref, ws_ref, wsb_ref, mcw_ref, mcb_ref, dtb_ref, alog_ref,
                md_ref, mng_ref, wb_ref, wo_ref, o_ref, sc_ext, xbc_ext, state_ref, y_ref, *, rows):
    @pl.when(pl.program_id(1) == 0)
    def _():
        sc_ext[0:HALO, :] = jnp.zeros((HALO, SC_W), F32)
        xbc_ext[0:HALO, :] = jnp.zeros((HALO, M_XBC), F32)
        state_ref[...] = jnp.zeros(state_ref.shape, F32)

    nchunk = rows // M_CHUNK
    row_i = lax.broadcasted_iota(I32, (M_CHUNK, M_CHUNK), 0)
    col_i = lax.broadcasted_iota(I32, (M_CHUNK, M_CHUNK), 1)
    tri = row_i >= col_i

    o_b = scb_ref[...].astype(F32) * _causal_conv(
        sc_ext, scc_ref[...].astype(F32) * scx_ref[...].astype(F32), scw_ref, SC_KERNEL, rows)

    guv = _gelu(guv_ref[...].astype(F32))
    gu = guv[:, :GM_W]
    gv = _rms(guv[:, GM_W:], gvg_ref[...]).astype(BF16)
    mixed_rows = []
    for ci in range(nchunk):
        r0 = ci * GM_CHUNK
        cols = []
        for gi in range(GM_GROUPS):
            vb = gv[r0:r0 + GM_CHUNK, gi * GM_GROUP_W:(gi + 1) * GM_GROUP_W]
            cols.append(_dot(ws_ref[gi], vb) + wsb_ref[:, gi:gi + 1])
        mixed_rows.append(jnp.concatenate(cols, axis=-1))
    o_c = gu * jnp.concatenate(mixed_rows, axis=0)

    xbc = _silu(_causal_conv(xbc_ext, xbc_ref[...].astype(F32), mcw_ref, M_CONV, rows) + mcb_ref[...])
    head_lane = lax.broadcasted_iota(I32, (1, LANES), 1) < M_HEADS
    dt = jnp.where(head_lane, _softplus(dt_ref[...] + dtb_ref[...]), 0.0)
    a_row = jnp.where(head_lane, -jnp.exp(alog_ref[...]), 0.0)
    tri_f = jnp.where(tri, 1.0, 0.0).astype(F32)
    for ci in range(nchunk):
        r0 = ci * M_CHUNK
        dtc = dt[r0:r0 + M_CHUNK, :]
        acs = _dot_f32(tri_f, dtc * a_row)
        acs_t = jnp.transpose(acs)
        b_t = jnp.transpose(xbc[r0:r0 + M_CHUNK, M_DINNER:M_DINNER + M_BC]).astype(BF16)
        cm = xbc[r0:r0 + M_CHUNK, M_DINNER + M_BC:M_XBC].astype(BF16)
        for g in range(M_GROUPS):
            bg_t = b_t[g * M_STATE:(g + 1) * M_STATE, :]
            cg = cm[:, g * M_STATE:(g + 1) * M_STATE]
            cb = _dot(cg, bg_t)
            for h in range(g * (M_HEADS // M_GROUPS), (g + 1) * (M_HEADS // M_GROUPS)):
                col = acs[:, h:h + 1]
                row = acs_t[h:h + 1, :]
                last = acs[M_CHUNK - 1:M_CHUNK, h:h + 1]
                decay = jnp.exp(jnp.where(tri, col - row, NEG_INF))
                xs_h = xbc[r0:r0 + M_CHUNK, h * M_HEAD_DIM:(h + 1) * M_HEAD_DIM]
                xdt = xs_h * dtc[:, h:h + 1]
                y = _dot((cb * decay).astype(BF16), xdt.astype(BF16))
                prev_t = state_ref[h]
                y = y + _dot(cg, prev_t.astype(BF16)) * jnp.exp(col)
                st_t = _dot(bg_t, (xdt * jnp.exp(last - col)).astype(BF16))
                state_ref[h] = prev_t * jnp.exp(last) + st_t
                y_ref[r0:r0 + M_CHUNK, h * M_HEAD_DIM:(h + 1) * M_HEAD_DIM] = y + md_ref[:, h:h + 1] * xs_h
    o_d = _rms(y_ref[...] * _silu(mz_ref[...].astype(F32)), mng_ref[...])

    branches = (oa_ref[...], o_b.astype(BF16), o_c.astype(BF16), o_d.astype(BF16))
    gate_refs = (g0_ref, g1_ref, g2_ref, g3_ref)
    merged = None
    for i in range(N_BRANCH):
        term = _sigmoid(gate_refs[i][...].astype(F32)) * _dot(branches[i], wb_ref[i])
        merged = term if merged is None else merged + term
    o_ref[...] = x_ref[...] + gate_ref[0] * _dot(merged.astype(BF16), wo_ref[...])


def _mix(x2, proj, small, o_a, gate1, lw, bsz, seq):
    d = x2.shape[1]
    rows = min(256, seq)
    ns = seq // rows
    tok = lambda w, c: pl.BlockSpec((rows, w), lambda b, s: (b * ns + s, c))
    full = lambda a: pl.BlockSpec(a.shape, lambda b, s: (0,) * a.ndim)
    weights = [lw["sc_conv_w"], lw["g_v_norm_g"], lw["ws"], lw["ws_b"], lw["m_conv_w"], lw["m_conv_b"],
               lw["m_dt_bias"], lw["m_a_log"], lw["m_d"], lw["m_norm_g"], lw["w_branch"], lw["w_out"]]
    return pl.pallas_call(
        functools.partial(_mix_kernel, rows=rows),
        grid=(bsz, ns),
        in_specs=[tok(d, 0)] + [tok(D_MODEL, i) for i in range(N_BRANCH)]
        + [tok(2 * GM_W, P_GUV // (2 * GM_W)), tok(SC_W, P_SCB // SC_W), tok(SC_W, P_SCC // SC_W),
           tok(SC_W, P_SCX // SC_W), tok(M_DINNER, P_MZ // M_DINNER), tok(M_XBC, P_XBC // M_XBC),
           tok(LANES, S_DT // LANES), tok(A_Q, 0),
           pl.BlockSpec((1, 1, d), lambda b, s: (b, 0, 0))]
        + [full(w) for w in weights],
        out_specs=tok(d, 0),
        out_shape=jax.ShapeDtypeStruct(x2.shape, F32),
        scratch_shapes=[pltpu.VMEM((HALO + rows, SC_W), F32), pltpu.VMEM((HALO + rows, M_XBC), F32),
                        pltpu.VMEM((M_HEADS, M_STATE, M_HEAD_DIM), F32), pltpu.VMEM((rows, M_DINNER), F32)],
        compiler_params=_params("arbitrary", "arbitrary"),
        name="mixers_merge",
    )(x2, proj, proj, proj, proj, proj, proj, proj, proj, proj, proj, small, o_a,
      gate1.reshape(bsz, 1, d), *weights)


def _swap_lanes(x, lane, dist):
    up = pltpu.roll(x, LANES - dist, 1)
    down = pltpu.roll(x, dist, 1)
    return jnp.where((lane % (2 * dist)) < dist, up, down)


def _route(scores, bias_row):
    lane = lax.broadcasted_iota(I32, scores.shape, 1)
    valid = lane < N_EXPERTS
    sel = jnp.where(valid, scores + bias_row, NEG_INF)
    p1 = _swap_lanes(sel, lane, 1)
    hi1, lo1 = jnp.maximum(sel, p1), jnp.minimum(sel, p1)
    hi2, lo2 = _swap_lanes(hi1, lane, 2), _swap_lanes(lo1, lane, 2)
    grp = jnp.maximum(hi1, hi2) + jnp.maximum(jnp.minimum(hi1, hi2), jnp.maximum(lo1, lo2))
    grp = jnp.where(valid, grp, NEG_INF)
    best_val = jnp.max(grp, axis=-1, keepdims=True)
    big = jnp.int32(LANES)
    best = jnp.min(jnp.where(grp == best_val, lane // EXPERTS_PER_GROUP, big), axis=-1, keepdims=True)
    masked = jnp.where((lane // EXPERTS_PER_GROUP == best) & valid, sel, NEG_INF)
    v1 = jnp.max(masked, axis=-1, keepdims=True)
    i1 = jnp.min(jnp.where(masked == v1, lane, big), axis=-1, keepdims=True)
    masked2 = jnp.where(lane == i1, NEG_INF, masked)
    v2 = jnp.max(masked2, axis=-1, keepdims=True)
    i2 = jnp.min(jnp.where(masked2 == v2, lane, big), axis=-1, keepdims=True)
    w1 = jnp.sum(jnp.where(lane == i1, scores, 0.0), axis=-1, keepdims=True)
    w2 = jnp.sum(jnp.where(lane == i2, scores, 0.0), axis=-1, keepdims=True)
    tot = w1 + w2
    return jnp.where(lane == i1, w1 / tot, 0.0) + jnp.where(lane == i2, w2 / tot, 0.0), best


MOE_CHUNK = 128


def _dot_tn(a, b):
    return lax.dot_general(a, b, (((0,), (0,)), ((), ())), preferred_element_type=F32)


def _moe_kernel(x_ref, g_ref, sc_ref, sh_ref, gate_ref, rwh_ref, rwl_ref, rb_ref, tri_ref, wgu_ref, wd_ref, o_ref,
                h_ref, c2_ref, grp_ref, rank_ref, tot_ref):
    grp_id = pl.program_id(1)
    tm = x_ref.shape[0]
    lane = lax.broadcasted_iota(I32, (tm, LANES), 1)

    @pl.when(grp_id == 0)
    def _():
        h = _rms(x_ref[...], g_ref[...]) * (1.0 + sc_ref[0]) + sh_ref[0]
        hh, hl = _split(h)
        h_ref[...] = hh
        comb, best = _route(_sigmoid(_dot3(hh, hl, rwh_ref[...], rwl_ref[...])), rb_ref[...])
        c_hi, c_lo = _split(comb)
        c2_ref[:, :LANES] = c_hi
        c2_ref[:, LANES:] = c_lo
        grp_ref[...] = jnp.broadcast_to(best, (tm, LANES))
        onehot = jnp.where(lane == best, 1.0, 0.0)
        offs = jnp.zeros((1, LANES), F32)
        for b in range(tm // MOE_CHUNK):
            blk = onehot[b * MOE_CHUNK:(b + 1) * MOE_CHUNK, :]
            ranks = _dot(tri_ref[...], blk.astype(BF16)) + offs
            own = jnp.sum(blk * ranks, axis=-1, keepdims=True)
            rank_ref[b * MOE_CHUNK:(b + 1) * MOE_CHUNK, :] = jnp.broadcast_to(own, (MOE_CHUNK, LANES))
            offs = offs + jnp.sum(blk, axis=0, keepdims=True)
        tot_ref[0:1, :] = offs
        o_ref[...] = jnp.zeros(o_ref.shape, F32)

    lane1 = lax.broadcasted_iota(I32, (1, LANES), 1)
    n_tok = jnp.sum(jnp.where(lane1 == grp_id, tot_ref[0:1, :], 0.0)).astype(I32)
    n_chunks = lax.div(n_tok + (MOE_CHUNK - 1), jnp.int32(MOE_CHUNK))
    lane_f = lane.astype(F32)
    lane_c = lax.broadcasted_iota(I32, (MOE_CHUNK, LANES), 1)

    def chunk_body(c, carry):
        base = (c * MOE_CHUNK).astype(F32)
        pick = ((rank_ref[...] - base) == lane_f) & (grp_ref[...] == grp_id)
        st = jnp.where(pick, 1.0, 0.0).astype(BF16)
        xc = _dot_tn(st, h_ref[...]).astype(BF16)
        wc2 = _dot_tn(st, c2_ref[...])
        wc = wc2[:, :LANES] + wc2[:, LANES:]
        y = jnp.zeros((MOE_CHUNK, x_ref.shape[1]), F32)
        for el in range(EXPERTS_PER_GROUP):
            gu = _dot(xc, wgu_ref[el])
            hid = _silu(gu[:, :D_FF_EXPERT]) * gu[:, D_FF_EXPERT:]
            w_e = jnp.sum(jnp.where(lane_c == grp_id * EXPERTS_PER_GROUP + el, wc, 0.0), axis=-1, keepdims=True)
            y = y + w_e * _dot(hid.astype(BF16), wd_ref[el])
        y_hi, y_lo = _split(y)
        o_ref[...] += _dot(jnp.concatenate([st, st], axis=1), jnp.concatenate([y_hi, y_lo], axis=0))
        return carry
    lax.fori_loop(0, n_chunks, chunk_body, 0)

    @pl.when(grp_id == pl.num_programs(1) - 1)
    def _():
        o_ref[...] = x_ref[...] + gate_ref[0] * o_ref[...]


def _moe(x2, seq, g, scale, shift, gate, rw_hi, rw_lo, rb, wgu, wd):
    ntok, d = x2.shape
    bsz = scale.shape[0]
    tm = min(1024, seq)
    row = lambda i, e: (i, 0)
    per_b = lambda i, e: ((i * tm) // seq, 0, 0)
    vec = pl.BlockSpec((1, 1, d), per_b)
    const2 = lambda a: pl.BlockSpec(a.shape, lambda i, e: (0, 0))
    tri = jnp.asarray(np.tril(np.ones((MOE_CHUNK, MOE_CHUNK), np.float32), -1), BF16)
    return pl.pallas_call(
        _moe_kernel,
        grid=(ntok // tm, N_EXPERT_GROUPS),
        in_specs=[pl.BlockSpec((tm, d), row), pl.BlockSpec((1, d), lambda i, e: (0, 0)), vec, vec, vec,
                  const2(rw_hi), const2(rw_lo), const2(rb), const2(tri),
                  pl.BlockSpec((EXPERTS_PER_GROUP, d, 2 * D_FF_EXPERT), lambda i, e: (e, 0, 0)),
                  pl.BlockSpec((EXPERTS_PER_GROUP, D_FF_EXPERT, d), lambda i, e: (e, 0, 0))],
        out_specs=pl.BlockSpec((tm, d), row),
        out_shape=jax.ShapeDtypeStruct(x2.shape, F32),
        scratch_shapes=[pltpu.VMEM((tm, d), BF16), pltpu.VMEM((tm, 2 * LANES), BF16),
                        pltpu.VMEM((tm, LANES), I32), pltpu.VMEM((tm, LANES), F32), pltpu.VMEM((SUBLANES, LANES), F32)],
        compiler_params=_params("arbitrary", "arbitrary"),
        name="moe",
    )(x2, g.reshape(1, d), scale.reshape(bsz, 1, d), shift.reshape(bsz, 1, d), gate.reshape(bsz, 1, d),
      rw_hi, rw_lo, rb, tri, wgu, wd)


def _pad_lanes(v, width=LANES):
    v = v.reshape(1, -1)
    return jnp.pad(v, ((0, 0), (0, width - v.shape[1])))


def _layer_weights(w_in, sc_conv_w, g_v_norm_g, g_spatial_w, g_spatial_b, m_conv_w, m_conv_b, m_dt_bias,
                   m_a_log, m_d, m_norm_g, w_branch, w_out):
    splits = (A_Q, A_KV, A_KV, IDX_Q, IDX_DIM, IDX_HEADS, SC_W, SC_W, SC_W, 2 * GM_W, M_DINNER, M_XBC, M_HEADS,
              N_BRANCH * D_MODEL)
    pts = [int(p) for p in np.cumsum(splits)[:-1]]
    (q, k, v, qi, ki, wi, sc_b, sc_c, sc_x, g_uv, m_z, m_xbc, m_dt, gates) = jnp.split(w_in, pts, axis=-1)
    d = w_in.shape[0]
    w_main = jnp.concatenate([gates, g_uv, q, sc_b, sc_c, sc_x, m_z, m_xbc, k, v], axis=-1).astype(BF16)
    w_small = jnp.concatenate(
        [qi, ki, wi, jnp.zeros((d, LANES - IDX_DIM - IDX_HEADS), F32), m_dt, jnp.zeros((d, LANES - M_HEADS), F32)],
        axis=-1)
    ws_hi = w_small.astype(BF16)
    ws_lo = (w_small - ws_hi.astype(F32)).astype(BF16)
    tri = np.tril(np.ones((GM_CHUNK, GM_CHUNK), dtype=bool))
    return dict(
        w_main=w_main, ws_hi=ws_hi, ws_lo=ws_lo,
        sc_conv_w=sc_conv_w, g_v_norm_g=g_v_norm_g.reshape(1, GM_W),
        ws=jnp.where(tri[None], g_spatial_w, 0.0).astype(BF16), ws_b=g_spatial_b.T,
        m_conv_w=m_conv_w, m_conv_b=m_conv_b.reshape(1, M_XBC),
        m_dt_bias=_pad_lanes(m_dt_bias), m_a_log=_pad_lanes(m_a_log), m_d=_pad_lanes(m_d),
        m_norm_g=m_norm_g.reshape(1, M_DINNER), w_branch=w_branch.astype(BF16), w_out=w_out.astype(BF16))


def kernel(x, c, positions, ada_w, ada_b, norm1_g, w_in, q_norm_g, k_norm_g, sc_conv_w, g_v_norm_g, g_spatial_w,
           g_spatial_b, m_conv_w, m_conv_b, m_dt_bias, m_a_log, m_d, m_norm_g, w_branch, w_out, norm2_g, router_w,
           router_bias, exp_w_gate, exp_w_up, exp_w_down):
    bsz, seq, d = x.shape
    depth = ada_w.shape[0]
    mod = _modulation(c, ada_w, ada_b)
    rw = jnp.pad(router_w, ((0, 0), (0, LANES - N_EXPERTS)))
    rw_hi = rw.astype(BF16)
    rw_lo = (rw - rw_hi.astype(F32)).astype(BF16)
    rb = _pad_lanes(router_bias)
    pos2 = positions.reshape(bsz * seq, 1)
    x2 = x.reshape(bsz * seq, d)
    for l in range(depth):
        shift1, scale1, gate1, shift2, scale2, gate2 = jnp.split(mod[l], 6, axis=-1)
        lw = _layer_weights(w_in[l], sc_conv_w[l], g_v_norm_g[l], g_spatial_w[l], g_spatial_b[l], m_conv_w[l],
                            m_conv_b[l], m_dt_bias[l], m_a_log[l], m_d[l], m_norm_g[l], w_branch[l], w_out[l])
        proj, small = _inproj(x2, seq, norm1_g[l], scale1, shift1, lw["w_main"], lw["ws_hi"], lw["ws_lo"])
        q_pad, k_pad, v_t, qi_cat, ki_cat = _prep(proj, small, pos2, seq, q_norm_g[l], k_norm_g[l])
        o_a = _dsa(qi_cat, small, q_pad, ki_cat, k_pad, v_t, bsz, seq)
        x2 = _mix(x2, proj, small, o_a, gate1, lw, bsz, seq)
        wgu = jnp.concatenate([exp_w_gate[l], exp_w_up[l]], axis=-1).astype(BF16)
        x2 = _moe(x2, seq, norm2_g[l], scale2, shift2, gate2, rw_hi, rw_lo, rb, wgu, exp_w_down[l].astype(BF16))
    return x2.reshape(bsz, seq, d)
```

```python
import functools
import math

import jax
import jax.numpy as jnp
import numpy as np
from jax import lax
from jax.experimental import pallas as pl
from jax.experimental.pallas import tpu as pltpu

F32 = jnp.float32
BF16 = jnp.bfloat16
I32 = jnp.int32

D_MODEL = 1024
RMS_EPS = 1e-6
ROPE_THETA = 500000.0
Q_BLOCK = 128
A_HEADS = 8
A_KV_HEADS = 2
A_HEAD_DIM = 64
A_ROT = A_HEAD_DIM // 4
A_REP = A_HEADS // A_KV_HEADS
IDX_HEADS = 8
IDX_DIM = 32
IDX_ROT = IDX_DIM // 4
IDX_TOPK_MAX = 256
SC_W = 512
SC_KERNEL = 3
GM_W = 512
GM_GROUPS = 4
GM_GROUP_W = GM_W // GM_GROUPS
GM_CHUNK = 128
M_DINNER = 512
M_HEAD_DIM = 64
M_HEADS = M_DINNER // M_HEAD_DIM
M_GROUPS = 2
M_STATE = 64
M_CONV = 4
M_CHUNK = 128
M_BC = M_GROUPS * M_STATE
M_XBC = M_DINNER + 2 * M_BC
N_BRANCH = 4
BRANCH_W = 512
N_EXPERTS = 16
N_EXPERT_GROUPS = 4
EXPERTS_PER_GROUP = N_EXPERTS // N_EXPERT_GROUPS
D_FF_EXPERT = 512
A_Q = A_HEADS * A_HEAD_DIM
A_KV = A_KV_HEADS * A_HEAD_DIM
IDX_Q = IDX_HEADS * IDX_DIM

LANES = 128
SUBLANES = 8
VMEM_LIMIT = 56 * 1024 * 1024

P_GATES = 0
P_GUV = 4096
P_Q = 5120
P_SCB = 5632
P_SCC = 6144
P_SCX = 6656
P_MZ = 7168
P_XBC = 7680
P_K = 8448
P_V = 8576
P_COLS = 8704
S_QI = 0
S_KW = 256
S_DT = 384
S_COLS = 512

INT_MIN = -2147483648
NEG_INF = float("-inf")
NEG_INF_KEY = -2139095041


def _dot(a, b):
    return jnp.dot(a, b, preferred_element_type=F32)


def _dot_nt(a, b):
    return lax.dot_general(a, b, (((1,), (1,)), ((), ())), preferred_element_type=F32)


def _dot_f32(a, b):
    return jnp.dot(a, b, preferred_element_type=F32, precision=lax.Precision.HIGHEST)


def _split(x):
    hi = x.astype(BF16)
    lo = (x - hi.astype(F32)).astype(BF16)
    return hi, lo


def _dot3(a_hi, a_lo, b_hi, b_lo):
    return _dot(a_hi, b_hi) + _dot(a_lo, b_hi) + _dot(a_hi, b_lo)


def _sigmoid(x):
    return 1.0 / (1.0 + jnp.exp(-x))


def _silu(x):
    return x * _sigmoid(x)


def _softplus(x):
    return jnp.maximum(x, 0.0) + jnp.log1p(jnp.exp(-jnp.abs(x)))


def _gelu(x):
    return 0.5 * x * (1.0 + lax.erf(x * np.float32(np.sqrt(0.5))))


def _rms(x, g_row):
    ms = jnp.mean(x * x, axis=-1, keepdims=True)
    return x * lax.rsqrt(ms + RMS_EPS) * g_row


def _params(*sem):
    return pltpu.CompilerParams(dimension_semantics=sem, vmem_limit_bytes=VMEM_LIMIT)


def _mod_kernel(c_ref, w_ref, b_ref, o_ref):
    o_ref[0] = _dot_f32(_silu(c_ref[...]), w_ref[0]) + b_ref[0]


def _modulation(c, ada_w, ada_b):
    depth, d, n = ada_w.shape
    bsz = c.shape[0]
    tn = 768
    return pl.pallas_call(
        _mod_kernel,
        grid=(depth, n // tn),
        in_specs=[pl.BlockSpec((bsz, d), lambda l, j: (0, 0)),
                  pl.BlockSpec((1, d, tn), lambda l, j: (l, 0, j)),
                  pl.BlockSpec((1, 1, tn), lambda l, j: (l, 0, j))],
        out_specs=pl.BlockSpec((1, bsz, tn), lambda l, j: (l, 0, j)),
        out_shape=jax.ShapeDtypeStruct((depth, bsz, n), F32),
        compiler_params=_params("arbitrary", "arbitrary"),
        name="modulation",
    )(c, ada_w, ada_b.reshape(depth, 1, n))


def _inproj_kernel(x_ref, g_ref, sc_ref, sh_ref, w_ref, wsh_ref, wsl_ref, o_ref, os_ref, hh_ref, hl_ref):
    @pl.when(pl.program_id(1) == 0)
    def _():
        h = _rms(x_ref[...], g_ref[...]) * (1.0 + sc_ref[0]) + sh_ref[0]
        hh, hl = _split(h)
        hh_ref[...] = hh
        hl_ref[...] = hl
        os_ref[...] = _dot3(hh, hl, wsh_ref[...], wsl_ref[...])

    o_ref[...] = _dot(hh_ref[...], w_ref[...]).astype(BF16)


def _inproj(x2, seq, g, scale, shift, w_main, ws_hi, ws_lo):
    ntok, d = x2.shape
    tm = min(1024, seq)
    tn = P_COLS // 4
    bsz = scale.shape[0]
    row = lambda i, j: (i, 0)
    per_b = lambda i, j: ((i * tm) // seq, 0, 0)
    return pl.pallas_call(
        _inproj_kernel,
        grid=(ntok // tm, P_COLS // tn),
        in_specs=[pl.BlockSpec((tm, d), row),
                  pl.BlockSpec((1, d), lambda i, j: (0, 0)),
                  pl.BlockSpec((1, 1, d), per_b),
                  pl.BlockSpec((1, 1, d), per_b),
                  pl.BlockSpec((d, tn), lambda i, j: (0, j)),
                  pl.BlockSpec((d, S_COLS), lambda i, j: (0, 0)),
                  pl.BlockSpec((d, S_COLS), lambda i, j: (0, 0))],
        out_specs=[pl.BlockSpec((tm, tn), lambda i, j: (i, j)),
                   pl.BlockSpec((tm, S_COLS), row)],
        out_shape=[jax.ShapeDtypeStruct((ntok, P_COLS), BF16),
                   jax.ShapeDtypeStruct((ntok, S_COLS), F32)],
        scratch_shapes=[pltpu.VMEM((tm, d), BF16), pltpu.VMEM((tm, d), BF16)],
        compiler_params=_params("arbitrary", "arbitrary"),
        name="inproj",
    )(x2, g.reshape(1, d), scale.reshape(bsz, 1, d), shift.reshape(bsz, 1, d), w_main, ws_hi, ws_lo)


ROPE_ZERO_LANE = A_ROT // 2 + IDX_ROT // 2


def _rope_angles(pos_f):
    half_a, half_i = A_ROT // 2, IDX_ROT // 2
    lane = lax.broadcasted_iota(I32, (1, LANES), 1)
    k = jnp.where(lane < half_a, lane, lane - half_a).astype(F32)
    rot = jnp.where(lane < half_a, np.float32(A_ROT), np.float32(IDX_ROT))
    inv_freq = jnp.exp(np.float32(-math.log(ROPE_THETA)) * k * 2.0 / rot)
    ang = pos_f * jnp.where(lane < half_a + half_i, inv_freq, 0.0)
    return jnp.cos(ang), jnp.sin(ang)


def _rope_tables(cos_sin, place_ref, period, rot, limit=LANES):
    half = rot // 2

    def expand(x):
        hi, lo = _split(x)
        return _dot(hi, place_ref[...]) + _dot(lo, place_ref[...])

    lane_all = lax.broadcasted_iota(I32, (1, LANES), 1)
    lane = lane_all % period
    cos = jnp.where(lane_all < limit, expand(cos_sin[0]), 1.0)
    sin = jnp.where(lane_all < limit, expand(cos_sin[1]), 0.0)
    sin_lo = jnp.where(lane < half, -sin, 0.0)
    sin_hi = jnp.where((lane >= half) & (lane < rot), sin, 0.0)
    return cos, sin_lo, sin_hi


def _rope(x, tables, half):
    cos, sin_lo, sin_hi = tables
    width = x.shape[-1]
    return (x * cos + pltpu.roll(x, width - half, 1) * sin_lo + pltpu.roll(x, half, 1) * sin_hi)


def _prep_kernel(q_ref, k_ref, v_ref, qi_ref, kw_ref, pos_ref, qg_ref, kg_ref, bdq_ref, bdk_ref,
                 pq_ref, pk_ref, pqi1_ref, pqi2_ref, pki1_ref, pki2_ref, pra_ref, pri_ref,
                 qo_ref, ko_ref, vto_ref, qio_ref, kio_ref):
    cos_sin = _rope_angles(pos_ref[...].astype(F32))
    att_tab = _rope_tables(cos_sin, pra_ref, A_HEAD_DIM, A_ROT)
    idx_tab = _rope_tables(cos_sin, pri_ref, IDX_DIM, IDX_ROT)
    kw_tab = _rope_tables(cos_sin, pri_ref, IDX_DIM, IDX_ROT, limit=IDX_DIM)

    def head_norm(x, bd_ref, g_ref):
        sq_hi, sq_lo = _split(x * x)
        ms = (_dot(sq_hi, bd_ref[...]) + _dot(sq_lo, bd_ref[...])) * np.float32(1.0 / A_HEAD_DIM)
        return x * lax.rsqrt(ms + RMS_EPS) * g_ref[...]

    def rope_wide(x, tab, half):
        pieces = [_rope(x[:, c * LANES:(c + 1) * LANES], tab, half) for c in range(x.shape[-1] // LANES)]
        return pieces[0] if len(pieces) == 1 else jnp.concatenate(pieces, axis=-1)

    def store_head_rows(o_ref, x):
        for blk in range(x.shape[0] // Q_BLOCK):
            for h in range(x.shape[1] // LANES):
                r0 = (blk * (x.shape[1] // LANES) + h) * Q_BLOCK
                o_ref[r0:r0 + Q_BLOCK, :] = x[blk * Q_BLOCK:(blk + 1) * Q_BLOCK, h * LANES:(h + 1) * LANES]

    q = rope_wide(head_norm(q_ref[...].astype(F32), bdq_ref, qg_ref), att_tab, A_ROT // 2)
    q = (q * np.float32(A_HEAD_DIM ** -0.5)).astype(BF16)
    store_head_rows(qo_ref, _dot(q, pq_ref[...]).astype(BF16))
    k = rope_wide(head_norm(k_ref[...].astype(F32), bdk_ref, kg_ref), att_tab, A_ROT // 2).astype(BF16)
    ko_ref[...] = _dot(k, pk_ref[...]).astype(BF16)
    v_pad = _dot(v_ref[...], pk_ref[...])
    ones_lane = (lax.broadcasted_iota(I32, (1, A_KV_HEADS * LANES), 1) % LANES) == A_HEAD_DIM
    vto_ref[...] = jnp.transpose(jnp.where(ones_lane, 1.0, v_pad)).astype(BF16)

    qi_hi, qi_lo = _split(rope_wide(qi_ref[...], idx_tab, IDX_ROT // 2))
    store_head_rows(qio_ref, (_dot(qi_hi, pqi1_ref[...]) + _dot(qi_lo, pqi2_ref[...])).astype(BF16))
    ki_hi, ki_lo = _split(_rope(kw_ref[...], kw_tab, IDX_ROT // 2))
    kio_ref[...] = (_dot(ki_hi, pki1_ref[...]) + _dot(ki_lo, pki2_ref[...])).astype(BF16)


@functools.lru_cache(maxsize=None)
def _prep_constants():
    def block_diag(width):
        i = np.arange(width)
        return (i[:, None] // A_HEAD_DIM == i[None, :] // A_HEAD_DIM).astype(np.float32)

    def pad_heads(n_heads):
        p = np.zeros((n_heads * A_HEAD_DIM, n_heads * LANES), np.float32)
        for h in range(n_heads):
            for d in range(A_HEAD_DIM):
                p[h * A_HEAD_DIM + d, h * LANES + d] = 1.0
        return p

    pqi1 = np.zeros((IDX_Q, IDX_HEADS * LANES), np.float32)
    pqi2 = np.zeros((IDX_Q, IDX_HEADS * LANES), np.float32)
    for h in range(IDX_HEADS):
        for d in range(IDX_DIM):
            pqi1[h * IDX_DIM + d, h * LANES + d] = 1.0
            pqi1[h * IDX_DIM + d, h * LANES + IDX_DIM + d] = 1.0
            pqi2[h * IDX_DIM + d, h * LANES + 2 * IDX_DIM + d] = 1.0
    pki1 = np.zeros((LANES, LANES), np.float32)
    pki2 = np.zeros((LANES, LANES), np.float32)
    for d in range(IDX_DIM):
        pki1[d, d] = 1.0
        pki1[d, 2 * IDX_DIM + d] = 1.0
        pki2[d, IDX_DIM + d] = 1.0
    def rope_place(period, rot, first_src):
        p = np.zeros((LANES, LANES), np.float32)
        for j in range(LANES):
            jj = j % period
            p[first_src + jj % (rot // 2) if jj < rot else ROPE_ZERO_LANE, j] = 1.0
        return p

    mats = (block_diag(A_Q), block_diag(A_KV), pad_heads(A_HEADS), pad_heads(A_KV_HEADS), pqi1, pqi2, pki1, pki2,
            rope_place(A_HEAD_DIM, A_ROT, 0), rope_place(IDX_DIM, IDX_ROT, A_ROT // 2))
    return tuple(np.asarray(m) for m in mats)


def _prep(proj, small, pos2, seq, q_g, k_g):
    ntok = proj.shape[0]
    tm = min(512, seq)
    consts = [jnp.asarray(m, BF16) for m in _prep_constants()]
    row = lambda w, c: pl.BlockSpec((tm, w), lambda i: (i, c))
    full = lambda a: pl.BlockSpec(a.shape, lambda i: (0, 0))
    qg = jnp.tile(q_g, A_HEADS).reshape(1, A_Q)
    kg = jnp.tile(k_g, A_KV_HEADS).reshape(1, A_KV)
    return pl.pallas_call(
        _prep_kernel,
        grid=(ntok // tm,),
        in_specs=[row(A_Q, P_Q // A_Q), row(A_KV, P_K // A_KV), row(A_KV, P_V // A_KV), row(IDX_Q, S_QI // IDX_Q),
                  row(LANES, S_KW // LANES), row(1, 0), full(qg), full(kg)] + [full(m) for m in consts],
        out_specs=[pl.BlockSpec((tm * A_HEADS, LANES), lambda i: (i, 0)), row(A_KV_HEADS * LANES, 0),
                   pl.BlockSpec((A_KV_HEADS * LANES, tm), lambda i: (0, i)),
                   pl.BlockSpec((tm * IDX_HEADS, LANES), lambda i: (i, 0)), row(LANES, 0)],
        out_shape=[jax.ShapeDtypeStruct((ntok * A_HEADS, LANES), BF16),
                   jax.ShapeDtypeStruct((ntok, A_KV_HEADS * LANES), BF16),
                   jax.ShapeDtypeStruct((A_KV_HEADS * LANES, ntok), BF16),
                   jax.ShapeDtypeStruct((ntok * IDX_HEADS, LANES), BF16),
                   jax.ShapeDtypeStruct((ntok, LANES), BF16)],
        compiler_params=_params("arbitrary"),
        name="attn_prep",
    )(proj, proj, proj, small, small, pos2, qg, kg, *consts)


def _sortable(x):
    bits = pltpu.bitcast(jnp.where(x == 0.0, 0.0, x), I32)
    return jnp.where(bits < 0, bits ^ jnp.int32(0x7FFFFFFF), bits)


KEY_STEP = 512
KEY_SUBS = KEY_STEP // Q_BLOCK


def _key_row0(step, sub):
    return pl.multiple_of(step * KEY_STEP + sub * Q_BLOCK, Q_BLOCK)


def _count_rows(pred_fn, nstep):
    def body(c, acc):
        for sub in range(KEY_SUBS):
            ones = jnp.where(pred_fn(_key_row0(c, sub)), jnp.int32(1), jnp.int32(0))
            acc = acc + jnp.sum(ones.reshape(Q_BLOCK // SUBLANES, SUBLANES, LANES), axis=0)
        return acc
    acc = lax.fori_loop(0, nstep, body, jnp.zeros((SUBLANES, LANES), I32))
    return jnp.sum(acc, axis=0, keepdims=True)


PAIR = 2 * LANES
N_PAIRS = A_HEADS // 2


def _dsa_kernel(qi_ref, kw_ref, q_ref, ki_ref, k_ref, vt_ref, o_ref, key_ref, s_ref, lim_ref, oacc_ref,
                *, topk, idx_bits):
    qb = pl.program_id(1)
    nstep = lax.div(qb, jnp.int32(KEY_SUBS)) + 1
    row_i = lax.broadcasted_iota(I32, (Q_BLOCK, LANES), 0)
    q_pos = qb * Q_BLOCK + lax.broadcasted_iota(I32, (Q_BLOCK, LANES), 1)

    w_rows = jnp.transpose(kw_ref[...])[IDX_DIM:IDX_DIM + IDX_HEADS, :]
    w_rows = w_rows * np.float32(IDX_HEADS ** -0.5) * np.float32(IDX_DIM ** -0.5)

    def rows(r0):
        return pl.ds(r0, Q_BLOCK)

    def score_body(c, carry):
        for sub in range(KEY_SUBS):
            r0 = _key_row0(c, sub)
            ki = ki_ref[rows(r0), :]
            s = jnp.zeros((Q_BLOCK, LANES), F32)
            for pair in range(IDX_HEADS // 2):
                d = _dot_nt(ki, qi_ref[pair * PAIR:(pair + 1) * PAIR, :])
                for j in range(2):
                    h = 2 * pair + j
                    s = s + jnp.maximum(d[:, j * LANES:(j + 1) * LANES], 0.0) * w_rows[h:h + 1, :]
            key_ref[rows(r0), :] = _sortable(jnp.where((r0 + row_i) <= q_pos, s, NEG_INF))
        return carry
    lax.fori_loop(0, nstep, score_body, 0)

    def bit_body(i, cand):
        trial = cand | lax.shift_left(jnp.int32(1), 31 - i)
        trial_s = trial ^ jnp.int32(INT_MIN)
        cnt = _count_rows(lambda r0: key_ref[rows(r0), :] >= trial_s, nstep)
        return jnp.where(cnt >= topk, trial, cand)
    cand = lax.fori_loop(0, 32, bit_body, jnp.zeros((1, LANES), I32))
    thr = cand ^ jnp.int32(INT_MIN)

    n_gt = _count_rows(lambda r0: key_ref[rows(r0), :] > thr, nstep)
    n_eq = _count_rows(lambda r0: key_ref[rows(r0), :] == thr, nstep)
    need = topk - n_gt
    excess = (n_eq > need) & (thr > jnp.int32(NEG_INF_KEY))
    lim_ref[0:1, :] = jnp.full((1, LANES), 2 ** 30, I32)

    @pl.when(jnp.max(jnp.where(excess, 1, 0)) > 0)
    def _():
        def idx_body(i, lim):
            trial = lim | lax.shift_left(jnp.int32(1), idx_bits - 1 - i)
            cnt = _count_rows(lambda r0: (key_ref[rows(r0), :] == thr) & ((r0 + row_i) < trial), nstep)
            return jnp.where(cnt < need, trial, lim)
        lim = lax.fori_loop(0, idx_bits, idx_body, jnp.zeros((1, LANES), I32))
        lim_ref[0:1, :] = jnp.where(excess, lim, 2 ** 30)
    tie_lim = lim_ref[0:1, :]

    def max_body(c, m_all):
        m_all = list(m_all)
        for sub in range(KEY_SUBS):
            r0 = _key_row0(c, sub)
            keys = key_ref[rows(r0), :]
            kpos = r0 + row_i
            sel = ((keys > thr) | ((keys == thr) & (kpos <= tie_lim))) & (kpos <= q_pos)
            bias = jnp.where(sel, 0.0, NEG_INF)
            bias2 = jnp.concatenate([bias, bias], axis=-1)
            for pair in range(N_PAIRS):
                g = (2 * pair) // A_REP
                s = _dot_nt(k_ref[rows(r0), g * LANES:(g + 1) * LANES], q_ref[pair * PAIR:(pair + 1) * PAIR, :]) + bias2
                s_ref[rows(r0), pair * PAIR:(pair + 1) * PAIR] = s
                m_all[pair] = jnp.maximum(m_all[pair], jnp.max(s, axis=0, keepdims=True))
        return tuple(m_all)
    m_fin = lax.fori_loop(0, nstep, max_body, tuple(jnp.full((1, PAIR), -1e30, F32) for _ in range(N_PAIRS)))

    oacc_ref[...] = jnp.zeros(oacc_ref.shape, F32)

    def pv_body(c, carry):
        cols = pl.ds(pl.multiple_of(c * KEY_STEP, KEY_STEP), KEY_STEP)
        for pair in range(N_PAIRS):
            g = (2 * pair) // A_REP
            p = jnp.exp(s_ref[cols, pair * PAIR:(pair + 1) * PAIR] - m_fin[pair]).astype(BF16)
            oacc_ref[pair] += _dot(vt_ref[g * LANES:(g + 1) * LANES, cols], p)
        return carry
    lax.fori_loop(0, nstep, pv_body, 0)

    outs = []
    for pair in range(N_PAIRS):
        o_p = oacc_ref[pair]
        for j in range(2):
            o_h = o_p[:, j * LANES:(j + 1) * LANES]
            outs.append(o_h[0:A_HEAD_DIM, :] / o_h[A_HEAD_DIM:A_HEAD_DIM + 1, :])
    o_ref[...] = jnp.transpose(jnp.concatenate(outs, axis=0)).astype(BF16)


def _dsa(qi_cat, small, q_pad, ki_cat, k_pad, v_t, bsz, seq):
    assert seq % KEY_STEP == 0
    nb = seq // Q_BLOCK
    topk = min(IDX_TOPK_MAX, seq // 4)
    idx_bits = max(1, int(math.ceil(math.log2(seq))))
    qrow = lambda w, c: pl.BlockSpec((Q_BLOCK, w), lambda b, j: (b * nb + j, c))
    brow = lambda w: pl.BlockSpec((seq, w), lambda b, j: (b, 0))
    return pl.pallas_call(
        functools.partial(_dsa_kernel, topk=topk, idx_bits=idx_bits),
        grid=(bsz, nb),
        in_specs=[pl.BlockSpec((IDX_HEADS * Q_BLOCK, LANES), lambda b, j: (b * nb + j, 0)),
                  qrow(LANES, S_KW // LANES),
                  pl.BlockSpec((A_HEADS * Q_BLOCK, LANES), lambda b, j: (b * nb + j, 0)),
                  brow(LANES), brow(A_KV_HEADS * LANES),
                  pl.BlockSpec((A_KV_HEADS * LANES, seq), lambda b, j: (0, b))],
        out_specs=qrow(A_Q, 0),
        out_shape=jax.ShapeDtypeStruct((bsz * seq, A_Q), BF16),
        scratch_shapes=[pltpu.VMEM((seq, LANES), I32), pltpu.VMEM((seq, A_HEADS * LANES), F32),
                        pltpu.VMEM((SUBLANES, LANES), I32), pltpu.VMEM((N_PAIRS, LANES, PAIR), F32)],
        compiler_params=_params("arbitrary", "arbitrary"),
        name="sparse_attention",
    )(qi_cat, small, q_pad, ki_cat, k_pad, v_t)


HALO = SUBLANES


def _causal_conv(ext_ref, u, w_ref, taps, rows):
    ext_ref[HALO:HALO + rows, :] = u
    y = u * w_ref[taps - 1:taps, :]
    for j in range(taps - 1):
        shift = taps - 1 - j
        y = y + ext_ref[HALO - shift:HALO - shift + rows, :] * w_ref[j:j + 1, :]
    ext_ref[0:HALO, :] = ext_ref[rows:rows + HALO, :]
    return y


def _mix_kernel(x_ref, g0_ref, g1_ref, g2_ref, g3_ref, guv_ref, scb_ref, scc_ref, scx_ref, mz_ref, xbc_ref,
                dt_ref, oa_ref, gate_ref, scw_ref, gvg_ref, ws_ref, wsb_ref, mcw_ref, mcb_ref, dtb_ref, alog_ref,
                md_ref, mng_ref, wb_ref, wo_ref, o_ref, sc_ext, xbc_ext, state_ref, y_ref, *, rows):
    @pl.when(pl.program_id(1) == 0)
    def _():
        sc_ext[0:HALO, :] = jnp.zeros((HALO, SC_W), F32)
        xbc_ext[0:HALO, :] = jnp.zeros((HALO, M_XBC), F32)
        state_ref[...] = jnp.zeros(state_ref.shape, F32)

    nchunk = rows // M_CHUNK
    row_i = lax.broadcasted_iota(I32, (M_CHUNK, M_CHUNK), 0)
    col_i = lax.broadcasted_iota(I32, (M_CHUNK, M_CHUNK), 1)
    tri = row_i >= col_i

    o_b = scb_ref[...].astype(F32) * _causal_conv(
        sc_ext, scc_ref[...].astype(F32) * scx_ref[...].astype(F32), scw_ref, SC_KERNEL, rows)

    guv = _gelu(guv_ref[...].astype(F32))
    gu = guv[:, :GM_W]
    gv = _rms(guv[:, GM_W:], gvg_ref[...]).astype(BF16)
    mixed_rows = []
    for ci in range(nchunk):
        r0 = ci * GM_CHUNK
        cols = []
        for gi in range(GM_GROUPS):
            vb = gv[r0:r0 + GM_CHUNK, gi * GM_GROUP_W:(gi + 1) * GM_GROUP_W]
            cols.append(_dot(ws_ref[gi], vb) + wsb_ref[:, gi:gi + 1])
        mixed_rows.append(jnp.concatenate(cols, axis=-1))
    o_c = gu * jnp.concatenate(mixed_rows, axis=0)

    xbc = _silu(_causal_conv(xbc_ext, xbc_ref[...].astype(F32), mcw_ref, M_CONV, rows) + mcb_ref[...])
    head_lane = lax.broadcasted_iota(I32, (1, LANES), 1) < M_HEADS
    dt = jnp.where(head_lane, _softplus(dt_ref[...] + dtb_ref[...]), 0.0)
    a_row = jnp.where(head_lane, -jnp.exp(alog_ref[...]), 0.0)
    tri_f = jnp.where(tri, 1.0, 0.0).astype(F32)
    for ci in range(nchunk):
        r0 = ci * M_CHUNK
        dtc = dt[r0:r0 + M_CHUNK, :]
        acs = _dot_f32(tri_f, dtc * a_row)
        acs_t = jnp.transpose(acs)
        b_t = jnp.transpose(xbc[r0:r0 + M_CHUNK, M_DINNER:M_DINNER + M_BC]).astype(BF16)
        cm = xbc[r0:r0 + M_CHUNK, M_DINNER + M_BC:M_XBC].astype(BF16)
        for g in range(M_GROUPS):
            bg_t = b_t[g * M_STATE:(g + 1) * M_STATE, :]
            cg = cm[:, g * M_STATE:(g + 1) * M_STATE]
            cb = _dot(cg, bg_t)
            for h in range(g * (M_HEADS // M_GROUPS), (g + 1) * (M_HEADS // M_GROUPS)):
                col = acs[:, h:h + 1]
                row = acs_t[h:h + 1, :]
                last = acs[M_CHUNK - 1:M_CHUNK, h:h + 1]
                decay = jnp.exp(jnp.where(tri, col - row, NEG_INF))
                xs_h = xbc[r0:r0 + M_CHUNK, h * M_HEAD_DIM:(h + 1) * M_HEAD_DIM]
                xdt = xs_h * dtc[:, h:h + 1]
                y = _dot((cb * decay).astype(BF16), xdt.astype(BF16))
                prev_t = state_ref[h]
                y = y + _dot(cg, prev_t.astype(BF16)) * jnp.exp(col)
                st_t = _dot(bg_t, (xdt * jnp.exp(last - col)).astype(BF16))
                state_ref[h] = prev_t * jnp.exp(last) + st_t
                y_ref[r0:r0 + M_CHUNK, h * M_HEAD_DIM:(h + 1) * M_HEAD_DIM] = y + md_ref[:, h:h + 1] * xs_h
    o_d = _rms(y_ref[...] * _silu(mz_ref[...].astype(F32)), mng_ref[...])

    branches = (oa_ref[...], o_b.astype(BF16), o_c.astype(BF16), o_d.astype(BF16))
    gate_refs = (g0_ref, g1_ref, g2_ref, g3_ref)
    merged = None
    for i in range(N_BRANCH):
        term = _sigmoid(gate_refs[i][...].astype(F32)) * _dot(branches[i], wb_ref[i])
        merged = term if merged is None else merged + term
    o_ref[...] = x_ref[...] + gate_ref[0] * _dot(merged.astype(BF16), wo_ref[...])


def _mix(x2, proj, small, o_a, gate1, lw, bsz, seq):
    d = x2.shape[1]
    rows = min(256, seq)
    ns = seq // rows
    tok = lambda w, c: pl.BlockSpec((rows, w), lambda b, s: (b * ns + s, c))
    full = lambda a: pl.BlockSpec(a.shape, lambda b, s: (0,) * a.ndim)
    weights = [lw["sc_conv_w"], lw["g_v_norm_g"], lw["ws"], lw["ws_b"], lw["m_conv_w"], lw["m_conv_b"],
               lw["m_dt_bias"], lw["m_a_log"], lw["m_d"], lw["m_norm_g"], lw["w_branch"], lw["w_out"]]
    return pl.pallas_call(
        functools.partial(_mix_kernel, rows=rows),
        grid=(bsz, ns),
        in_specs=[tok(d, 0)] + [tok(D_MODEL, i) for i in range(N_BRANCH)]
        + [tok(2 * GM_W, P_GUV // (2 * GM_W)), tok(SC_W, P_SCB // SC_W), tok(SC_W, P_SCC // SC_W),
           tok(SC_W, P_SCX // SC_W), tok(M_DINNER, P_MZ // M_DINNER), tok(M_XBC, P_XBC // M_XBC),
           tok(LANES, S_DT // LANES), tok(A_Q, 0),
           pl.BlockSpec((1, 1, d), lambda b, s: (b, 0, 0))]
        + [full(w) for w in weights],
        out_specs=tok(d, 0),
        out_shape=jax.ShapeDtypeStruct(x2.shape, F32),
        scratch_shapes=[pltpu.VMEM((HALO + rows, SC_W), F32), pltpu.VMEM((HALO + rows, M_XBC), F32),
                        pltpu.VMEM((M_HEADS, M_STATE, M_HEAD_DIM), F32), pltpu.VMEM((rows, M_DINNER), F32)],
        compiler_params=_params("arbitrary", "arbitrary"),
        name="mixers_merge",
    )(x2, proj, proj, proj, proj, proj, proj, proj, proj, proj, proj, small, o_a,
      gate1.reshape(bsz, 1, d), *weights)


def _swap_lanes(x, lane, dist):
    up = pltpu.roll(x, LANES - dist, 1)
    down = pltpu.roll(x, dist, 1)
    return jnp.where((lane % (2 * dist)) < dist, up, down)


def _route(scores, bias_row):
    lane = lax.broadcasted_iota(I32, scores.shape, 1)
    valid = lane < N_EXPERTS
    sel = jnp.where(valid, scores + bias_row, NEG_INF)
    p1 = _swap_lanes(sel, lane, 1)
    hi1, lo1 = jnp.maximum(sel, p1), jnp.minimum(sel, p1)
    hi2, lo2 = _swap_lanes(hi1, lane, 2), _swap_lanes(lo1, lane, 2)
    grp = jnp.maximum(hi1, hi2) + jnp.maximum(jnp.minimum(hi1, hi2), jnp.maximum(lo1, lo2))
    grp = jnp.where(valid, grp, NEG_INF)
    best_val = jnp.max(grp, axis=-1, keepdims=True)
    big = jnp.int32(LANES)
    best = jnp.min(jnp.where(grp == best_val, lane // EXPERTS_PER_GROUP, big), axis=-1, keepdims=True)
    masked = jnp.where((lane // EXPERTS_PER_GROUP == best) & valid, sel, NEG_INF)
    v1 = jnp.max(masked, axis=-1, keepdims=True)
    i1 = jnp.min(jnp.where(masked == v1, lane, big), axis=-1, keepdims=True)
    masked2 = jnp.where(lane == i1, NEG_INF, masked)
    v2 = jnp.max(masked2, axis=-1, keepdims=True)
    i2 = jnp.min(jnp.where(masked2 == v2, lane, big), axis=-1, keepdims=True)
    w1 = jnp.sum(jnp.where(lane == i1, scores, 0.0), axis=-1, keepdims=True)
    w2 = jnp.sum(jnp.where(lane == i2, scores, 0.0), axis=-1, keepdims=True)
    tot = w1 + w2
    return jnp.where(lane == i1, w1 / tot, 0.0) + jnp.where(lane == i2, w2 / tot, 0.0), best


MOE_CHUNK = 128


def _dot_tn(a, b):
    return lax.dot_general(a, b, (((0,), (0,)), ((), ())), preferred_element_type=F32)


def _moe_kernel(x_ref, g_ref, sc_ref, sh_ref, gate_ref, rwh_ref, rwl_ref, rb_ref, tri_ref, wgu_ref, wd_ref, o_ref,
                h_ref, c2_ref, grp_ref, rank_ref, tot_ref):
    grp_id = pl.program_id(1)
    tm = x_ref.shape[0]
    lane = lax.broadcasted_iota(I32, (tm, LANES), 1)

    @pl.when(grp_id == 0)
    def _():
        h = _rms(x_ref[...], g_ref[...]) * (1.0 + sc_ref[0]) + sh_ref[0]
        hh, hl = _split(h)
        h_ref[...] = hh
        comb, best = _route(_sigmoid(_dot3(hh, hl, rwh_ref[...], rwl_ref[...])), rb_ref[...])
        c_hi, c_lo = _split(comb)
        c2_ref[:, :LANES] = c_hi
        c2_ref[:, LANES:] = c_lo
        grp_ref[...] = jnp.broadcast_to(best, (tm, LANES))
        onehot = jnp.where(lane == best, 1.0, 0.0)
        offs = jnp.zeros((1, LANES), F32)
        for b in range(tm // MOE_CHUNK):
            blk = onehot[b * MOE_CHUNK:(b + 1) * MOE_CHUNK, :]
            ranks = _dot(tri_ref[...], blk.astype(BF16)) + offs
            own = jnp.sum(blk * ranks, axis=-1, keepdims=True)
            rank_ref[b * MOE_CHUNK:(b + 1) * MOE_CHUNK, :] = jnp.broadcast_to(own, (MOE_CHUNK, LANES))
            offs = offs + jnp.sum(blk, axis=0, keepdims=True)
        tot_ref[0:1, :] = offs
        o_ref[...] = jnp.zeros(o_ref.shape, F32)

    lane1 = lax.broadcasted_iota(I32, (1, LANES), 1)
    n_tok = jnp.sum(jnp.where(lane1 == grp_id, tot_ref[0:1, :], 0.0)).astype(I32)
    n_chunks = lax.div(n_tok + (MOE_CHUNK - 1), jnp.int32(MOE_CHUNK))
    lane_f = lane.astype(F32)
    lane_c = lax.broadcasted_iota(I32, (MOE_CHUNK, LANES), 1)

    def chunk_body(c, carry):
        base = (c * MOE_CHUNK).astype(F32)
        pick = ((rank_ref[...] - base) == lane_f) & (grp_ref[...] == grp_id)
        st = jnp.where(pick, 1.0, 0.0).astype(BF16)
        xc = _dot_tn(st, h_ref[...]).astype(BF16)
        wc2 = _dot_tn(st, c2_ref[...])
        wc = wc2[:, :LANES] + wc2[:, LANES:]
        y = jnp.zeros((MOE_CHUNK, x_ref.shape[1]), F32)
        for el in range(EXPERTS_PER_GROUP):
            gu = _dot(xc, wgu_ref[el])
            hid = _silu(gu[:, :D_FF_EXPERT]) * gu[:, D_FF_EXPERT:]
            w_e = jnp.sum(jnp.where(lane_c == grp_id * EXPERTS_PER_GROUP + el, wc, 0.0), axis=-1, keepdims=True)
            y = y + w_e * _dot(hid.astype(BF16), wd_ref[el])
        y_hi, y_lo = _split(y)
        o_ref[...] += _dot(jnp.concatenate([st, st], axis=1), jnp.concatenate([y_hi, y_lo], axis=0))
        return carry
    lax.fori_loop(0, n_chunks, chunk_body, 0)

    @pl.when(grp_id == pl.num_programs(1) - 1)
    def _():
        o_ref[...] = x_ref[...] + gate_ref[0] * o_ref[...]


def _moe(x2, seq, g, scale, shift, gate, rw_hi, rw_lo, rb, wgu, wd):
    ntok, d = x2.shape
    bsz = scale.shape[0]
    tm = min(1024, seq)
    row = lambda i, e: (i, 0)
    per_b = lambda i, e: ((i * tm) // seq, 0, 0)
    vec = pl.BlockSpec((1, 1, d), per_b)
    const2 = lambda a: pl.BlockSpec(a.shape, lambda i, e: (0, 0))
    tri = jnp.asarray(np.tril(np.ones((MOE_CHUNK, MOE_CHUNK), np.float32), -1), BF16)
    return pl.pallas_call(
        _moe_kernel,
        grid=(ntok // tm, N_EXPERT_GROUPS),
        in_specs=[pl.BlockSpec((tm, d), row), pl.BlockSpec((1, d), lambda i, e: (0, 0)), vec, vec, vec,
                  const2(rw_hi), const2(rw_lo), const2(rb), const2(tri),
                  pl.BlockSpec((EXPERTS_PER_GROUP, d, 2 * D_FF_EXPERT), lambda i, e: (e, 0, 0)),
                  pl.BlockSpec((EXPERTS_PER_GROUP, D_FF_EXPERT, d), lambda i, e: (e, 0, 0))],
        out_specs=pl.BlockSpec((tm, d), row),
        out_shape=jax.ShapeDtypeStruct(x2.shape, F32),
        scratch_shapes=[pltpu.VMEM((tm, d), BF16), pltpu.VMEM((tm, 2 * LANES), BF16),
                        pltpu.VMEM((tm, LANES), I32), pltpu.VMEM((tm, LANES), F32), pltpu.VMEM((SUBLANES, LANES), F32)],
        compiler_params=_params("arbitrary", "arbitrary"),
        name="moe",
    )(x2, g.reshape(1, d), scale.reshape(bsz, 1, d), shift.reshape(bsz, 1, d), gate.reshape(bsz, 1, d),
      rw_hi, rw_lo, rb, tri, wgu, wd)


def _pad_lanes(v, width=LANES):
    v = v.reshape(1, -1)
    return jnp.pad(v, ((0, 0), (0, width - v.shape[1])))


def _layer_weights(w_in, sc_conv_w, g_v_norm_g, g_spatial_w, g_spatial_b, m_conv_w, m_conv_b, m_dt_bias,
                   m_a_log, m_d, m_norm_g, w_branch, w_out):
    splits = (A_Q, A_KV, A_KV, IDX_Q, IDX_DIM, IDX_HEADS, SC_W, SC_W, SC_W, 2 * GM_W, M_DINNER, M_XBC, M_HEADS,
              N_BRANCH * D_MODEL)
    pts = [int(p) for p in np.cumsum(splits)[:-1]]
    (q, k, v, qi, ki, wi, sc_b, sc_c, sc_x, g_uv, m_z, m_xbc, m_dt, gates) = jnp.split(w_in, pts, axis=-1)
    d = w_in.shape[0]
    w_main = jnp.concatenate([gates, g_uv, q, sc_b, sc_c, sc_x, m_z, m_xbc, k, v], axis=-1).astype(BF16)
    w_small = jnp.concatenate(
        [qi, ki, wi, jnp.zeros((d, LANES - IDX_DIM - IDX_HEADS), F32), m_dt, jnp.zeros((d, LANES - M_HEADS), F32)],
        axis=-1)
    ws_hi = w_small.astype(BF16)
    ws_lo = (w_small - ws_hi.astype(F32)).astype(BF16)
    tri = np.tril(np.ones((GM_CHUNK, GM_CHUNK), dtype=bool))
    return dict(
        w_main=w_main, ws_hi=ws_hi, ws_lo=ws_lo,
        sc_conv_w=sc_conv_w, g_v_norm_g=g_v_norm_g.reshape(1, GM_W),
        ws=jnp.where(tri[None], g_spatial_w, 0.0).astype(BF16), ws_b=g_spatial_b.T,
        m_conv_w=m_conv_w, m_conv_b=m_conv_b.reshape(1, M_XBC),
        m_dt_bias=_pad_lanes(m_dt_bias), m_a_log=_pad_lanes(m_a_log), m_d=_pad_lanes(m_d),
        m_norm_g=m_norm_g.reshape(1, M_DINNER), w_branch=w_branch.astype(BF16), w_out=w_out.astype(BF16))


def kernel(x, c, positions, ada_w, ada_b, norm1_g, w_in, q_norm_g, k_norm_g, sc_conv_w, g_v_norm_g, g_spatial_w,
           g_spatial_b, m_conv_w, m_conv_b, m_dt_bias, m_a_log, m_d, m_norm_g, w_branch, w_out, norm2_g, router_w,
           router_bias, exp_w_gate, exp_w_up, exp_w_down):
    bsz, seq, d = x.shape
    depth = ada_w.shape[0]
    mod = _modulation(c, ada_w, ada_b)
    rw = jnp.pad(router_w, ((0, 0), (0, LANES - N_EXPERTS)))
    rw_hi = rw.astype(BF16)
    rw_lo = (rw - rw_hi.astype(F32)).astype(BF16)
    rb = _pad_lanes(router_bias)
    pos2 = positions.reshape(bsz * seq, 1)
    x2 = x.reshape(bsz * seq, d)
    for l in range(depth):
        shift1, scale1, gate1, shift2, scale2, gate2 = jnp.split(mod[l], 6, axis=-1)
        lw = _layer_weights(w_in[l], sc_conv_w[l], g_v_norm_g[l], g_spatial_w[l], g_spatial_b[l], m_conv_w[l],
                            m_conv_b[l], m_dt_bias[l], m_a_log[l], m_d[l], m_norm_g[l], w_branch[l], w_out[l])
        proj, small = _inproj(x2, seq, norm1_g[l], scale1, shift1, lw["w_main"], lw["ws_hi"], lw["ws_lo"])
        q_pad, k_pad, v_t, qi_cat, ki_cat = _prep(proj, small, pos2, seq, q_norm_g[l], k_norm_g[l])
        o_a = _dsa(qi_cat, small, q_pad, ki_cat, k_pad, v_t, bsz, seq)
        x2 = _mix(x2, proj, small, o_a, gate1, lw, bsz, seq)
        wgu = jnp.concatenate([exp_w_gate[l], exp_w_up[l]], axis=-1).astype(BF16)
        x2 = _moe(x2, seq, norm2_g[l], scale2, shift2, gate2, rw_hi, rw_lo, rb, wgu, exp_w_down[l].astype(BF16))
    return x2.reshape(bsz, seq, d)
```

```python
import functools
import math

import jax
import jax.numpy as jnp
import numpy as np
from jax import lax
from jax.experimental import pallas as pl
from jax.experimental.pallas import tpu as pltpu

F32 = jnp.float32
BF16 = jnp.bfloat16
I32 = jnp.int32

D_MODEL = 1024
RMS_EPS = 1e-6
ROPE_THETA = 500000.0
Q_BLOCK = 128
A_HEADS = 8
A_KV_HEADS = 2
A_HEAD_DIM = 64
A_ROT = A_HEAD_DIM // 4
A_REP = A_HEADS // A_KV_HEADS
IDX_HEADS = 8
IDX_DIM = 32
IDX_ROT = IDX_DIM // 4
IDX_TOPK_MAX = 256
SC_W = 512
SC_KERNEL = 3
GM_W = 512
GM_GROUPS = 4
GM_GROUP_W = GM_W // GM_GROUPS
GM_CHUNK = 128
M_DINNER = 512
M_HEAD_DIM = 64
M_HEADS = M_DINNER // M_HEAD_DIM
M_GROUPS = 2
M_STATE = 64
M_CONV = 4
M_CHUNK = 128
M_BC = M_GROUPS * M_STATE
M_XBC = M_DINNER + 2 * M_BC
N_BRANCH = 4
BRANCH_W = 512
N_EXPERTS = 16
N_EXPERT_GROUPS = 4
EXPERTS_PER_GROUP = N_EXPERTS // N_EXPERT_GROUPS
D_FF_EXPERT = 512
A_Q = A_HEADS * A_HEAD_DIM
A_KV = A_KV_HEADS * A_HEAD_DIM
IDX_Q = IDX_HEADS * IDX_DIM

LANES = 128
SUBLANES = 8
VMEM_LIMIT = 56 * 1024 * 1024

P_GATES = 0
P_GUV = 4096
P_Q = 5120
P_SCB = 5632
P_SCC = 6144
P_SCX = 6656
P_MZ = 7168
P_XBC = 7680
P_K = 8448
P_V = 8576
P_COLS = 8704
S_QI = 0
S_KW = 256
S_DT = 384
S_COLS = 512

INT_MIN = -2147483648
NEG_INF = float("-inf")
NEG_INF_KEY = -2139095041


def _dot(a, b):
    return jnp.dot(a, b, preferred_element_type=F32)


def _dot_nt(a, b):
    return lax.dot_general(a, b, (((1,), (1,)), ((), ())), preferred_element_type=F32)


def _dot_f32(a, b):
    return jnp.dot(a, b, preferred_element_type=F32, precision=lax.Precision.HIGHEST)


def _split(x):
    hi = x.astype(BF16)
    lo = (x - hi.astype(F32)).astype(BF16)
    return hi, lo


def _dot3(a_hi, a_lo, b_hi, b_lo):
    return _dot(a_hi, b_hi) + _dot(a_lo, b_hi) + _dot(a_hi, b_lo)


def _sigmoid(x):
    return 1.0 / (1.0 + jnp.exp(-x))


def _silu(x):
    return x * _sigmoid(x)


def _softplus(x):
    return jnp.maximum(x, 0.0) + jnp.log1p(jnp.exp(-jnp.abs(x)))


def _gelu(x):
    return 0.5 * x * (1.0 + lax.erf(x * np.float32(np.sqrt(0.5))))


def _rms(x, g_row):
    ms = jnp.mean(x * x, axis=-1, keepdims=True)
    return x * lax.rsqrt(ms + RMS_EPS) * g_row


def _params(*sem):
    return pltpu.CompilerParams(dimension_semantics=sem, vmem_limit_bytes=VMEM_LIMIT)


def _mod_kernel(c_ref, w_ref, b_ref, o_ref):
    o_ref[0] = _dot_f32(_silu(c_ref[...]), w_ref[0]) + b_ref[0]


def _modulation(c, ada_w, ada_b):
    depth, d, n = ada_w.shape
    bsz = c.shape[0]
    tn = 768
    return pl.pallas_call(
        _mod_kernel,
        grid=(depth, n // tn),
        in_specs=[pl.BlockSpec((bsz, d), lambda l, j: (0, 0)),
                  pl.BlockSpec((1, d, tn), lambda l, j: (l, 0, j)),
                  pl.BlockSpec((1, 1, tn), lambda l, j: (l, 0, j))],
        out_specs=pl.BlockSpec((1, bsz, tn), lambda l, j: (l, 0, j)),
        out_shape=jax.ShapeDtypeStruct((depth, bsz, n), F32),
        compiler_params=_params("arbitrary", "arbitrary"),
        name="modulation",
    )(c, ada_w, ada_b.reshape(depth, 1, n))


def _inproj_kernel(x_ref, g_ref, sc_ref, sh_ref, w_ref, wsh_ref, wsl_ref, o_ref, os_ref, hh_ref, hl_ref):
    @pl.when(pl.program_id(1) == 0)
    def _():
        h = _rms(x_ref[...], g_ref[...]) * (1.0 + sc_ref[0]) + sh_ref[0]
        hh, hl = _split(h)
        hh_ref[...] = hh
        hl_ref[...] = hl
        os_ref[...] = _dot3(hh, hl, wsh_ref[...], wsl_ref[...])

    o_ref[...] = _dot(hh_ref[...], w_ref[...]).astype(BF16)


def _inproj(x2, seq, g, scale, shift, w_main, ws_hi, ws_lo):
    ntok, d = x2.shape
    tm = min(1024, seq)
    tn = P_COLS // 4
    bsz = scale.shape[0]
    row = lambda i, j: (i, 0)
    per_b = lambda i, j: ((i * tm) // seq, 0, 0)
    return pl.pallas_call(
        _inproj_kernel,
        grid=(ntok // tm, P_COLS // tn),
        in_specs=[pl.BlockSpec((tm, d), row),
                  pl.BlockSpec((1, d), lambda i, j: (0, 0)),
                  pl.BlockSpec((1, 1, d), per_b),
                  pl.BlockSpec((1, 1, d), per_b),
                  pl.BlockSpec((d, tn), lambda i, j: (0, j)),
                  pl.BlockSpec((d, S_COLS), lambda i, j: (0, 0)),
                  pl.BlockSpec((d, S_COLS), lambda i, j: (0, 0))],
        out_specs=[pl.BlockSpec((tm, tn), lambda i, j: (i, j)),
                   pl.BlockSpec((tm, S_COLS), row)],
        out_shape=[jax.ShapeDtypeStruct((ntok, P_COLS), BF16),
                   jax.ShapeDtypeStruct((ntok, S_COLS), F32)],
        scratch_shapes=[pltpu.VMEM((tm, d), BF16), pltpu.VMEM((tm, d), BF16)],
        compiler_params=_params("arbitrary", "arbitrary"),
        name="inproj",
    )(x2, g.reshape(1, d), scale.reshape(bsz, 1, d), shift.reshape(bsz, 1, d), w_main, ws_hi, ws_lo)


ROPE_ZERO_LANE = A_ROT // 2 + IDX_ROT // 2


def _rope_angles(pos_f):
    half_a, half_i = A_ROT // 2, IDX_ROT // 2
    lane = lax.broadcasted_iota(I32, (1, LANES), 1)
    k = jnp.where(lane < half_a, lane, lane - half_a).astype(F32)
    rot = jnp.where(lane < half_a, np.float32(A_ROT), np.float32(IDX_ROT))
    inv_freq = jnp.exp(np.float32(-math.log(ROPE_THETA)) * k * 2.0 / rot)
    ang = pos_f * jnp.where(lane < half_a + half_i, inv_freq, 0.0)
    return jnp.cos(ang), jnp.sin(ang)


def _rope_tables(cos_sin, place_ref, period, rot, limit=LANES):
    half = rot // 2

    def expand(x):
        hi, lo = _split(x)
        return _dot(hi, place_ref[...]) + _dot(lo, place_ref[...])

    lane_all = lax.broadcasted_iota(I32, (1, LANES), 1)
    lane = lane_all % period
    cos = jnp.where(lane_all < limit, expand(cos_sin[0]), 1.0)
    sin = jnp.where(lane_all < limit, expand(cos_sin[1]), 0.0)
    sin_lo = jnp.where(lane < half, -sin, 0.0)
    sin_hi = jnp.where((lane >= half) & (lane < rot), sin, 0.0)
    return cos, sin_lo, sin_hi


def _rope(x, tables, half):
    cos, sin_lo, sin_hi = tables
    width = x.shape[-1]
    return (x * cos + pltpu.roll(x, width - half, 1) * sin_lo + pltpu.roll(x, half, 1) * sin_hi)


def _prep_kernel(q_ref, k_ref, v_ref, qi_ref, kw_ref, pos_ref, qg_ref, kg_ref, bdq_ref, bdk_ref,
                 pq_ref, pk_ref, pqi1_ref, pqi2_ref, pki1_ref, pki2_ref, pra_ref, pri_ref,
                 qo_ref, ko_ref, vto_ref, qio_ref, kio_ref):
    cos_sin = _rope_angles(pos_ref[...].astype(F32))
    att_tab = _rope_tables(cos_sin, pra_ref, A_HEAD_DIM, A_ROT)
    idx_tab = _rope_tables(cos_sin, pri_ref, IDX_DIM, IDX_ROT)
    kw_tab = _rope_tables(cos_sin, pri_ref, IDX_DIM, IDX_ROT, limit=IDX_DIM)

    def head_norm(x, bd_ref, g_ref):
        sq_hi, sq_lo = _split(x * x)
        ms = (_dot(sq_hi, bd_ref[...]) + _dot(sq_lo, bd_ref[...])) * np.float32(1.0 / A_HEAD_DIM)
        return x * lax.rsqrt(ms + RMS_EPS) * g_ref[...]

    def rope_wide(x, tab, half):
        pieces = [_rope(x[:, c * LANES:(c + 1) * LANES], tab, half) for c in range(x.shape[-1] // LANES)]
        return pieces[0] if len(pieces) == 1 else jnp.concatenate(pieces, axis=-1)

    def store_head_rows(o_ref, x):
        for blk in range(x.shape[0] // Q_BLOCK):
            for h in range(x.shape[1] // LANES):
                r0 = (blk * (x.shape[1] // LANES) + h) * Q_BLOCK
                o_ref[r0:r0 + Q_BLOCK, :] = x[blk * Q_BLOCK:(blk + 1) * Q_BLOCK, h * LANES:(h + 1) * LANES]

    q = rope_wide(head_norm(q_ref[...].astype(F32), bdq_ref, qg_ref), att_tab, A_ROT // 2)
    q = (q * np.float32(A_HEAD_DIM ** -0.5)).astype(BF16)
    store_head_rows(qo_ref, _dot(q, pq_ref[...]).astype(BF16))
    k = rope_wide(head_norm(k_ref[...].astype(F32), bdk_ref, kg_ref), att_tab, A_ROT // 2).astype(BF16)
    ko_ref[...] = _dot(k, pk_ref[...]).astype(BF16)
    v_pad = _dot(v_ref[...], pk_ref[...])
    ones_lane = (lax.broadcasted_iota(I32, (1, A_KV_HEADS * LANES), 1) % LANES) == A_HEAD_DIM
    vto_ref[...] = jnp.transpose(jnp.where(ones_lane, 1.0, v_pad)).astype(BF16)

    qi_hi, qi_lo = _split(rope_wide(qi_ref[...], idx_tab, IDX_ROT // 2))
    store_head_rows(qio_ref, (_dot(qi_hi, pqi1_ref[...]) + _dot(qi_lo, pqi2_ref[...])).astype(BF16))
    ki_hi, ki_lo = _split(_rope(kw_ref[...], kw_tab, IDX_ROT // 2))
    kio_ref[...] = (_dot(ki_hi, pki1_ref[...]) + _dot(ki_lo, pki2_ref[...])).astype(BF16)


@functools.lru_cache(maxsize=None)
def _prep_constants():
    def block_diag(width):
        i = np.arange(width)
        return (i[:, None] // A_HEAD_DIM == i[None, :] // A_HEAD_DIM).astype(np.float32)

    def pad_heads(n_heads):
        p = np.zeros((n_heads * A_HEAD_DIM, n_heads * LANES), np.float32)
        for h in range(n_heads):
            for d in range(A_HEAD_DIM):
                p[h * A_HEAD_DIM + d, h * LANES + d] = 1.0
        return p

    pqi1 = np.zeros((IDX_Q, IDX_HEADS * LANES), np.float32)
    pqi2 = np.zeros((IDX_Q, IDX_HEADS * LANES), np.float32)
    for h in range(IDX_HEADS):
        for d in range(IDX_DIM):
            pqi1[h * IDX_DIM + d, h * LANES + d] = 1.0
            pqi1[h * IDX_DIM + d, h * LANES + IDX_DIM + d] = 1.0
            pqi2[h * IDX_DIM + d, h * LANES + 2 * IDX_DIM + d] = 1.0
    pki1 = np.zeros((LANES, LANES), np.float32)
    pki2 = np.zeros((LANES, LANES), np.float32)
    for d in range(IDX_DIM):
        pki1[d, d] = 1.0
        pki1[d, 2 * IDX_DIM + d] = 1.0
        pki2[d, IDX_DIM + d] = 1.0
    def rope_place(period, rot, first_src):
        p = np.zeros((LANES, LANES), np.float32)
        for j in range(LANES):
            jj = j % period
            p[first_src + jj % (rot // 2) if jj < rot else ROPE_ZERO_LANE, j] = 1.0
        return p

    mats = (block_diag(A_Q), block_diag(A_KV), pad_heads(A_HEADS), pad_heads(A_KV_HEADS), pqi1, pqi2, pki1, pki2,
            rope_place(A_HEAD_DIM, A_ROT, 0), rope_place(IDX_DIM, IDX_ROT, A_ROT // 2))
    return tuple(np.asarray(m) for m in mats)


def _prep(proj, small, pos2, seq, q_g, k_g):
    ntok = proj.shape[0]
    tm = min(512, seq)
    consts = [jnp.asarray(m, BF16) for m in _prep_constants()]
    row = lambda w, c: pl.BlockSpec((tm, w), lambda i: (i, c))
    full = lambda a: pl.BlockSpec(a.shape, lambda i: (0, 0))
    qg = jnp.tile(q_g, A_HEADS).reshape(1, A_Q)
    kg = jnp.tile(k_g, A_KV_HEADS).reshape(1, A_KV)
    return pl.pallas_call(
        _prep_kernel,
        grid=(ntok // tm,),
        in_specs=[row(A_Q, P_Q // A_Q), row(A_KV, P_K // A_KV), row(A_KV, P_V // A_KV), row(IDX_Q, S_QI // IDX_Q),
                  row(LANES, S_KW // LANES), row(1, 0), full(qg), full(kg)] + [full(m) for m in consts],
        out_specs=[pl.BlockSpec((tm * A_HEADS, LANES), lambda i: (i, 0)), row(A_KV_HEADS * LANES, 0),
                   pl.BlockSpec((A_KV_HEADS * LANES, tm), lambda i: (0, i)),
                   pl.BlockSpec((tm * IDX_HEADS, LANES), lambda i: (i, 0)), row(LANES, 0)],
        out_shape=[jax.ShapeDtypeStruct((ntok * A_HEADS, LANES), BF16),
                   jax.ShapeDtypeStruct((ntok, A_KV_HEADS * LANES), BF16),
                   jax.ShapeDtypeStruct((A_KV_HEADS * LANES, ntok), BF16),
                   jax.ShapeDtypeStruct((ntok * IDX_HEADS, LANES), BF16),
                   jax.ShapeDtypeStruct((ntok, LANES), BF16)],
        compiler_params=_params("arbitrary"),
        name="attn_prep",
    )(proj, proj, proj, small, small, pos2, qg, kg, *consts)


def _sortable(x):
    bits = pltpu.bitcast(jnp.where(x == 0.0, 0.0, x), I32)
    return jnp.where(bits < 0, bits ^ jnp.int32(0x7FFFFFFF), bits)


KEY_STEP = 512
KEY_SUBS = KEY_STEP // Q_BLOCK


def _key_row0(step, sub):
    return pl.multiple_of(step * KEY_STEP + sub * Q_BLOCK, Q_BLOCK)


def _loop_steps(nstep, body, init):
    pairs = lax.shift_right_logical(nstep, 1)

    def two(i, carry):
        return body(2 * i + 1, body(2 * i, carry))
    carry = lax.fori_loop(0, pairs, two, init)
    return lax.fori_loop(2 * pairs, nstep, body, carry)


def _count_rows(pred_fn, nstep):
    def body(c, acc):
        for sub in range(KEY_SUBS):
            ones = jnp.where(pred_fn(_key_row0(c, sub)), jnp.int32(1), jnp.int32(0))
            acc = acc + jnp.sum(ones.reshape(Q_BLOCK // SUBLANES, SUBLANES, LANES), axis=0)
        return acc
    acc = lax.fori_loop(0, nstep, body, jnp.zeros((SUBLANES, LANES), I32))
    return jnp.sum(acc, axis=0, keepdims=True)


PAIR = 2 * LANES
N_PAIRS = A_HEADS // 2


def _dsa_kernel(qi_ref, kw_ref, q_ref, ki_ref, k_ref, vt_ref, o_ref, key_ref, s_ref, lim_ref, oacc_ref,
                *, topk, idx_bits):
    qb = pl.program_id(1)
    nstep = lax.div(qb, jnp.int32(KEY_SUBS)) + 1
    row_i = lax.broadcasted_iota(I32, (Q_BLOCK, LANES), 0)
    q_pos = qb * Q_BLOCK + lax.broadcasted_iota(I32, (Q_BLOCK, LANES), 1)

    w_rows = jnp.transpose(kw_ref[...])[IDX_DIM:IDX_DIM + IDX_HEADS, :]
    w_rows = w_rows * np.float32(IDX_HEADS ** -0.5) * np.float32(IDX_DIM ** -0.5)

    def rows(r0):
        return pl.ds(r0, Q_BLOCK)

    def score_body(c, carry):
        for sub in range(KEY_SUBS):
            r0 = _key_row0(c, sub)
            ki = ki_ref[rows(r0), :]
            s = jnp.zeros((Q_BLOCK, LANES), F32)
            for pair in range(IDX_HEADS // 2):
                d = _dot_nt(ki, qi_ref[pair * PAIR:(pair + 1) * PAIR, :])
                for j in range(2):
                    h = 2 * pair + j
                    s = s + jnp.maximum(d[:, j * LANES:(j + 1) * LANES], 0.0) * w_rows[h:h + 1, :]
            key_ref[rows(r0), :] = _sortable(jnp.where((r0 + row_i) <= q_pos, s, NEG_INF))
        return carry
    _loop_steps(nstep, score_body, 0)

    def bit_body(i, cand):
        trial = cand | lax.shift_left(jnp.int32(1), 31 - i)
        trial_s = trial ^ jnp.int32(INT_MIN)
        cnt = _count_rows(lambda r0: key_ref[rows(r0), :] >= trial_s, nstep)
        return jnp.where(cnt >= topk, trial, cand)
    cand = lax.fori_loop(0, 32, bit_body, jnp.zeros((1, LANES), I32))
    thr = cand ^ jnp.int32(INT_MIN)

    n_gt = _count_rows(lambda r0: key_ref[rows(r0), :] > thr, nstep)
    n_eq = _count_rows(lambda r0: key_ref[rows(r0), :] == thr, nstep)
    need = topk - n_gt
    excess = (n_eq > need) & (thr > jnp.int32(NEG_INF_KEY))
    lim_ref[0:1, :] = jnp.full((1, LANES), 2 ** 30, I32)

    @pl.when(jnp.max(jnp.where(excess, 1, 0)) > 0)
    def _():
        def idx_body(i, lim):
            trial = lim | lax.shift_left(jnp.int32(1), idx_bits - 1 - i)
            cnt = _count_rows(lambda r0: (key_ref[rows(r0), :] == thr) & ((r0 + row_i) < trial), nstep)
            return jnp.where(cnt < need, trial, lim)
        lim = lax.fori_loop(0, idx_bits, idx_body, jnp.zeros((1, LANES), I32))
        lim_ref[0:1, :] = jnp.where(excess, lim, 2 ** 30)
    tie_lim = lim_ref[0:1, :]

    def max_body(c, m_all):
        m_all = list(m_all)
        for sub in range(KEY_SUBS):
            r0 = _key_row0(c, sub)
            keys = key_ref[rows(r0), :]
            kpos = r0 + row_i
            sel = ((keys > thr) | ((keys == thr) & (kpos <= tie_lim))) & (kpos <= q_pos)
            bias = jnp.where(sel, 0.0, NEG_INF)
            bias2 = jnp.concatenate([bias, bias], axis=-1)
            for pair in range(N_PAIRS):
                g = (2 * pair) // A_REP
                s = _dot_nt(k_ref[rows(r0), g * LANES:(g + 1) * LANES], q_ref[pair * PAIR:(pair + 1) * PAIR, :]) + bias2
                s_ref[rows(r0), pair * PAIR:(pair + 1) * PAIR] = s
                m_all[pair] = jnp.maximum(m_all[pair], jnp.max(s, axis=0, keepdims=True))
        return tuple(m_all)
    m_fin = _loop_steps(nstep, max_body, tuple(jnp.full((1, PAIR), -1e30, F32) for _ in range(N_PAIRS)))

    oacc_ref[...] = jnp.zeros(oacc_ref.shape, F32)

    def pv_body(c, carry):
        cols = pl.ds(pl.multiple_of(c * KEY_STEP, KEY_STEP), KEY_STEP)
        for pair in range(N_PAIRS):
            g = (2 * pair) // A_REP
            p = jnp.exp(s_ref[cols, pair * PAIR:(pair + 1) * PAIR] - m_fin[pair]).astype(BF16)
            oacc_ref[pair] += _dot(vt_ref[g * LANES:(g + 1) * LANES, cols], p)
        return carry
    _loop_steps(nstep, pv_body, 0)

    outs = []
    for pair in range(N_PAIRS):
        o_p = oacc_ref[pair]
        for j in range(2):
            o_h = o_p[:, j * LANES:(j + 1) * LANES]
            outs.append(o_h[0:A_HEAD_DIM, :] / o_h[A_HEAD_DIM:A_HEAD_DIM + 1, :])
    o_ref[...] = jnp.transpose(jnp.concatenate(outs, axis=0)).astype(BF16)


def _dsa(qi_cat, small, q_pad, ki_cat, k_pad, v_t, bsz, seq):
    assert seq % KEY_STEP == 0
    nb = seq // Q_BLOCK
    topk = min(IDX_TOPK_MAX, seq // 4)
    idx_bits = max(1, int(math.ceil(math.log2(seq))))
    qrow = lambda w, c: pl.BlockSpec((Q_BLOCK, w), lambda b, j: (b * nb + j, c))
    brow = lambda w: pl.BlockSpec((seq, w), lambda b, j: (b, 0))
    return pl.pallas_call(
        functools.partial(_dsa_kernel, topk=topk, idx_bits=idx_bits),
        grid=(bsz, nb),
        in_specs=[pl.BlockSpec((IDX_HEADS * Q_BLOCK, LANES), lambda b, j: (b * nb + j, 0)),
                  qrow(LANES, S_KW // LANES),
                  pl.BlockSpec((A_HEADS * Q_BLOCK, LANES), lambda b, j: (b * nb + j, 0)),
                  brow(LANES), brow(A_KV_HEADS * LANES),
                  pl.BlockSpec((A_KV_HEADS * LANES, seq), lambda b, j: (0, b))],
        out_specs=qrow(A_Q, 0),
        out_shape=jax.ShapeDtypeStruct((bsz * seq, A_Q), BF16),
        scratch_shapes=[pltpu.VMEM((seq, LANES), I32), pltpu.VMEM((seq, A_HEADS * LANES), F32),
                        pltpu.VMEM((SUBLANES, LANES), I32), pltpu.VMEM((N_PAIRS, LANES, PAIR), F32)],
        compiler_params=_params("arbitrary", "arbitrary"),
        name="sparse_attention",
    )(qi_cat, small, q_pad, ki_cat, k_pad, v_t)


HALO = SUBLANES


def _causal_conv(ext_ref, u, w_ref, taps, rows):
    ext_ref[HALO:HALO + rows, :] = u
    y = u * w_ref[taps - 1:taps, :]
    for j in range(taps - 1):
        shift = taps - 1 - j
        y = y + ext_ref[HALO - shift:HALO - shift + rows, :] * w_ref[j:j + 1, :]
    ext_ref[0:HALO, :] = ext_ref[rows:rows + HALO, :]
    return y


def _mix_kernel(x_ref, g0_ref, g1_ref, g2_ref, g3_ref, guv_ref, scb_ref, scc_ref, scx_ref, mz_ref, xbc_ref,
                dt_ref, oa_ref, gate_ref, scw_ref, gvg_ref, ws_ref, wsb_ref, mcw_ref, mcb_ref, dtb_ref, alog_ref,
                md_ref, mng_ref, wb_ref, wo_ref, o_ref, sc_ext, xbc_ext, state_ref, y_ref, *, rows):
    @pl.when(pl.program_id(1) == 0)
    def _():
        sc_ext[0:HALO, :] = jnp.zeros((HALO, SC_W), F32)
        xbc_ext[0:HALO, :] = jnp.zeros((HALO, M_XBC), F32)
        state_ref[...] = jnp.zeros(state_ref.shape, F32)

    nchunk = rows // M_CHUNK
    row_i = lax.broadcasted_iota(I32, (M_CHUNK, M_CHUNK), 0)
    col_i = lax.broadcasted_iota(I32, (M_CHUNK, M_CHUNK), 1)
    tri = row_i >= col_i

    o_b = scb_ref[...].astype(F32) * _causal_conv(
        sc_ext, scc_ref[...].astype(F32) * scx_ref[...].astype(F32), scw_ref, SC_KERNEL, rows)

    guv = _gelu(guv_ref[...].astype(F32))
    gu = guv[:, :GM_W]
    gv = _rms(guv[:, GM_W:], gvg_ref[...]).astype(BF16)
    mixed_rows = []
    for ci in range(nchunk):
        r0 = ci * GM_CHUNK
        cols = []
        for gi in range(GM_GROUPS):
            vb = gv[r0:r0 + GM_CHUNK, gi * GM_GROUP_W:(gi + 1) * GM_GROUP_W]
            cols.append(_dot(ws_ref[gi], vb) + wsb_ref[:, gi:gi + 1])
        mixed_rows.append(jnp.concatenate(cols, axis=-1))
    o_c = gu * jnp.concatenate(mixed_rows, axis=0)

    xbc = _silu(_causal_conv(xbc_ext, xbc_ref[...].astype(F32), mcw_ref, M_CONV, rows) + mcb_ref[...])
    head_lane = lax.broadcasted_iota(I32, (1, LANES), 1) < M_HEADS
    dt = jnp.where(head_lane, _softplus(dt_ref[...] + dtb_ref[...]), 0.0)
    a_row = jnp.where(head_lane, -jnp.exp(alog_ref[...]), 0.0)
    tri_f = jnp.where(tri, 1.0, 0.0).astype(F32)
    for ci in range(nchunk):
        r0 = ci * M_CHUNK
        dtc = dt[r0:r0 + M_CHUNK, :]
        acs = _dot_f32(tri_f, dtc * a_row)
        acs_t = jnp.transpose(acs)
        b_t = jnp.transpose(xbc[r0:r0 + M_CHUNK, M_DINNER:M_DINNER + M_BC]).astype(BF16)
        cm = xbc[r0:r0 + M_CHUNK, M_DINNER + M_BC:M_XBC].astype(BF16)
        for g in range(M_GROUPS):
            bg_t = b_t[g * M_STATE:(g + 1) * M_STATE, :]
            cg = cm[:, g * M_STATE:(g + 1) * M_STATE]
            cb = _dot(cg, bg_t)
            for h in range(g * (M_HEADS // M_GROUPS), (g + 1) * (M_HEADS // M_GROUPS)):
                col = acs[:, h:h + 1]
                row = acs_t[h:h + 1, :]
                last = acs[M_CHUNK - 1:M_CHUNK, h:h + 1]
                decay = jnp.exp(jnp.where(tri, col - row, NEG_INF))
                xs_h = xbc[r0:r0 + M_CHUNK, h * M_HEAD_DIM:(h + 1) * M_HEAD_DIM]
                xdt = xs_h * dtc[:, h:h + 1]
                y = _dot((cb * decay).astype(BF16), xdt.astype(BF16))
                prev_t = state_ref[h]
                y = y + _dot(cg, prev_t.astype(BF16)) * jnp.exp(col)
                st_t = _dot(bg_t, (xdt * jnp.exp(last - col)).astype(BF16))
                state_ref[h] = prev_t * jnp.exp(last) + st_t
                y_ref[r0:r0 + M_CHUNK, h * M_HEAD_DIM:(h + 1) * M_HEAD_DIM] = y + md_ref[:, h:h + 1] * xs_h
    o_d = _rms(y_ref[...] * _silu(mz_ref[...].astype(F32)), mng_ref[...])

    branches = (oa_ref[...], o_b.astype(BF16), o_c.astype(BF16), o_d.astype(BF16))
    gate_refs = (g0_ref, g1_ref, g2_ref, g3_ref)
    merged = None
    for i in range(N_BRANCH):
        term = _sigmoid(gate_refs[i][...].astype(F32)) * _dot(branches[i], wb_ref[i])
        merged = term if merged is None else merged + term
    o_ref[...] = x_ref[...] + gate_ref[0] * _dot(merged.astype(BF16), wo_ref[...])


def _mix(x2, proj, small, o_a, gate1, lw, bsz, seq):
    d = x2.shape[1]
    rows = min(256, seq)
    ns = seq // rows
    tok = lambda w, c: pl.BlockSpec((rows, w), lambda b, s: (b * ns + s, c))
    full = lambda a: pl.BlockSpec(a.shape, lambda b, s: (0,) * a.ndim)
    weights = [lw["sc_conv_w"], lw["g_v_norm_g"], lw["ws"], lw["ws_b"], lw["m_conv_w"], lw["m_conv_b"],
               lw["m_dt_bias"], lw["m_a_log"], lw["m_d"], lw["m_norm_g"], lw["w_branch"], lw["w_out"]]
    return pl.pallas_call(
        functools.partial(_mix_kernel, rows=rows),
        grid=(bsz, ns),
        in_specs=[tok(d, 0)] + [tok(D_MODEL, i) for i in range(N_BRANCH)]
        + [tok(2 * GM_W, P_GUV // (2 * GM_W)), tok(SC_W, P_SCB // SC_W), tok(SC_W, P_SCC // SC_W),
           tok(SC_W, P_SCX // SC_W), tok(M_DINNER, P_MZ // M_DINNER), tok(M_XBC, P_XBC // M_XBC),
           tok(LANES, S_DT // LANES), tok(A_Q, 0),
           pl.BlockSpec((1, 1, d), lambda b, s: (b, 0, 0))]
        + [full(w) for w in weights],
        out_specs=tok(d, 0),
        out_shape=jax.ShapeDtypeStruct(x2.shape, F32),
        scratch_shapes=[pltpu.VMEM((HALO + rows, SC_W), F32), pltpu.VMEM((HALO + rows, M_XBC), F32),
                        pltpu.VMEM((M_HEADS, M_STATE, M_HEAD_DIM), F32), pltpu.VMEM((rows, M_DINNER), F32)],
        compiler_params=_params("arbitrary", "arbitrary"),
        name="mixers_merge",
    )(x2, proj, proj, proj, proj, proj, proj, proj, proj, proj, proj, small, o_a,
      gate1.reshape(bsz, 1, d), *weights)


def _swap_lanes(x, lane, dist):
    up = pltpu.roll(x, LANES - dist, 1)
    down = pltpu.roll(x, dist, 1)
    return jnp.where((lane & (2 * dist - 1)) < dist, up, down)


def _route(scores, bias_row):
    lane = lax.broadcasted_iota(I32, scores.shape, 1)
    valid = lane < N_EXPERTS
    sel = jnp.where(valid, scores + bias_row, NEG_INF)
    p1 = _swap_lanes(sel, lane, 1)
    hi1, lo1 = jnp.maximum(sel, p1), jnp.minimum(sel, p1)
    hi2, lo2 = _swap_lanes(hi1, lane, 2), _swap_lanes(lo1, lane, 2)
    grp = jnp.maximum(hi1, hi2) + jnp.maximum(jnp.minimum(hi1, hi2), jnp.maximum(lo1, lo2))
    grp = jnp.where(valid, grp, NEG_INF)
    best_val = jnp.max(grp, axis=-1, keepdims=True)
    big = jnp.int32(LANES)
    lane_grp = lax.shift_right_logical(lane, int(math.log2(EXPERTS_PER_GROUP)))
    best = jnp.min(jnp.where(grp == best_val, lane_grp, big), axis=-1, keepdims=True)
    masked = jnp.where((lane_grp == best) & valid, sel, NEG_INF)
    v1 = jnp.max(masked, axis=-1, keepdims=True)
    i1 = jnp.min(jnp.where(masked == v1, lane, big), axis=-1, keepdims=True)
    masked2 = jnp.where(lane == i1, NEG_INF, masked)
    v2 = jnp.max(masked2, axis=-1, keepdims=True)
    i2 = jnp.min(jnp.where(masked2 == v2, lane, big), axis=-1, keepdims=True)
    w1 = jnp.sum(jnp.where(lane == i1, scores, 0.0), axis=-1, keepdims=True)
    w2 = jnp.sum(jnp.where(lane == i2, scores, 0.0), axis=-1, keepdims=True)
    tot = w1 + w2
    return jnp.where(lane == i1, w1 / tot, 0.0) + jnp.where(lane == i2, w2 / tot, 0.0), best


MOE_CHUNK = 128


def _dot_tn(a, b):
    return lax.dot_general(a, b, (((0,), (0,)), ((), ())), preferred_element_type=F32)


def _moe_kernel(x_ref, g_ref, sc_ref, sh_ref, gate_ref, rwh_ref, rwl_ref, rb_ref, tri_ref, wgu_ref, wd_ref, o_ref,
                h_ref, c2_ref, grp_ref, rank_ref, tot_ref):
    grp_id = pl.program_id(1)
    tm = x_ref.shape[0]
    lane = lax.broadcasted_iota(I32, (tm, LANES), 1)

    @pl.when(grp_id == 0)
    def _():
        h = _rms(x_ref[...], g_ref[...]) * (1.0 + sc_ref[0]) + sh_ref[0]
        hh, hl = _split(h)
        h_ref[...] = hh
        comb, best = _route(_sigmoid(_dot3(hh, hl, rwh_ref[...], rwl_ref[...])), rb_ref[...])
        c_hi, c_lo = _split(comb)
        c2_ref[:, :LANES] = c_hi
        c2_ref[:, LANES:] = c_lo
        grp_ref[...] = jnp.broadcast_to(best, (tm, LANES))
        onehot = jnp.where(lane == best, 1.0, 0.0)
        offs = jnp.zeros((1, LANES), F32)
        for b in range(tm // MOE_CHUNK):
            blk = onehot[b * MOE_CHUNK:(b + 1) * MOE_CHUNK, :]
            ranks = _dot(tri_ref[...], blk.astype(BF16)) + offs
            own = jnp.sum(blk * ranks, axis=-1, keepdims=True)
            rank_ref[b * MOE_CHUNK:(b + 1) * MOE_CHUNK, :] = jnp.broadcast_to(own, (MOE_CHUNK, LANES))
            offs = offs + jnp.sum(blk, axis=0, keepdims=True)
        tot_ref[0:1, :] = offs
        o_ref[...] = jnp.zeros(o_ref.shape, F32)

    lane1 = lax.broadcasted_iota(I32, (1, LANES), 1)
    n_tok = jnp.sum(jnp.where(lane1 == grp_id, tot_ref[0:1, :], 0.0)).astype(I32)
    n_chunks = lax.div(n_tok + (MOE_CHUNK - 1), jnp.int32(MOE_CHUNK))
    lane_f = lane.astype(F32)
    lane_c = lax.broadcasted_iota(I32, (MOE_CHUNK, LANES), 1)

    def chunk_body(c, carry):
        base = (c * MOE_CHUNK).astype(F32)
        pick = ((rank_ref[...] - base) == lane_f) & (grp_ref[...] == grp_id)
        st = jnp.where(pick, 1.0, 0.0).astype(BF16)
        xc = _dot_tn(st, h_ref[...]).astype(BF16)
        wc2 = _dot_tn(st, c2_ref[...])
        wc = wc2[:, :LANES] + wc2[:, LANES:]
        y = jnp.zeros((MOE_CHUNK, x_ref.shape[1]), F32)
        for el in range(EXPERTS_PER_GROUP):
            gu = _dot(xc, wgu_ref[el])
            hid = _silu(gu[:, :D_FF_EXPERT]) * gu[:, D_FF_EXPERT:]
            w_e = jnp.sum(jnp.where(lane_c == grp_id * EXPERTS_PER_GROUP + el, wc, 0.0), axis=-1, keepdims=True)
            y = y + w_e * _dot(hid.astype(BF16), wd_ref[el])
        y_hi, y_lo = _split(y)
        o_ref[...] += _dot(jnp.concatenate([st, st], axis=1), jnp.concatenate([y_hi, y_lo], axis=0))
        return carry
    lax.fori_loop(0, n_chunks, chunk_body, 0)

    @pl.when(grp_id == pl.num_programs(1) - 1)
    def _():
        o_ref[...] = x_ref[...] + gate_ref[0] * o_ref[...]


def _moe(x2, seq, g, scale, shift, gate, rw_hi, rw_lo, rb, wgu, wd):
    ntok, d = x2.shape
    bsz = scale.shape[0]
    tm = min(1024, seq)
    row = lambda i, e: (i, 0)
    per_b = lambda i, e: ((i * tm) // seq, 0, 0)
    vec = pl.BlockSpec((1, 1, d), per_b)
    const2 = lambda a: pl.BlockSpec(a.shape, lambda i, e: (0, 0))
    tri = jnp.asarray(np.tril(np.ones((MOE_CHUNK, MOE_CHUNK), np.float32), -1), BF16)
    return pl.pallas_call(
        _moe_kernel,
        grid=(ntok // tm, N_EXPERT_GROUPS),
        in_specs=[pl.BlockSpec((tm, d), row), pl.BlockSpec((1, d), lambda i, e: (0, 0)), vec, vec, vec,
                  const2(rw_hi), const2(rw_lo), const2(rb), const2(tri),
                  pl.BlockSpec((EXPERTS_PER_GROUP, d, 2 * D_FF_EXPERT), lambda i, e: (e, 0, 0)),
                  pl.BlockSpec((EXPERTS_PER_GROUP, D_FF_EXPERT, d), lambda i, e: (e, 0, 0))],
        out_specs=pl.BlockSpec((tm, d), row),
        out_shape=jax.ShapeDtypeStruct(x2.shape, F32),
        scratch_shapes=[pltpu.VMEM((tm, d), BF16), pltpu.VMEM((tm, 2 * LANES), BF16),
                        pltpu.VMEM((tm, LANES), I32), pltpu.VMEM((tm, LANES), F32), pltpu.VMEM((SUBLANES, LANES), F32)],
        compiler_params=_params("arbitrary", "arbitrary"),
        name="moe",
    )(x2, g.reshape(1, d), scale.reshape(bsz, 1, d), shift.reshape(bsz, 1, d), gate.reshape(bsz, 1, d),
      rw_hi, rw_lo, rb, tri, wgu, wd)


def _pad_lanes(v, width=LANES):
    v = v.reshape(1, -1)
    return jnp.pad(v, ((0, 0), (0, width - v.shape[1])))


def _layer_weights(w_in, sc_conv_w, g_v_norm_g, g_spatial_w, g_spatial_b, m_conv_w, m_conv_b, m_dt_bias,
                   m_a_log, m_d, m_norm_g, w_branch, w_out):
    splits = (A_Q, A_KV, A_KV, IDX_Q, IDX_DIM, IDX_HEADS, SC_W, SC_W, SC_W, 2 * GM_W, M_DINNER, M_XBC, M_HEADS,
              N_BRANCH * D_MODEL)
    pts = [int(p) for p in np.cumsum(splits)[:-1]]
    (q, k, v, qi, ki, wi, sc_b, sc_c, sc_x, g_uv, m_z, m_xbc, m_dt, gates) = jnp.split(w_in, pts, axis=-1)
    d = w_in.shape[0]
    w_main = jnp.concatenate([gates, g_uv, q, sc_b, sc_c, sc_x, m_z, m_xbc, k, v], axis=-1).astype(BF16)
    w_small = jnp.concatenate(
        [qi, ki, wi, jnp.zeros((d, LANES - IDX_DIM - IDX_HEADS), F32), m_dt, jnp.zeros((d, LANES - M_HEADS), F32)],
        axis=-1)
    ws_hi = w_small.astype(BF16)
    ws_lo = (w_small - ws_hi.astype(F32)).astype(BF16)
    tri = np.tril(np.ones((GM_CHUNK, GM_CHUNK), dtype=bool))
    return dict(
        w_main=w_main, ws_hi=ws_hi, ws_lo=ws_lo,
        sc_conv_w=sc_conv_w, g_v_norm_g=g_v_norm_g.reshape(1, GM_W),
        ws=jnp.where(tri[None], g_spatial_w, 0.0).astype(BF16), ws_b=g_spatial_b.T,
        m_conv_w=m_conv_w, m_conv_b=m_conv_b.reshape(1, M_XBC),
        m_dt_bias=_pad_lanes(m_dt_bias), m_a_log=_pad_lanes(m_a_log), m_d=_pad_lanes(m_d),
        m_norm_g=m_norm_g.reshape(1, M_DINNER), w_branch=w_branch.astype(BF16), w_out=w_out.astype(BF16))


def kernel(x, c, positions, ada_w, ada_b, norm1_g, w_in, q_norm_g, k_norm_g, sc_conv_w, g_v_norm_g, g_spatial_w,
           g_spatial_b, m_conv_w, m_conv_b, m_dt_bias, m_a_log, m_d, m_norm_g, w_branch, w_out, norm2_g, router_w,
           router_bias, exp_w_gate, exp_w_up, exp_w_down):
    bsz, seq, d = x.shape
    depth = ada_w.shape[0]
    mod = _modulation(c, ada_w, ada_b)
    rw = jnp.pad(router_w, ((0, 0), (0, LANES - N_EXPERTS)))
    rw_hi = rw.astype(BF16)
    rw_lo = (rw - rw_hi.astype(F32)).astype(BF16)
    rb = _pad_lanes(router_bias)
    pos2 = positions.reshape(bsz * seq, 1)
    x2 = x.reshape(bsz * seq, d)
    for l in range(depth):
        shift1, scale1, gate1, shift2, scale2, gate2 = jnp.split(mod[l], 6, axis=-1)
        lw = _layer_weights(w_in[l], sc_conv_w[l], g_v_norm_g[l], g_spatial_w[l], g_spatial_b[l], m_conv_w[l],
                            m_conv_b[l], m_dt_bias[l], m_a_log[l], m_d[l], m_norm_g[l], w_branch[l], w_out[l])
        proj, small = _inproj(x2, seq, norm1_g[l], scale1, shift1, lw["w_main"], lw["ws_hi"], lw["ws_lo"])
        q_pad, k_pad, v_t, qi_cat, ki_cat = _prep(proj, small, pos2, seq, q_norm_g[l], k_norm_g[l])
        o_a = _dsa(qi_cat, small, q_pad, ki_cat, k_pad, v_t, bsz, seq)
        x2 = _mix(x2, proj, small, o_a, gate1, lw, bsz, seq)
        wgu = jnp.concatenate([exp_w_gate[l], exp_w_up[l]], axis=-1).astype(BF16)
        x2 = _moe(x2, seq, norm2_g[l], scale2, shift2, gate2, rw_hi, rw_lo, rb, wgu, exp_w_down[l].astype(BF16))
    return x2.reshape(bsz, seq, d)
```

```python
import functools
import math

import jax
import jax.numpy as jnp
import numpy as np
from jax import lax
from jax.experimental import pallas as pl
from jax.experimental.pallas import tpu as pltpu

F32 = jnp.float32
BF16 = jnp.bfloat16
I32 = jnp.int32

D_MODEL = 1024
RMS_EPS = 1e-6
ROPE_THETA = 500000.0
Q_BLOCK = 128
A_HEADS = 8
A_KV_HEADS = 2
A_HEAD_DIM = 64
A_ROT = A_HEAD_DIM // 4
A_REP = A_HEADS // A_KV_HEADS
IDX_HEADS = 8
IDX_DIM = 32
IDX_ROT = IDX_DIM // 4
IDX_TOPK_MAX = 256
SC_W = 512
SC_KERNEL = 3
GM_W = 512
GM_GROUPS = 4
GM_GROUP_W = GM_W // GM_GROUPS
GM_CHUNK = 128
M_DINNER = 512
M_HEAD_DIM = 64
M_HEADS = M_DINNER // M_HEAD_DIM
M_GROUPS = 2
M_STATE = 64
M_CONV = 4
M_CHUNK = 128
M_BC = M_GROUPS * M_STATE
M_XBC = M_DINNER + 2 * M_BC
N_BRANCH = 4
BRANCH_W = 512
N_EXPERTS = 16
N_EXPERT_GROUPS = 4
EXPERTS_PER_GROUP = N_EXPERTS // N_EXPERT_GROUPS
D_FF_EXPERT = 512
A_Q = A_HEADS * A_HEAD_DIM
A_KV = A_KV_HEADS * A_HEAD_DIM
IDX_Q = IDX_HEADS * IDX_DIM

LANES = 128
SUBLANES = 8
VMEM_LIMIT = 56 * 1024 * 1024

P_GATES = 0
P_GUV = 4096
P_Q = 5120
P_SCB = 5632
P_SCC = 6144
P_SCX = 6656
P_MZ = 7168
P_XBC = 7680
P_K = 8448
P_V = 8576
P_COLS = 8704
S_QI = 0
S_KW = 256
S_DT = 384
S_COLS = 512

INT_MIN = -2147483648
NEG_INF = float("-inf")
NEG_INF_KEY = -2139095041


def _dot(a, b):
    return jnp.dot(a, b, preferred_element_type=F32)


def _dot_nt(a, b):
    return lax.dot_general(a, b, (((1,), (1,)), ((), ())), preferred_element_type=F32)


def _dot_f32(a, b):
    return jnp.dot(a, b, preferred_element_type=F32, precision=lax.Precision.HIGHEST)


def _split(x):
    hi = x.astype(BF16)
    lo = (x - hi.astype(F32)).astype(BF16)
    return hi, lo


def _dot3(a_hi, a_lo, b_hi, b_lo):
    return _dot(a_hi, b_hi) + _dot(a_lo, b_hi) + _dot(a_hi, b_lo)


def _sigmoid(x):
    return 1.0 / (1.0 + jnp.exp(-x))


def _silu(x):
    return x * _sigmoid(x)


def _softplus(x):
    return jnp.maximum(x, 0.0) + jnp.log1p(jnp.exp(-jnp.abs(x)))


def _gelu(x):
    return 0.5 * x * (1.0 + lax.erf(x * np.float32(np.sqrt(0.5))))


def _rms(x, g_row):
    ms = jnp.mean(x * x, axis=-1, keepdims=True)
    return x * lax.rsqrt(ms + RMS_EPS) * g_row


def _params(*sem):
    return pltpu.CompilerParams(dimension_semantics=sem, vmem_limit_bytes=VMEM_LIMIT)


def _mod_kernel(c_ref, w_ref, b_ref, o_ref):
    o_ref[0] = _dot_f32(_silu(c_ref[...]), w_ref[0]) + b_ref[0]


def _modulation(c, ada_w, ada_b):
    depth, d, n = ada_w.shape
    bsz = c.shape[0]
    tn = 768
    return pl.pallas_call(
        _mod_kernel,
        grid=(depth, n // tn),
        in_specs=[pl.BlockSpec((bsz, d), lambda l, j: (0, 0)),
                  pl.BlockSpec((1, d, tn), lambda l, j: (l, 0, j)),
                  pl.BlockSpec((1, 1, tn), lambda l, j: (l, 0, j))],
        out_specs=pl.BlockSpec((1, bsz, tn), lambda l, j: (l, 0, j)),
        out_shape=jax.ShapeDtypeStruct((depth, bsz, n), F32),
        compiler_params=_params("arbitrary", "arbitrary"),
        name="modulation",
    )(c, ada_w, ada_b.reshape(depth, 1, n))


def _inproj_kernel(x_ref, g_ref, sc_ref, sh_ref, w_ref, wsh_ref, wsl_ref, o_ref, os_ref, hh_ref, hl_ref):
    @pl.when(pl.program_id(1) == 0)
    def _():
        h = _rms(x_ref[...], g_ref[...]) * (1.0 + sc_ref[0]) + sh_ref[0]
        hh, hl = _split(h)
        hh_ref[...] = hh
        hl_ref[...] = hl
        os_ref[...] = _dot3(hh, hl, wsh_ref[...], wsl_ref[...])

    o_ref[...] = _dot(hh_ref[...], w_ref[...]).astype(BF16)


def _inproj(x2, seq, g, scale, shift, w_main, ws_hi, ws_lo):
    ntok, d = x2.shape
    tm = min(1024, seq)
    tn = P_COLS // 4
    bsz = scale.shape[0]
    row = lambda i, j: (i, 0)
    per_b = lambda i, j: ((i * tm) // seq, 0, 0)
    return pl.pallas_call(
        _inproj_kernel,
        grid=(ntok // tm, P_COLS // tn),
        in_specs=[pl.BlockSpec((tm, d), row),
                  pl.BlockSpec((1, d), lambda i, j: (0, 0)),
                  pl.BlockSpec((1, 1, d), per_b),
                  pl.BlockSpec((1, 1, d), per_b),
                  pl.BlockSpec((d, tn), lambda i, j: (0, j)),
                  pl.BlockSpec((d, S_COLS), lambda i, j: (0, 0)),
                  pl.BlockSpec((d, S_COLS), lambda i, j: (0, 0))],
        out_specs=[pl.BlockSpec((tm, tn), lambda i, j: (i, j)),
                   pl.BlockSpec((tm, S_COLS), row)],
        out_shape=[jax.ShapeDtypeStruct((ntok, P_COLS), BF16),
                   jax.ShapeDtypeStruct((ntok, S_COLS), F32)],
        scratch_shapes=[pltpu.VMEM((tm, d), BF16), pltpu.VMEM((tm, d), BF16)],
        compiler_params=_params("arbitrary", "arbitrary"),
        name="inproj",
    )(x2, g.reshape(1, d), scale.reshape(bsz, 1, d), shift.reshape(bsz, 1, d), w_main, ws_hi, ws_lo)


ROPE_ZERO_LANE = A_ROT // 2 + IDX_ROT // 2


def _rope_angles(pos_f):
    half_a, half_i = A_ROT // 2, IDX_ROT // 2
    lane = lax.broadcasted_iota(I32, (1, LANES), 1)
    k = jnp.where(lane < half_a, lane, lane - half_a).astype(F32)
    rot = jnp.where(lane < half_a, np.float32(A_ROT), np.float32(IDX_ROT))
    inv_freq = jnp.exp(np.float32(-math.log(ROPE_THETA)) * k * 2.0 / rot)
    ang = pos_f * jnp.where(lane < half_a + half_i, inv_freq, 0.0)
    return jnp.cos(ang), jnp.sin(ang)


def _rope_tables(cos_sin, place_ref, period, rot, limit=LANES):
    half = rot // 2

    def expand(x):
        hi, lo = _split(x)
        return _dot(hi, place_ref[...]) + _dot(lo, place_ref[...])

    lane_all = lax.broadcasted_iota(I32, (1, LANES), 1)
    lane = lane_all % period
    cos = jnp.where(lane_all < limit, expand(cos_sin[0]), 1.0)
    sin = jnp.where(lane_all < limit, expand(cos_sin[1]), 0.0)
    sin_lo = jnp.where(lane < half, -sin, 0.0)
    sin_hi = jnp.where((lane >= half) & (lane < rot), sin, 0.0)
    return cos, sin_lo, sin_hi


def _rope(x, tables, half):
    cos, sin_lo, sin_hi = tables
    width = x.shape[-1]
    return (x * cos + pltpu.roll(x, width - half, 1) * sin_lo + pltpu.roll(x, half, 1) * sin_hi)


def _prep_kernel(q_ref, k_ref, v_ref, qi_ref, kw_ref, pos_ref, qg_ref, kg_ref, bdq_ref, bdk_ref,
                 pq_ref, pk_ref, pqi1_ref, pqi2_ref, pki1_ref, pki2_ref, pra_ref, pri_ref,
                 qo_ref, ko_ref, vto_ref, qio_ref, kio_ref):
    cos_sin = _rope_angles(pos_ref[...].astype(F32))
    att_tab = _rope_tables(cos_sin, pra_ref, A_HEAD_DIM, A_ROT)
    idx_tab = _rope_tables(cos_sin, pri_ref, IDX_DIM, IDX_ROT)
    kw_tab = _rope_tables(cos_sin, pri_ref, IDX_DIM, IDX_ROT, limit=IDX_DIM)

    def head_norm(x, bd_ref, g_ref):
        sq_hi, sq_lo = _split(x * x)
        ms = (_dot(sq_hi, bd_ref[...]) + _dot(sq_lo, bd_ref[...])) * np.float32(1.0 / A_HEAD_DIM)
        return x * lax.rsqrt(ms + RMS_EPS) * g_ref[...]

    def rope_wide(x, tab, half):
        pieces = [_rope(x[:, c * LANES:(c + 1) * LANES], tab, half) for c in range(x.shape[-1] // LANES)]
        return pieces[0] if len(pieces) == 1 else jnp.concatenate(pieces, axis=-1)

    def store_head_rows(o_ref, x):
        for blk in range(x.shape[0] // Q_BLOCK):
            for h in range(x.shape[1] // LANES):
                r0 = (blk * (x.shape[1] // LANES) + h) * Q_BLOCK
                o_ref[r0:r0 + Q_BLOCK, :] = x[blk * Q_BLOCK:(blk + 1) * Q_BLOCK, h * LANES:(h + 1) * LANES]

    q = rope_wide(head_norm(q_ref[...].astype(F32), bdq_ref, qg_ref), att_tab, A_ROT // 2)
    q = (q * np.float32(A_HEAD_DIM ** -0.5)).astype(BF16)
    store_head_rows(qo_ref, _dot(q, pq_ref[...]).astype(BF16))
    k = rope_wide(head_norm(k_ref[...].astype(F32), bdk_ref, kg_ref), att_tab, A_ROT // 2).astype(BF16)
    ko_ref[...] = _dot(k, pk_ref[...]).astype(BF16)
    v_pad = _dot(v_ref[...], pk_ref[...])
    ones_lane = (lax.broadcasted_iota(I32, (1, A_KV_HEADS * LANES), 1) % LANES) == A_HEAD_DIM
    vto_ref[...] = jnp.transpose(jnp.where(ones_lane, 1.0, v_pad)).astype(BF16)

    qi_hi, qi_lo = _split(rope_wide(qi_ref[...], idx_tab, IDX_ROT // 2))
    store_head_rows(qio_ref, (_dot(qi_hi, pqi1_ref[...]) + _dot(qi_lo, pqi2_ref[...])).astype(BF16))
    ki_hi, ki_lo = _split(_rope(kw_ref[...], kw_tab, IDX_ROT // 2))
    kio_ref[...] = (_dot(ki_hi, pki1_ref[...]) + _dot(ki_lo, pki2_ref[...])).astype(BF16)


@functools.lru_cache(maxsize=None)
def _prep_constants():
    def block_diag(width):
        i = np.arange(width)
        return (i[:, None] // A_HEAD_DIM == i[None, :] // A_HEAD_DIM).astype(np.float32)

    def pad_heads(n_heads):
        p = np.zeros((n_heads * A_HEAD_DIM, n_heads * LANES), np.float32)
        for h in range(n_heads):
            for d in range(A_HEAD_DIM):
                p[h * A_HEAD_DIM + d, h * LANES + d] = 1.0
        return p

    pqi1 = np.zeros((IDX_Q, IDX_HEADS * LANES), np.float32)
    pqi2 = np.zeros((IDX_Q, IDX_HEADS * LANES), np.float32)
    for h in range(IDX_HEADS):
        for d in range(IDX_DIM):
            pqi1[h * IDX_DIM + d, h * LANES + d] = 1.0
            pqi1[h * IDX_DIM + d, h * LANES + IDX_DIM + d] = 1.0
            pqi2[h * IDX_DIM + d, h * LANES + 2 * IDX_DIM + d] = 1.0
    pki1 = np.zeros((LANES, LANES), np.float32)
    pki2 = np.zeros((LANES, LANES), np.float32)
    for d in range(IDX_DIM):
        pki1[d, d] = 1.0
        pki1[d, 2 * IDX_DIM + d] = 1.0
        pki2[d, IDX_DIM + d] = 1.0
    def rope_place(period, rot, first_src):
        p = np.zeros((LANES, LANES), np.float32)
        for j in range(LANES):
            jj = j % period
            p[first_src + jj % (rot // 2) if jj < rot else ROPE_ZERO_LANE, j] = 1.0
        return p

    mats = (block_diag(A_Q), block_diag(A_KV), pad_heads(A_HEADS), pad_heads(A_KV_HEADS), pqi1, pqi2, pki1, pki2,
            rope_place(A_HEAD_DIM, A_ROT, 0), rope_place(IDX_DIM, IDX_ROT, A_ROT // 2))
    return tuple(np.asarray(m) for m in mats)


def _prep(proj, small, pos2, seq, q_g, k_g):
    ntok = proj.shape[0]
    tm = min(512, seq)
    consts = [jnp.asarray(m, BF16) for m in _prep_constants()]
    row = lambda w, c: pl.BlockSpec((tm, w), lambda i: (i, c))
    full = lambda a: pl.BlockSpec(a.shape, lambda i: (0, 0))
    qg = jnp.tile(q_g, A_HEADS).reshape(1, A_Q)
    kg = jnp.tile(k_g, A_KV_HEADS).reshape(1, A_KV)
    return pl.pallas_call(
        _prep_kernel,
        grid=(ntok // tm,),
        in_specs=[row(A_Q, P_Q // A_Q), row(A_KV, P_K // A_KV), row(A_KV, P_V // A_KV), row(IDX_Q, S_QI // IDX_Q),
                  row(LANES, S_KW // LANES), row(1, 0), full(qg), full(kg)] + [full(m) for m in consts],
        out_specs=[pl.BlockSpec((tm * A_HEADS, LANES), lambda i: (i, 0)), row(A_KV_HEADS * LANES, 0),
                   pl.BlockSpec((A_KV_HEADS * LANES, tm), lambda i: (0, i)),
                   pl.BlockSpec((tm * IDX_HEADS, LANES), lambda i: (i, 0)), row(LANES, 0)],
        out_shape=[jax.ShapeDtypeStruct((ntok * A_HEADS, LANES), BF16),
                   jax.ShapeDtypeStruct((ntok, A_KV_HEADS * LANES), BF16),
                   jax.ShapeDtypeStruct((A_KV_HEADS * LANES, ntok), BF16),
                   jax.ShapeDtypeStruct((ntok * IDX_HEADS, LANES), BF16),
                   jax.ShapeDtypeStruct((ntok, LANES), BF16)],
        compiler_params=_params("arbitrary"),
        name="attn_prep",
    )(proj, proj, proj, small, small, pos2, qg, kg, *consts)


def _sortable(x):
    bits = pltpu.bitcast(jnp.where(x == 0.0, 0.0, x), I32)
    return jnp.where(bits < 0, bits ^ jnp.int32(0x7FFFFFFF), bits)


KEY_STEP = 512
KEY_SUBS = KEY_STEP // Q_BLOCK


def _key_row0(step, sub):
    return pl.multiple_of(step * KEY_STEP + sub * Q_BLOCK, Q_BLOCK)


MXU_LOOP_WIDTHS = (4, 2, 1)


def _loop_steps(nstep, body, init, widths=(2, 1)):
    carry, start = init, jnp.int32(0)
    for w in widths:
        count = lax.div(nstep - start, jnp.int32(w))

        def multi(i, c, w=w, start=start):
            for k in range(w):
                c = body(start + i * w + k, c)
            return c
        carry = lax.fori_loop(0, count, multi, carry)
        start = start + count * w
    return carry


def _count_rows(pred_fn, nstep):
    def body(c, acc):
        for sub in range(KEY_SUBS):
            ones = jnp.where(pred_fn(_key_row0(c, sub)), jnp.int32(1), jnp.int32(0))
            acc = acc + jnp.sum(ones.reshape(Q_BLOCK // SUBLANES, SUBLANES, LANES), axis=0)
        return acc
    acc = _loop_steps(nstep, body, jnp.zeros((SUBLANES, LANES), I32))
    return jnp.sum(acc, axis=0, keepdims=True)


PAIR = 2 * LANES
N_PAIRS = A_HEADS // 2


def _dsa_kernel(qi_ref, kw_ref, q_ref, ki_ref, k_ref, vt_ref, o_ref, key_ref, s_ref, lim_ref, oacc_ref,
                *, topk, idx_bits):
    qb = pl.program_id(1)
    nstep = lax.div(qb, jnp.int32(KEY_SUBS)) + 1
    row_i = lax.broadcasted_iota(I32, (Q_BLOCK, LANES), 0)
    q_pos = qb * Q_BLOCK + lax.broadcasted_iota(I32, (Q_BLOCK, LANES), 1)

    w_rows = jnp.transpose(kw_ref[...])[IDX_DIM:IDX_DIM + IDX_HEADS, :]
    w_rows = w_rows * np.float32(IDX_HEADS ** -0.5) * np.float32(IDX_DIM ** -0.5)

    def rows(r0):
        return pl.ds(r0, Q_BLOCK)

    def score_body(c, carry):
        for sub in range(KEY_SUBS):
            r0 = _key_row0(c, sub)
            ki = ki_ref[rows(r0), :]
            s = jnp.zeros((Q_BLOCK, LANES), F32)
            for pair in range(IDX_HEADS // 2):
                d = _dot_nt(ki, qi_ref[pair * PAIR:(pair + 1) * PAIR, :])
                for j in range(2):
                    h = 2 * pair + j
                    s = s + jnp.maximum(d[:, j * LANES:(j + 1) * LANES], 0.0) * w_rows[h:h + 1, :]
            key_ref[rows(r0), :] = _sortable(jnp.where((r0 + row_i) <= q_pos, s, NEG_INF))
        return carry
    _loop_steps(nstep, score_body, 0, MXU_LOOP_WIDTHS)

    def bit_body(i, cand):
        trial = cand | lax.shift_left(jnp.int32(1), 31 - i)
        trial_s = trial ^ jnp.int32(INT_MIN)
        cnt = _count_rows(lambda r0: key_ref[rows(r0), :] >= trial_s, nstep)
        return jnp.where(cnt >= topk, trial, cand)
    cand = lax.fori_loop(0, 32, bit_body, jnp.zeros((1, LANES), I32))
    thr = cand ^ jnp.int32(INT_MIN)

    n_gt = _count_rows(lambda r0: key_ref[rows(r0), :] > thr, nstep)
    n_eq = _count_rows(lambda r0: key_ref[rows(r0), :] == thr, nstep)
    need = topk - n_gt
    excess = (n_eq > need) & (thr > jnp.int32(NEG_INF_KEY))
    lim_ref[0:1, :] = jnp.full((1, LANES), 2 ** 30, I32)

    @pl.when(jnp.max(jnp.where(excess, 1, 0)) > 0)
    def _():
        def idx_body(i, lim):
            trial = lim | lax.shift_left(jnp.int32(1), idx_bits - 1 - i)
            cnt = _count_rows(lambda r0: (key_ref[rows(r0), :] == thr) & ((r0 + row_i) < trial), nstep)
            return jnp.where(cnt < need, trial, lim)
        lim = lax.fori_loop(0, idx_bits, idx_body, jnp.zeros((1, LANES), I32))
        lim_ref[0:1, :] = jnp.where(excess, lim, 2 ** 30)
    tie_lim = lim_ref[0:1, :]

    def max_body(c, m_all):
        m_all = list(m_all)
        for sub in range(KEY_SUBS):
            r0 = _key_row0(c, sub)
            keys = key_ref[rows(r0), :]
            kpos = r0 + row_i
            sel = ((keys > thr) | ((keys == thr) & (kpos <= tie_lim))) & (kpos <= q_pos)
            bias = jnp.where(sel, 0.0, NEG_INF)
            bias2 = jnp.concatenate([bias, bias], axis=-1)
            for pair in range(N_PAIRS):
                g = (2 * pair) // A_REP
                s = _dot_nt(k_ref[rows(r0), g * LANES:(g + 1) * LANES], q_ref[pair * PAIR:(pair + 1) * PAIR, :]) + bias2
                s_ref[rows(r0), pair * PAIR:(pair + 1) * PAIR] = s
                m_all[pair] = jnp.maximum(m_all[pair], jnp.max(s, axis=0, keepdims=True))
        return tuple(m_all)
    m_fin = _loop_steps(nstep, max_body, tuple(jnp.full((1, PAIR), -1e30, F32) for _ in range(N_PAIRS)),
                        MXU_LOOP_WIDTHS)

    oacc_ref[...] = jnp.zeros(oacc_ref.shape, F32)

    def pv_body(c, carry):
        cols = pl.ds(pl.multiple_of(c * KEY_STEP, KEY_STEP), KEY_STEP)
        for pair in range(N_PAIRS):
            g = (2 * pair) // A_REP
            p = jnp.exp(s_ref[cols, pair * PAIR:(pair + 1) * PAIR] - m_fin[pair]).astype(BF16)
            oacc_ref[pair] += _dot(vt_ref[g * LANES:(g + 1) * LANES, cols], p)
        return carry
    _loop_steps(nstep, pv_body, 0, MXU_LOOP_WIDTHS)

    outs = []
    for pair in range(N_PAIRS):
        o_p = oacc_ref[pair]
        for j in range(2):
            o_h = o_p[:, j * LANES:(j + 1) * LANES]
            outs.append(o_h[0:A_HEAD_DIM, :] / o_h[A_HEAD_DIM:A_HEAD_DIM + 1, :])
    o_ref[...] = jnp.transpose(jnp.concatenate(outs, axis=0)).astype(BF16)


def _dsa(qi_cat, small, q_pad, ki_cat, k_pad, v_t, bsz, seq):
    assert seq % KEY_STEP == 0
    nb = seq // Q_BLOCK
    topk = min(IDX_TOPK_MAX, seq // 4)
    idx_bits = max(1, int(math.ceil(math.log2(seq))))
    qrow = lambda w, c: pl.BlockSpec((Q_BLOCK, w), lambda b, j: (b * nb + j, c))
    brow = lambda w: pl.BlockSpec((seq, w), lambda b, j: (b, 0))
    return pl.pallas_call(
        functools.partial(_dsa_kernel, topk=topk, idx_bits=idx_bits),
        grid=(bsz, nb),
        in_specs=[pl.BlockSpec((IDX_HEADS * Q_BLOCK, LANES), lambda b, j: (b * nb + j, 0)),
                  qrow(LANES, S_KW // LANES),
                  pl.BlockSpec((A_HEADS * Q_BLOCK, LANES), lambda b, j: (b * nb + j, 0)),
                  brow(LANES), brow(A_KV_HEADS * LANES),
                  pl.BlockSpec((A_KV_HEADS * LANES, seq), lambda b, j: (0, b))],
        out_specs=qrow(A_Q, 0),
        out_shape=jax.ShapeDtypeStruct((bsz * seq, A_Q), BF16),
        scratch_shapes=[pltpu.VMEM((seq, LANES), I32), pltpu.VMEM((seq, A_HEADS * LANES), F32),
                        pltpu.VMEM((SUBLANES, LANES), I32), pltpu.VMEM((N_PAIRS, LANES, PAIR), F32)],
        compiler_params=_params("arbitrary", "arbitrary"),
        name="sparse_attention",
    )(qi_cat, small, q_pad, ki_cat, k_pad, v_t)


HALO = SUBLANES


def _causal_conv(ext_ref, u, w_ref, taps, rows):
    ext_ref[HALO:HALO + rows, :] = u
    y = u * w_ref[taps - 1:taps, :]
    for j in range(taps - 1):
        shift = taps - 1 - j
        y = y + ext_ref[HALO - shift:HALO - shift + rows, :] * w_ref[j:j + 1, :]
    ext_ref[0:HALO, :] = ext_ref[rows:rows + HALO, :]
    return y


def _mix_kernel(x_ref, g0_ref, g1_ref, g2_ref, g3_ref, guv_ref, scb_ref, scc_ref, scx_ref, mz_ref, xbc_ref,
                dt_ref, oa_ref, gate_ref, scw_ref, gvg_ref, ws_ref, wsb_ref, mcw_ref, mcb_ref, dtb_ref, alog_ref,
                md_ref, mng_ref, wb_ref, wo_ref, o_ref, sc_ext, xbc_ext, state_ref, y_ref, *, rows):
    @pl.when(pl.program_id(1) == 0)
    def _():
        sc_ext[0:HALO, :] = jnp.zeros((HALO, SC_W), F32)
        xbc_ext[0:HALO, :] = jnp.zeros((HALO, M_XBC), F32)
        state_ref[...] = jnp.zeros(state_ref.shape, F32)

    nchunk = rows // M_CHUNK
    row_i = lax.broadcasted_iota(I32, (M_CHUNK, M_CHUNK), 0)
    col_i = lax.broadcasted_iota(I32, (M_CHUNK, M_CHUNK), 1)
    tri = row_i >= col_i

    o_b = scb_ref[...].astype(F32) * _causal_conv(
        sc_ext, scc_ref[...].astype(F32) * scx_ref[...].astype(F32), scw_ref, SC_KERNEL, rows)

    guv = _gelu(guv_ref[...].astype(F32))
    gu = guv[:, :GM_W]
    gv = _rms(guv[:, GM_W:], gvg_ref[...]).astype(BF16)
    mixed_rows = []
    for ci in range(nchunk):
        r0 = ci * GM_CHUNK
        cols = []
        for gi in range(GM_GROUPS):
            vb = gv[r0:r0 + GM_CHUNK, gi * GM_GROUP_W:(gi + 1) * GM_GROUP_W]
            cols.append(_dot(ws_ref[gi], vb) + wsb_ref[:, gi:gi + 1])
        mixed_rows.append(jnp.concatenate(cols, axis=-1))
    o_c = gu * jnp.concatenate(mixed_rows, axis=0)

    xbc = _silu(_causal_conv(xbc_ext, xbc_ref[...].astype(F32), mcw_ref, M_CONV, rows) + mcb_ref[...])
    head_lane = lax.broadcasted_iota(I32, (1, LANES), 1) < M_HEADS
    dt = jnp.where(head_lane, _softplus(dt_ref[...] + dtb_ref[...]), 0.0)
    a_row = jnp.where(head_lane, -jnp.exp(alog_ref[...]), 0.0)
    tri_f = jnp.where(tri, 1.0, 0.0).astype(F32)
    for ci in range(nchunk):
        r0 = ci * M_CHUNK
        dtc = dt[r0:r0 + M_CHUNK, :]
        acs = _dot_f32(tri_f, dtc * a_row)
        acs_t = jnp.transpose(acs)
        b_t = jnp.transpose(xbc[r0:r0 + M_CHUNK, M_DINNER:M_DINNER + M_BC]).astype(BF16)
        cm = xbc[r0:r0 + M_CHUNK, M_DINNER + M_BC:M_XBC].astype(BF16)
        for g in range(M_GROUPS):
            bg_t = b_t[g * M_STATE:(g + 1) * M_STATE, :]
            cg = cm[:, g * M_STATE:(g + 1) * M_STATE]
            cb = _dot(cg, bg_t)
            for h in range(g * (M_HEADS // M_GROUPS), (g + 1) * (M_HEADS // M_GROUPS)):
                col = acs[:, h:h + 1]
                row = acs_t[h:h + 1, :]
                last = acs[M_CHUNK - 1:M_CHUNK, h:h + 1]
                decay = jnp.exp(jnp.where(tri, col - row, NEG_INF))
                xs_h = xbc[r0:r0 + M_CHUNK, h * M_HEAD_DIM:(h + 1) * M_HEAD_DIM]
                xdt = xs_h * dtc[:, h:h + 1]
                y = _dot((cb * decay).astype(BF16), xdt.astype(BF16))
                prev_t = state_ref[h]
                y = y + _dot(cg, prev_t.astype(BF16)) * jnp.exp(col)
                st_t = _dot(bg_t, (xdt * jnp.exp(last - col)).astype(BF16))
                state_ref[h] = prev_t * jnp.exp(last) + st_t
                y_ref[r0:r0 + M_CHUNK, h * M_HEAD_DIM:(h + 1) * M_HEAD_DIM] = y + md_ref[:, h:h + 1] * xs_h
    o_d = _rms(y_ref[...] * _silu(mz_ref[...].astype(F32)), mng_ref[...])

    branches = (oa_ref[...], o_b.astype(BF16), o_c.astype(BF16), o_d.astype(BF16))
    gate_refs = (g0_ref, g1_ref, g2_ref, g3_ref)
    merged = None
    for i in range(N_BRANCH):
        term = _sigmoid(gate_refs[i][...].astype(F32)) * _dot(branches[i], wb_ref[i])
        merged = term if merged is None else merged + term
    o_ref[...] = x_ref[...] + gate_ref[0] * _dot(merged.astype(BF16), wo_ref[...])


def _mix(x2, proj, small, o_a, gate1, lw, bsz, seq):
    d = x2.shape[1]
    rows = min(256, seq)
    ns = seq // rows
    tok = lambda w, c: pl.BlockSpec((rows, w), lambda b, s: (b * ns + s, c))
    full = lambda a: pl.BlockSpec(a.shape, lambda b, s: (0,) * a.ndim)
    weights = [lw["sc_conv_w"], lw["g_v_norm_g"], lw["ws"], lw["ws_b"], lw["m_conv_w"], lw["m_conv_b"],
               lw["m_dt_bias"], lw["m_a_log"], lw["m_d"], lw["m_norm_g"], lw["w_branch"], lw["w_out"]]
    return pl.pallas_call(
        functools.partial(_mix_kernel, rows=rows),
        grid=(bsz, ns),
        in_specs=[tok(d, 0)] + [tok(D_MODEL, i) for i in range(N_BRANCH)]
        + [tok(2 * GM_W, P_GUV // (2 * GM_W)), tok(SC_W, P_SCB // SC_W), tok(SC_W, P_SCC // SC_W),
           tok(SC_W, P_SCX // SC_W), tok(M_DINNER, P_MZ // M_DINNER), tok(M_XBC, P_XBC // M_XBC),
           tok(LANES, S_DT // LANES), tok(A_Q, 0),
           pl.BlockSpec((1, 1, d), lambda b, s: (b, 0, 0))]
        + [full(w) for w in weights],
        out_specs=tok(d, 0),
        out_shape=jax.ShapeDtypeStruct(x2.shape, F32),
        scratch_shapes=[pltpu.VMEM((HALO + rows, SC_W), F32), pltpu.VMEM((HALO + rows, M_XBC), F32),
                        pltpu.VMEM((M_HEADS, M_STATE, M_HEAD_DIM), F32), pltpu.VMEM((rows, M_DINNER), F32)],
        compiler_params=_params("arbitrary", "arbitrary"),
        name="mixers_merge",
    )(x2, proj, proj, proj, proj, proj, proj, proj, proj, proj, proj, small, o_a,
      gate1.reshape(bsz, 1, d), *weights)


def _swap_lanes(x, lane, dist):
    up = pltpu.roll(x, LANES - dist, 1)
    down = pltpu.roll(x, dist, 1)
    return jnp.where((lane & (2 * dist - 1)) < dist, up, down)


def _route(scores, bias_row):
    lane = lax.broadcasted_iota(I32, scores.shape, 1)
    valid = lane < N_EXPERTS
    sel = jnp.where(valid, scores + bias_row, NEG_INF)
    p1 = _swap_lanes(sel, lane, 1)
    hi1, lo1 = jnp.maximum(sel, p1), jnp.minimum(sel, p1)
    hi2, lo2 = _swap_lanes(hi1, lane, 2), _swap_lanes(lo1, lane, 2)
    grp = jnp.maximum(hi1, hi2) + jnp.maximum(jnp.minimum(hi1, hi2), jnp.maximum(lo1, lo2))
    grp = jnp.where(valid, grp, NEG_INF)
    best_val = jnp.max(grp, axis=-1, keepdims=True)
    big = jnp.int32(LANES)
    lane_grp = lax.shift_right_logical(lane, int(math.log2(EXPERTS_PER_GROUP)))
    best = jnp.min(jnp.where(grp == best_val, lane_grp, big), axis=-1, keepdims=True)
    masked = jnp.where((lane_grp == best) & valid, sel, NEG_INF)
    v1 = jnp.max(masked, axis=-1, keepdims=True)
    i1 = jnp.min(jnp.where(masked == v1, lane, big), axis=-1, keepdims=True)
    masked2 = jnp.where(lane == i1, NEG_INF, masked)
    v2 = jnp.max(masked2, axis=-1, keepdims=True)
    i2 = jnp.min(jnp.where(masked2 == v2, lane, big), axis=-1, keepdims=True)
    w1 = jnp.sum(jnp.where(lane == i1, scores, 0.0), axis=-1, keepdims=True)
    w2 = jnp.sum(jnp.where(lane == i2, scores, 0.0), axis=-1, keepdims=True)
    tot = w1 + w2
    return jnp.where(lane == i1, w1 / tot, 0.0) + jnp.where(lane == i2, w2 / tot, 0.0), best


MOE_CHUNK = 128


def _dot_tn(a, b):
    return lax.dot_general(a, b, (((0,), (0,)), ((), ())), preferred_element_type=F32)


def _moe_kernel(x_ref, g_ref, sc_ref, sh_ref, gate_ref, rwh_ref, rwl_ref, rb_ref, tri_ref, wgu_ref, wd_ref, o_ref,
                h_ref, c2_ref, grp_ref, rank_ref, tot_ref):
    grp_id = pl.program_id(1)
    tm = x_ref.shape[0]
    lane = lax.broadcasted_iota(I32, (tm, LANES), 1)

    @pl.when(grp_id == 0)
    def _():
        h = _rms(x_ref[...], g_ref[...]) * (1.0 + sc_ref[0]) + sh_ref[0]
        hh, hl = _split(h)
        h_ref[...] = hh
        comb, best = _route(_sigmoid(_dot3(hh, hl, rwh_ref[...], rwl_ref[...])), rb_ref[...])
        c_hi, c_lo = _split(comb)
        c2_ref[:, :LANES] = c_hi
        c2_ref[:, LANES:] = c_lo
        grp_ref[...] = jnp.broadcast_to(best, (tm, LANES))
        onehot = jnp.where(lane == best, 1.0, 0.0)
        offs = jnp.zeros((1, LANES), F32)
        for b in range(tm // MOE_CHUNK):
            blk = onehot[b * MOE_CHUNK:(b + 1) * MOE_CHUNK, :]
            ranks = _dot(tri_ref[...], blk.astype(BF16)) + offs
            own = jnp.sum(blk * ranks, axis=-1, keepdims=True)
            rank_ref[b * MOE_CHUNK:(b + 1) * MOE_CHUNK, :] = jnp.broadcast_to(own, (MOE_CHUNK, LANES))
            offs = offs + jnp.sum(blk, axis=0, keepdims=True)
        tot_ref[0:1, :] = offs
        o_ref[...] = jnp.zeros(o_ref.shape, F32)

    lane1 = lax.broadcasted_iota(I32, (1, LANES), 1)
    n_tok = jnp.sum(jnp.where(lane1 == grp_id, tot_ref[0:1, :], 0.0)).astype(I32)
    n_chunks = lax.div(n_tok + (MOE_CHUNK - 1), jnp.int32(MOE_CHUNK))
    lane_f = lane.astype(F32)
    lane_c = lax.broadcasted_iota(I32, (MOE_CHUNK, LANES), 1)

    def chunk_body(c, carry):
        base = (c * MOE_CHUNK).astype(F32)
        pick = ((rank_ref[...] - base) == lane_f) & (grp_ref[...] == grp_id)
        st = jnp.where(pick, 1.0, 0.0).astype(BF16)
        xc = _dot_tn(st, h_ref[...]).astype(BF16)
        wc2 = _dot_tn(st, c2_ref[...])
        wc = wc2[:, :LANES] + wc2[:, LANES:]
        y = jnp.zeros((MOE_CHUNK, x_ref.shape[1]), F32)
        for el in range(EXPERTS_PER_GROUP):
            gu = _dot(xc, wgu_ref[el])
            hid = _silu(gu[:, :D_FF_EXPERT]) * gu[:, D_FF_EXPERT:]
            w_e = jnp.sum(jnp.where(lane_c == grp_id * EXPERTS_PER_GROUP + el, wc, 0.0), axis=-1, keepdims=True)
            y = y + w_e * _dot(hid.astype(BF16), wd_ref[el])
        y_hi, y_lo = _split(y)
        o_ref[...] += _dot(jnp.concatenate([st, st], axis=1), jnp.concatenate([y_hi, y_lo], axis=0))
        return carry
    _loop_steps(n_chunks, chunk_body, 0)

    @pl.when(grp_id == pl.num_programs(1) - 1)
    def _():
        o_ref[...] = x_ref[...] + gate_ref[0] * o_ref[...]


def _moe(x2, seq, g, scale, shift, gate, rw_hi, rw_lo, rb, wgu, wd):
    ntok, d = x2.shape
    bsz = scale.shape[0]
    tm = min(1024, seq)
    row = lambda i, e: (i, 0)
    per_b = lambda i, e: ((i * tm) // seq, 0, 0)
    vec = pl.BlockSpec((1, 1, d), per_b)
    const2 = lambda a: pl.BlockSpec(a.shape, lambda i, e: (0, 0))
    tri = jnp.asarray(np.tril(np.ones((MOE_CHUNK, MOE_CHUNK), np.float32), -1), BF16)
    return pl.pallas_call(
        _moe_kernel,
        grid=(ntok // tm, N_EXPERT_GROUPS),
        in_specs=[pl.BlockSpec((tm, d), row), pl.BlockSpec((1, d), lambda i, e: (0, 0)), vec, vec, vec,
                  const2(rw_hi), const2(rw_lo), const2(rb), const2(tri),
                  pl.BlockSpec((EXPERTS_PER_GROUP, d, 2 * D_FF_EXPERT), lambda i, e: (e, 0, 0)),
                  pl.BlockSpec((EXPERTS_PER_GROUP, D_FF_EXPERT, d), lambda i, e: (e, 0, 0))],
        out_specs=pl.BlockSpec((tm, d), row),
        out_shape=jax.ShapeDtypeStruct(x2.shape, F32),
        scratch_shapes=[pltpu.VMEM((tm, d), BF16), pltpu.VMEM((tm, 2 * LANES), BF16),
                        pltpu.VMEM((tm, LANES), I32), pltpu.VMEM((tm, LANES), F32), pltpu.VMEM((SUBLANES, LANES), F32)],
        compiler_params=_params("arbitrary", "arbitrary"),
        name="moe",
    )(x2, g.reshape(1, d), scale.reshape(bsz, 1, d), shift.reshape(bsz, 1, d), gate.reshape(bsz, 1, d),
      rw_hi, rw_lo, rb, tri, wgu, wd)


def _pad_lanes(v, width=LANES):
    v = v.reshape(1, -1)
    return jnp.pad(v, ((0, 0), (0, width - v.shape[1])))


def _layer_weights(w_in, sc_conv_w, g_v_norm_g, g_spatial_w, g_spatial_b, m_conv_w, m_conv_b, m_dt_bias,
                   m_a_log, m_d, m_norm_g, w_branch, w_out):
    splits = (A_Q, A_KV, A_KV, IDX_Q, IDX_DIM, IDX_HEADS, SC_W, SC_W, SC_W, 2 * GM_W, M_DINNER, M_XBC, M_HEADS,
              N_BRANCH * D_MODEL)
    pts = [int(p) for p in np.cumsum(splits)[:-1]]
    (q, k, v, qi, ki, wi, sc_b, sc_c, sc_x, g_uv, m_z, m_xbc, m_dt, gates) = jnp.split(w_in, pts, axis=-1)
    d = w_in.shape[0]
    w_main = jnp.concatenate([gates, g_uv, q, sc_b, sc_c, sc_x, m_z, m_xbc, k, v], axis=-1).astype(BF16)
    w_small = jnp.concatenate(
        [qi, ki, wi, jnp.zeros((d, LANES - IDX_DIM - IDX_HEADS), F32), m_dt, jnp.zeros((d, LANES - M_HEADS), F32)],
        axis=-1)
    ws_hi = w_small.astype(BF16)
    ws_lo = (w_small - ws_hi.astype(F32)).astype(BF16)
    tri = np.tril(np.ones((GM_CHUNK, GM_CHUNK), dtype=bool))
    return dict(
        w_main=w_main, ws_hi=ws_hi, ws_lo=ws_lo,
        sc_conv_w=sc_conv_w, g_v_norm_g=g_v_norm_g.reshape(1, GM_W),
        ws=jnp.where(tri[None], g_spatial_w, 0.0).astype(BF16), ws_b=g_spatial_b.T,
        m_conv_w=m_conv_w, m_conv_b=m_conv_b.reshape(1, M_XBC),
        m_dt_bias=_pad_lanes(m_dt_bias), m_a_log=_pad_lanes(m_a_log), m_d=_pad_lanes(m_d),
        m_norm_g=m_norm_g.reshape(1, M_DINNER), w_branch=w_branch.astype(BF16), w_out=w_out.astype(BF16))


def kernel(x, c, positions, ada_w, ada_b, norm1_g, w_in, q_norm_g, k_norm_g, sc_conv_w, g_v_norm_g, g_spatial_w,
           g_spatial_b, m_conv_w, m_conv_b, m_dt_bias, m_a_log, m_d, m_norm_g, w_branch, w_out, norm2_g, router_w,
           router_bias, exp_w_gate, exp_w_up, exp_w_down):
    bsz, seq, d = x.shape
    depth = ada_w.shape[0]
    mod = _modulation(c, ada_w, ada_b)
    rw = jnp.pad(router_w, ((0, 0), (0, LANES - N_EXPERTS)))
    rw_hi = rw.astype(BF16)
    rw_lo = (rw - rw_hi.astype(F32)).astype(BF16)
    rb = _pad_lanes(router_bias)
    pos2 = positions.reshape(bsz * seq, 1)
    x2 = x.reshape(bsz * seq, d)
    for l in range(depth):
        shift1, scale1, gate1, shift2, scale2, gate2 = jnp.split(mod[l], 6, axis=-1)
        lw = _layer_weights(w_in[l], sc_conv_w[l], g_v_norm_g[l], g_spatial_w[l], g_spatial_b[l], m_conv_w[l],
                            m_conv_b[l], m_dt_bias[l], m_a_log[l], m_d[l], m_norm_g[l], w_branch[l], w_out[l])
        proj, small = _inproj(x2, seq, norm1_g[l], scale1, shift1, lw["w_main"], lw["ws_hi"], lw["ws_lo"])
        q_pad, k_pad, v_t, qi_cat, ki_cat = _prep(proj, small, pos2, seq, q_norm_g[l], k_norm_g[l])
        o_a = _dsa(qi_cat, small, q_pad, ki_cat, k_pad, v_t, bsz, seq)
        x2 = _mix(x2, proj, small, o_a, gate1, lw, bsz, seq)
        wgu = jnp.concatenate([exp_w_gate[l], exp_w_up[l]], axis=-1).astype(BF16)
        x2 = _moe(x2, seq, norm2_g[l], scale2, shift2, gate2, rw_hi, rw_lo, rb, wgu, exp_w_down[l].astype(BF16))
    return x2.reshape(bsz, seq, d)
```

```python
import functools
import math

import jax
import jax.numpy as jnp
import numpy as np
from jax import lax
from jax.experimental import pallas as pl
from jax.experimental.pallas import tpu as pltpu

F32 = jnp.float32
BF16 = jnp.bfloat16
I32 = jnp.int32

D_MODEL = 1024
RMS_EPS = 1e-6
ROPE_THETA = 500000.0
Q_BLOCK = 128
A_HEADS = 8
A_KV_HEADS = 2
A_HEAD_DIM = 64
A_ROT = A_HEAD_DIM // 4
A_REP = A_HEADS // A_KV_HEADS
IDX_HEADS = 8
IDX_DIM = 32
IDX_ROT = IDX_DIM // 4
IDX_TOPK_MAX = 256
SC_W = 512
SC_KERNEL = 3
GM_W = 512
GM_GROUPS = 4
GM_GROUP_W = GM_W // GM_GROUPS
GM_CHUNK = 128
M_DINNER = 512
M_HEAD_DIM = 64
M_HEADS = M_DINNER // M_HEAD_DIM
M_GROUPS = 2
M_STATE = 64
M_CONV = 4
M_CHUNK = 128
M_BC = M_GROUPS * M_STATE
M_XBC = M_DINNER + 2 * M_BC
N_BRANCH = 4
BRANCH_W = 512
N_EXPERTS = 16
N_EXPERT_GROUPS = 4
EXPERTS_PER_GROUP = N_EXPERTS // N_EXPERT_GROUPS
D_FF_EXPERT = 512
A_Q = A_HEADS * A_HEAD_DIM
A_KV = A_KV_HEADS * A_HEAD_DIM
IDX_Q = IDX_HEADS * IDX_DIM

LANES = 128
SUBLANES = 8
VMEM_LIMIT = 56 * 1024 * 1024

P_GATES = 0
P_GUV = 4096
P_Q = 5120
P_SCB = 5632
P_SCC = 6144
P_SCX = 6656
P_MZ = 7168
P_XBC = 7680
P_K = 8448
P_V = 8576
P_COLS = 8704
S_QI = 0
S_KW = 256
S_DT = 384
S_COLS = 512

INT_MIN = -2147483648
NEG_INF = float("-inf")
NEG_INF_KEY = -2139095041


def _dot(a, b):
    return jnp.dot(a, b, preferred_element_type=F32)


def _dot_nt(a, b):
    return lax.dot_general(a, b, (((1,), (1,)), ((), ())), preferred_element_type=F32)


def _dot_f32(a, b):
    return jnp.dot(a, b, preferred_element_type=F32, precision=lax.Precision.HIGHEST)


def _split(x):
    hi = x.astype(BF16)
    lo = (x - hi.astype(F32)).astype(BF16)
    return hi, lo


def _dot3(a_hi, a_lo, b_hi, b_lo):
    return _dot(a_hi, b_hi) + _dot(a_lo, b_hi) + _dot(a_hi, b_lo)


def _sigmoid(x):
    return jax.nn.sigmoid(x)


def _silu(x):
    return x * _sigmoid(x)


def _softplus(x):
    return jnp.maximum(x, 0.0) + jnp.log1p(jnp.exp(-jnp.abs(x)))


def _gelu(x):
    return 0.5 * x * (1.0 + lax.erf(x * np.float32(np.sqrt(0.5))))


def _rms(x, g_row):
    ms = jnp.mean(x * x, axis=-1, keepdims=True)
    return x * lax.rsqrt(ms + RMS_EPS) * g_row


def _params(*sem):
    return pltpu.CompilerParams(dimension_semantics=sem, vmem_limit_bytes=VMEM_LIMIT)


def _mod_kernel(c_ref, w_ref, b_ref, o_ref):
    o_ref[0] = _dot_f32(_silu(c_ref[...]), w_ref[0]) + b_ref[0]


def _modulation(c, ada_w, ada_b):
    depth, d, n = ada_w.shape
    bsz = c.shape[0]
    tn = 768
    return pl.pallas_call(
        _mod_kernel,
        grid=(depth, n // tn),
        in_specs=[pl.BlockSpec((bsz, d), lambda l, j: (0, 0)),
                  pl.BlockSpec((1, d, tn), lambda l, j: (l, 0, j)),
                  pl.BlockSpec((1, 1, tn), lambda l, j: (l, 0, j))],
        out_specs=pl.BlockSpec((1, bsz, tn), lambda l, j: (l, 0, j)),
        out_shape=jax.ShapeDtypeStruct((depth, bsz, n), F32),
        compiler_params=_params("arbitrary", "arbitrary"),
        name="modulation",
    )(c, ada_w, ada_b.reshape(depth, 1, n))


def _inproj_kernel(x_ref, g_ref, sc_ref, sh_ref, w_ref, wsh_ref, wsl_ref, o_ref, os_ref, hh_ref, hl_ref):
    @pl.when(pl.program_id(1) == 0)
    def _():
        h = _rms(x_ref[...], g_ref[...]) * (1.0 + sc_ref[0]) + sh_ref[0]
        hh, hl = _split(h)
        hh_ref[...] = hh
        hl_ref[...] = hl
        os_ref[...] = _dot3(hh, hl, wsh_ref[...], wsl_ref[...])

    o_ref[...] = _dot(hh_ref[...], w_ref[...]).astype(BF16)


def _inproj(x2, seq, g, scale, shift, w_main, ws_hi, ws_lo):
    ntok, d = x2.shape
    tm = min(1024, seq)
    tn = P_COLS // 4
    bsz = scale.shape[0]
    row = lambda i, j: (i, 0)
    per_b = lambda i, j: ((i * tm) // seq, 0, 0)
    return pl.pallas_call(
        _inproj_kernel,
        grid=(ntok // tm, P_COLS // tn),
        in_specs=[pl.BlockSpec((tm, d), row),
                  pl.BlockSpec((1, d), lambda i, j: (0, 0)),
                  pl.BlockSpec((1, 1, d), per_b),
                  pl.BlockSpec((1, 1, d), per_b),
                  pl.BlockSpec((d, tn), lambda i, j: (0, j)),
                  pl.BlockSpec((d, S_COLS), lambda i, j: (0, 0)),
                  pl.BlockSpec((d, S_COLS), lambda i, j: (0, 0))],
        out_specs=[pl.BlockSpec((tm, tn), lambda i, j: (i, j)),
                   pl.BlockSpec((tm, S_COLS), row)],
        out_shape=[jax.ShapeDtypeStruct((ntok, P_COLS), BF16),
                   jax.ShapeDtypeStruct((ntok, S_COLS), F32)],
        scratch_shapes=[pltpu.VMEM((tm, d), BF16), pltpu.VMEM((tm, d), BF16)],
        compiler_params=_params("arbitrary", "arbitrary"),
        name="inproj",
    )(x2, g.reshape(1, d), scale.reshape(bsz, 1, d), shift.reshape(bsz, 1, d), w_main, ws_hi, ws_lo)


ROPE_ZERO_LANE = A_ROT // 2 + IDX_ROT // 2


def _rope_angles(pos_f):
    half_a, half_i = A_ROT // 2, IDX_ROT // 2
    lane = lax.broadcasted_iota(I32, (1, LANES), 1)
    k = jnp.where(lane < half_a, lane, lane - half_a).astype(F32)
    rot = jnp.where(lane < half_a, np.float32(A_ROT), np.float32(IDX_ROT))
    inv_freq = jnp.exp(np.float32(-math.log(ROPE_THETA)) * k * 2.0 / rot)
    ang = pos_f * jnp.where(lane < half_a + half_i, inv_freq, 0.0)
    return jnp.cos(ang), jnp.sin(ang)


def _rope_tables(cos_sin, place_ref, period, rot, limit=LANES):
    half = rot // 2

    def expand(x):
        hi, lo = _split(x)
        return _dot(hi, place_ref[...]) + _dot(lo, place_ref[...])

    lane_all = lax.broadcasted_iota(I32, (1, LANES), 1)
    lane = lane_all % period
    cos = jnp.where(lane_all < limit, expand(cos_sin[0]), 1.0)
    sin = jnp.where(lane_all < limit, expand(cos_sin[1]), 0.0)
    sin_lo = jnp.where(lane < half, -sin, 0.0)
    sin_hi = jnp.where((lane >= half) & (lane < rot), sin, 0.0)
    return cos, sin_lo, sin_hi


def _rope(x, tables, half):
    cos, sin_lo, sin_hi = tables
    width = x.shape[-1]
    return (x * cos + pltpu.roll(x, width - half, 1) * sin_lo + pltpu.roll(x, half, 1) * sin_hi)


def _prep_kernel(q_ref, k_ref, v_ref, qi_ref, kw_ref, pos_ref, qg_ref, kg_ref, bdq_ref, bdk_ref,
                 pq_ref, pk_ref, pqi1_ref, pqi2_ref, pki1_ref, pki2_ref, pra_ref, pri_ref,
                 qo_ref, ko_ref, vto_ref, qio_ref, kio_ref):
    cos_sin = _rope_angles(pos_ref[...].astype(F32))
    att_tab = _rope_tables(cos_sin, pra_ref, A_HEAD_DIM, A_ROT)
    idx_tab = _rope_tables(cos_sin, pri_ref, IDX_DIM, IDX_ROT)
    kw_tab = _rope_tables(cos_sin, pri_ref, IDX_DIM, IDX_ROT, limit=IDX_DIM)

    def head_norm(x, bd_ref, g_ref):
        sq_hi, sq_lo = _split(x * x)
        ms = (_dot(sq_hi, bd_ref[...]) + _dot(sq_lo, bd_ref[...])) * np.float32(1.0 / A_HEAD_DIM)
        return x * lax.rsqrt(ms + RMS_EPS) * g_ref[...]

    def rope_wide(x, tab, half):
        pieces = [_rope(x[:, c * LANES:(c + 1) * LANES], tab, half) for c in range(x.shape[-1] // LANES)]
        return pieces[0] if len(pieces) == 1 else jnp.concatenate(pieces, axis=-1)

    def store_head_rows(o_ref, x):
        for blk in range(x.shape[0] // Q_BLOCK):
            for h in range(x.shape[1] // LANES):
                r0 = (blk * (x.shape[1] // LANES) + h) * Q_BLOCK
                o_ref[r0:r0 + Q_BLOCK, :] = x[blk * Q_BLOCK:(blk + 1) * Q_BLOCK, h * LANES:(h + 1) * LANES]

    q = rope_wide(head_norm(q_ref[...].astype(F32), bdq_ref, qg_ref), att_tab, A_ROT // 2)
    q = (q * np.float32(A_HEAD_DIM ** -0.5)).astype(BF16)
    store_head_rows(qo_ref, _dot(q, pq_ref[...]).astype(BF16))
    k = rope_wide(head_norm(k_ref[...].astype(F32), bdk_ref, kg_ref), att_tab, A_ROT // 2).astype(BF16)
    ko_ref[...] = _dot(k, pk_ref[...]).astype(BF16)
    v_pad = _dot(v_ref[...], pk_ref[...])
    ones_lane = (lax.broadcasted_iota(I32, (1, A_KV_HEADS * LANES), 1) % LANES) == A_HEAD_DIM
    vto_ref[...] = jnp.transpose(jnp.where(ones_lane, 1.0, v_pad)).astype(BF16)

    qi_hi, qi_lo = _split(rope_wide(qi_ref[...], idx_tab, IDX_ROT // 2))
    store_head_rows(qio_ref, (_dot(qi_hi, pqi1_ref[...]) + _dot(qi_lo, pqi2_ref[...])).astype(BF16))
    ki_hi, ki_lo = _split(_rope(kw_ref[...], kw_tab, IDX_ROT // 2))
    kio_ref[...] = (_dot(ki_hi, pki1_ref[...]) + _dot(ki_lo, pki2_ref[...])).astype(BF16)


@functools.lru_cache(maxsize=None)
def _prep_constants():
    def block_diag(width):
        i = np.arange(width)
        return (i[:, None] // A_HEAD_DIM == i[None, :] // A_HEAD_DIM).astype(np.float32)

    def pad_heads(n_heads):
        p = np.zeros((n_heads * A_HEAD_DIM, n_heads * LANES), np.float32)
        for h in range(n_heads):
            for d in range(A_HEAD_DIM):
                p[h * A_HEAD_DIM + d, h * LANES + d] = 1.0
        return p

    pqi1 = np.zeros((IDX_Q, IDX_HEADS * LANES), np.float32)
    pqi2 = np.zeros((IDX_Q, IDX_HEADS * LANES), np.float32)
    for h in range(IDX_HEADS):
        for d in range(IDX_DIM):
            pqi1[h * IDX_DIM + d, h * LANES + d] = 1.0
            pqi1[h * IDX_DIM + d, h * LANES + IDX_DIM + d] = 1.0
            pqi2[h * IDX_DIM + d, h * LANES + 2 * IDX_DIM + d] = 1.0
    pki1 = np.zeros((LANES, LANES), np.float32)
    pki2 = np.zeros((LANES, LANES), np.float32)
    for d in range(IDX_DIM):
        pki1[d, d] = 1.0
        pki1[d, 2 * IDX_DIM + d] = 1.0
        pki2[d, IDX_DIM + d] = 1.0
    def rope_place(period, rot, first_src):
        p = np.zeros((LANES, LANES), np.float32)
        for j in range(LANES):
            jj = j % period
            p[first_src + jj % (rot // 2) if jj < rot else ROPE_ZERO_LANE, j] = 1.0
        return p

    mats = (block_diag(A_Q), block_diag(A_KV), pad_heads(A_HEADS), pad_heads(A_KV_HEADS), pqi1, pqi2, pki1, pki2,
            rope_place(A_HEAD_DIM, A_ROT, 0), rope_place(IDX_DIM, IDX_ROT, A_ROT // 2))
    return tuple(np.asarray(m) for m in mats)


def _prep(proj, small, pos2, seq, q_g, k_g):
    ntok = proj.shape[0]
    tm = min(512, seq)
    consts = [jnp.asarray(m, BF16) for m in _prep_constants()]
    row = lambda w, c: pl.BlockSpec((tm, w), lambda i: (i, c))
    full = lambda a: pl.BlockSpec(a.shape, lambda i: (0, 0))
    qg = jnp.tile(q_g, A_HEADS).reshape(1, A_Q)
    kg = jnp.tile(k_g, A_KV_HEADS).reshape(1, A_KV)
    return pl.pallas_call(
        _prep_kernel,
        grid=(ntok // tm,),
        in_specs=[row(A_Q, P_Q // A_Q), row(A_KV, P_K // A_KV), row(A_KV, P_V // A_KV), row(IDX_Q, S_QI // IDX_Q),
                  row(LANES, S_KW // LANES), row(1, 0), full(qg), full(kg)] + [full(m) for m in consts],
        out_specs=[pl.BlockSpec((tm * A_HEADS, LANES), lambda i: (i, 0)), row(A_KV_HEADS * LANES, 0),
                   pl.BlockSpec((A_KV_HEADS * LANES, tm), lambda i: (0, i)),
                   pl.BlockSpec((tm * IDX_HEADS, LANES), lambda i: (i, 0)), row(LANES, 0)],
        out_shape=[jax.ShapeDtypeStruct((ntok * A_HEADS, LANES), BF16),
                   jax.ShapeDtypeStruct((ntok, A_KV_HEADS * LANES), BF16),
                   jax.ShapeDtypeStruct((A_KV_HEADS * LANES, ntok), BF16),
                   jax.ShapeDtypeStruct((ntok * IDX_HEADS, LANES), BF16),
                   jax.ShapeDtypeStruct((ntok, LANES), BF16)],
        compiler_params=_params("arbitrary"),
        name="attn_prep",
    )(proj, proj, proj, small, small, pos2, qg, kg, *consts)


def _sortable(x):
    bits = pltpu.bitcast(jnp.where(x == 0.0, 0.0, x), I32)
    return jnp.where(bits < 0, bits ^ jnp.int32(0x7FFFFFFF), bits)


KEY_STEP = 512
KEY_SUBS = KEY_STEP // Q_BLOCK


def _key_row0(step, sub):
    return pl.multiple_of(step * KEY_STEP + sub * Q_BLOCK, Q_BLOCK)


MXU_LOOP_WIDTHS = (4, 2, 1)


def _loop_steps(nstep, body, init, widths=(2, 1)):
    carry, start = init, jnp.int32(0)
    for w in widths:
        count = lax.div(nstep - start, jnp.int32(w))

        def multi(i, c, w=w, start=start):
            for k in range(w):
                c = body(start + i * w + k, c)
            return c
        carry = lax.fori_loop(0, count, multi, carry)
        start = start + count * w
    return carry


def _count_rows(pred_fn, nstep):
    def body(c, acc):
        for sub in range(KEY_SUBS):
            ones = jnp.where(pred_fn(_key_row0(c, sub)), jnp.int32(1), jnp.int32(0))
            acc = acc + jnp.sum(ones.reshape(Q_BLOCK // SUBLANES, SUBLANES, LANES), axis=0)
        return acc
    acc = _loop_steps(nstep, body, jnp.zeros((SUBLANES, LANES), I32), MXU_LOOP_WIDTHS)
    return jnp.sum(acc, axis=0, keepdims=True)


PAIR = 2 * LANES
N_PAIRS = A_HEADS // 2


def _dsa_kernel(qi_ref, kw_ref, q_ref, ki_ref, k_ref, vt_ref, o_ref, key_ref, s_ref, lim_ref, oacc_ref,
                *, topk, idx_bits):
    qb = pl.program_id(1)
    nstep = lax.div(qb, jnp.int32(KEY_SUBS)) + 1
    row_i = lax.broadcasted_iota(I32, (Q_BLOCK, LANES), 0)
    q_pos = qb * Q_BLOCK + lax.broadcasted_iota(I32, (Q_BLOCK, LANES), 1)

    w_rows = jnp.transpose(kw_ref[...])[IDX_DIM:IDX_DIM + IDX_HEADS, :]
    w_rows = w_rows * np.float32(IDX_HEADS ** -0.5) * np.float32(IDX_DIM ** -0.5)

    def rows(r0):
        return pl.ds(r0, Q_BLOCK)

    def score_body(c, carry):
        for sub in range(KEY_SUBS):
            r0 = _key_row0(c, sub)
            ki = ki_ref[rows(r0), :]
            s = jnp.zeros((Q_BLOCK, LANES), F32)
            for pair in range(IDX_HEADS // 2):
                d = _dot_nt(ki, qi_ref[pair * PAIR:(pair + 1) * PAIR, :])
                for j in range(2):
                    h = 2 * pair + j
                    s = s + jnp.maximum(d[:, j * LANES:(j + 1) * LANES], 0.0) * w_rows[h:h + 1, :]
            key_ref[rows(r0), :] = _sortable(jnp.where((r0 + row_i) <= q_pos, s, NEG_INF))
        return carry
    _loop_steps(nstep, score_body, 0, MXU_LOOP_WIDTHS)

    def bit_body(i, cand):
        trial = cand | lax.shift_left(jnp.int32(1), 31 - i)
        trial_s = trial ^ jnp.int32(INT_MIN)
        cnt = _count_rows(lambda r0: key_ref[rows(r0), :] >= trial_s, nstep)
        return jnp.where(cnt >= topk, trial, cand)
    cand = lax.fori_loop(0, 32, bit_body, jnp.zeros((1, LANES), I32))
    thr = cand ^ jnp.int32(INT_MIN)

    n_gt = _count_rows(lambda r0: key_ref[rows(r0), :] > thr, nstep)
    n_eq = _count_rows(lambda r0: key_ref[rows(r0), :] == thr, nstep)
    need = topk - n_gt
    excess = (n_eq > need) & (thr > jnp.int32(NEG_INF_KEY))
    lim_ref[0:1, :] = jnp.full((1, LANES), 2 ** 30, I32)

    @pl.when(jnp.max(jnp.where(excess, 1, 0)) > 0)
    def _():
        def idx_body(i, lim):
            trial = lim | lax.shift_left(jnp.int32(1), idx_bits - 1 - i)
            cnt = _count_rows(lambda r0: (key_ref[rows(r0), :] == thr) & ((r0 + row_i) < trial), nstep)
            return jnp.where(cnt < need, trial, lim)
        lim = lax.fori_loop(0, idx_bits, idx_body, jnp.zeros((1, LANES), I32))
        lim_ref[0:1, :] = jnp.where(excess, lim, 2 ** 30)
    tie_lim = lim_ref[0:1, :]

    def max_body(c, m_all):
        m_all = list(m_all)
        for sub in range(KEY_SUBS):
            r0 = _key_row0(c, sub)
            keys = key_ref[rows(r0), :]
            kpos = r0 + row_i
            sel = ((keys > thr) | ((keys == thr) & (kpos <= tie_lim))) & (kpos <= q_pos)
            bias = jnp.where(sel, 0.0, NEG_INF)
            bias2 = jnp.concatenate([bias, bias], axis=-1)
            for pair in range(N_PAIRS):
                g = (2 * pair) // A_REP
                s = _dot_nt(k_ref[rows(r0), g * LANES:(g + 1) * LANES], q_ref[pair * PAIR:(pair + 1) * PAIR, :]) + bias2
                s_ref[rows(r0), pair * PAIR:(pair + 1) * PAIR] = s
                m_all[pair] = jnp.maximum(m_all[pair], jnp.max(s, axis=0, keepdims=True))
        return tuple(m_all)
    m_fin = _loop_steps(nstep, max_body, tuple(jnp.full((1, PAIR), -1e30, F32) for _ in range(N_PAIRS)),
                        MXU_LOOP_WIDTHS)

    oacc_ref[...] = jnp.zeros(oacc_ref.shape, F32)

    def pv_body(c, carry):
        cols = pl.ds(pl.multiple_of(c * KEY_STEP, KEY_STEP), KEY_STEP)
        for pair in range(N_PAIRS):
            g = (2 * pair) // A_REP
            p = jnp.exp(s_ref[cols, pair * PAIR:(pair + 1) * PAIR] - m_fin[pair]).astype(BF16)
            oacc_ref[pair] += _dot(vt_ref[g * LANES:(g + 1) * LANES, cols], p)
        return carry
    _loop_steps(nstep, pv_body, 0, MXU_LOOP_WIDTHS)

    outs = []
    for pair in range(N_PAIRS):
        o_p = oacc_ref[pair]
        for j in range(2):
            o_h = o_p[:, j * LANES:(j + 1) * LANES]
            outs.append(o_h[0:A_HEAD_DIM, :] / o_h[A_HEAD_DIM:A_HEAD_DIM + 1, :])
    o_ref[...] = jnp.transpose(jnp.concatenate(outs, axis=0)).astype(BF16)


def _dsa(qi_cat, small, q_pad, ki_cat, k_pad, v_t, bsz, seq):
    assert seq % KEY_STEP == 0
    nb = seq // Q_BLOCK
    topk = min(IDX_TOPK_MAX, seq // 4)
    idx_bits = max(1, int(math.ceil(math.log2(seq))))
    qrow = lambda w, c: pl.BlockSpec((Q_BLOCK, w), lambda b, j: (b * nb + j, c))
    brow = lambda w: pl.BlockSpec((seq, w), lambda b, j: (b, 0))
    return pl.pallas_call(
        functools.partial(_dsa_kernel, topk=topk, idx_bits=idx_bits),
        grid=(bsz, nb),
        in_specs=[pl.BlockSpec((IDX_HEADS * Q_BLOCK, LANES), lambda b, j: (b * nb + j, 0)),
                  qrow(LANES, S_KW // LANES),
                  pl.BlockSpec((A_HEADS * Q_BLOCK, LANES), lambda b, j: (b * nb + j, 0)),
                  brow(LANES), brow(A_KV_HEADS * LANES),
                  pl.BlockSpec((A_KV_HEADS * LANES, seq), lambda b, j: (0, b))],
        out_specs=qrow(A_Q, 0),
        out_shape=jax.ShapeDtypeStruct((bsz * seq, A_Q), BF16),
        scratch_shapes=[pltpu.VMEM((seq, LANES), I32), pltpu.VMEM((seq, A_HEADS * LANES), F32),
                        pltpu.VMEM((SUBLANES, LANES), I32), pltpu.VMEM((N_PAIRS, LANES, PAIR), F32)],
        compiler_params=_params("arbitrary", "arbitrary"),
        name="sparse_attention",
    )(qi_cat, small, q_pad, ki_cat, k_pad, v_t)


HALO = SUBLANES


def _causal_conv(ext_ref, u, w_ref, taps, rows):
    ext_ref[HALO:HALO + rows, :] = u
    y = u * w_ref[taps - 1:taps, :]
    for j in range(taps - 1):
        shift = taps - 1 - j
        y = y + ext_ref[HALO - shift:HALO - shift + rows, :] * w_ref[j:j + 1, :]
    ext_ref[0:HALO, :] = ext_ref[rows:rows + HALO, :]
    return y


def _mix_kernel(x_ref, g0_ref, g1_ref, g2_ref, g3_ref, guv_ref, scb_ref, scc_ref, scx_ref, mz_ref, xbc_ref,
                dt_ref, oa_ref, gate_ref, scw_ref, gvg_ref, ws_ref, wsb_ref, mcw_ref, mcb_ref, dtb_ref, alog_ref,
                md_ref, mng_ref, wb_ref, wo_ref, o_ref, sc_ext, xbc_ext, state_ref, y_ref, *, rows):
    @pl.when(pl.program_id(1) == 0)
    def _():
        sc_ext[0:HALO, :] = jnp.zeros((HALO, SC_W), F32)
        xbc_ext[0:HALO, :] = jnp.zeros((HALO, M_XBC), F32)
        state_ref[...] = jnp.zeros(state_ref.shape, F32)

    nchunk = rows // M_CHUNK
    row_i = lax.broadcasted_iota(I32, (M_CHUNK, M_CHUNK), 0)
    col_i = lax.broadcasted_iota(I32, (M_CHUNK, M_CHUNK), 1)
    tri = row_i >= col_i

    o_b = scb_ref[...].astype(F32) * _causal_conv(
        sc_ext, scc_ref[...].astype(F32) * scx_ref[...].astype(F32), scw_ref, SC_KERNEL, rows)

    guv = _gelu(guv_ref[...].astype(F32))
    gu = guv[:, :GM_W]
    gv = _rms(guv[:, GM_W:], gvg_ref[...]).astype(BF16)
    mixed_rows = []
    for ci in range(nchunk):
        r0 = ci * GM_CHUNK
        cols = []
        for gi in range(GM_GROUPS):
            vb = gv[r0:r0 + GM_CHUNK, gi * GM_GROUP_W:(gi + 1) * GM_GROUP_W]
            cols.append(_dot(ws_ref[gi], vb) + wsb_ref[:, gi:gi + 1])
        mixed_rows.append(jnp.concatenate(cols, axis=-1))
    o_c = gu * jnp.concatenate(mixed_rows, axis=0)

    xbc = _silu(_causal_conv(xbc_ext, xbc_ref[...].astype(F32), mcw_ref, M_CONV, rows) + mcb_ref[...])
    head_lane = lax.broadcasted_iota(I32, (1, LANES), 1) < M_HEADS
    dt = jnp.where(head_lane, _softplus(dt_ref[...] + dtb_ref[...]), 0.0)
    a_row = jnp.where(head_lane, -jnp.exp(alog_ref[...]), 0.0)
    tri_f = jnp.where(tri, 1.0, 0.0).astype(F32)
    for ci in range(nchunk):
        r0 = ci * M_CHUNK
        dtc = dt[r0:r0 + M_CHUNK, :]
        acs = _dot_f32(tri_f, dtc * a_row)
        acs_t = jnp.transpose(acs)
        b_t = jnp.transpose(xbc[r0:r0 + M_CHUNK, M_DINNER:M_DINNER + M_BC]).astype(BF16)
        cm = xbc[r0:r0 + M_CHUNK, M_DINNER + M_BC:M_XBC].astype(BF16)
        for g in range(M_GROUPS):
            bg_t = b_t[g * M_STATE:(g + 1) * M_STATE, :]
            cg = cm[:, g * M_STATE:(g + 1) * M_STATE]
            cb = _dot(cg, bg_t)
            for h in range(g * (M_HEADS // M_GROUPS), (g + 1) * (M_HEADS // M_GROUPS)):
                col = acs[:, h:h + 1]
                row = acs_t[h:h + 1, :]
                last = acs[M_CHUNK - 1:M_CHUNK, h:h + 1]
                decay = jnp.exp(jnp.where(tri, col - row, NEG_INF))
                xs_h = xbc[r0:r0 + M_CHUNK, h * M_HEAD_DIM:(h + 1) * M_HEAD_DIM]
                xdt = xs_h * dtc[:, h:h + 1]
                y = _dot((cb * decay).astype(BF16), xdt.astype(BF16))
                prev_t = state_ref[h]
                y = y + _dot(cg, prev_t.astype(BF16)) * jnp.exp(col)
                st_t = _dot(bg_t, (xdt * jnp.exp(last - col)).astype(BF16))
                state_ref[h] = prev_t * jnp.exp(last) + st_t
                y_ref[r0:r0 + M_CHUNK, h * M_HEAD_DIM:(h + 1) * M_HEAD_DIM] = y + md_ref[:, h:h + 1] * xs_h
    o_d = _rms(y_ref[...] * _silu(mz_ref[...].astype(F32)), mng_ref[...])

    branches = (oa_ref[...], o_b.astype(BF16), o_c.astype(BF16), o_d.astype(BF16))
    gate_refs = (g0_ref, g1_ref, g2_ref, g3_ref)
    merged = None
    for i in range(N_BRANCH):
        term = _sigmoid(gate_refs[i][...].astype(F32)) * _dot(branches[i], wb_ref[i])
        merged = term if merged is None else merged + term
    o_ref[...] = x_ref[...] + gate_ref[0] * _dot(merged.astype(BF16), wo_ref[...])


def _mix(x2, proj, small, o_a, gate1, lw, bsz, seq):
    d = x2.shape[1]
    rows = min(256, seq)
    ns = seq // rows
    tok = lambda w, c: pl.BlockSpec((rows, w), lambda b, s: (b * ns + s, c))
    full = lambda a: pl.BlockSpec(a.shape, lambda b, s: (0,) * a.ndim)
    weights = [lw["sc_conv_w"], lw["g_v_norm_g"], lw["ws"], lw["ws_b"], lw["m_conv_w"], lw["m_conv_b"],
               lw["m_dt_bias"], lw["m_a_log"], lw["m_d"], lw["m_norm_g"], lw["w_branch"], lw["w_out"]]
    return pl.pallas_call(
        functools.partial(_mix_kernel, rows=rows),
        grid=(bsz, ns),
        in_specs=[tok(d, 0)] + [tok(D_MODEL, i) for i in range(N_BRANCH)]
        + [tok(2 * GM_W, P_GUV // (2 * GM_W)), tok(SC_W, P_SCB // SC_W), tok(SC_W, P_SCC // SC_W),
           tok(SC_W, P_SCX // SC_W), tok(M_DINNER, P_MZ // M_DINNER), tok(M_XBC, P_XBC // M_XBC),
           tok(LANES, S_DT // LANES), tok(A_Q, 0),
           pl.BlockSpec((1, 1, d), lambda b, s: (b, 0, 0))]
        + [full(w) for w in weights],
        out_specs=tok(d, 0),
        out_shape=jax.ShapeDtypeStruct(x2.shape, F32),
        scratch_shapes=[pltpu.VMEM((HALO + rows, SC_W), F32), pltpu.VMEM((HALO + rows, M_XBC), F32),
                        pltpu.VMEM((M_HEADS, M_STATE, M_HEAD_DIM), F32), pltpu.VMEM((rows, M_DINNER), F32)],
        compiler_params=_params("arbitrary", "arbitrary"),
        name="mixers_merge",
    )(x2, proj, proj, proj, proj, proj, proj, proj, proj, proj, proj, small, o_a,
      gate1.reshape(bsz, 1, d), *weights)


def _swap_lanes(x, lane, dist):
    up = pltpu.roll(x, LANES - dist, 1)
    down = pltpu.roll(x, dist, 1)
    return jnp.where((lane & (2 * dist - 1)) < dist, up, down)


def _route(scores, bias_row):
    lane = lax.broadcasted_iota(I32, scores.shape, 1)
    valid = lane < N_EXPERTS
    sel = jnp.where(valid, scores + bias_row, NEG_INF)
    p1 = _swap_lanes(sel, lane, 1)
    hi1, lo1 = jnp.maximum(sel, p1), jnp.minimum(sel, p1)
    hi2, lo2 = _swap_lanes(hi1, lane, 2), _swap_lanes(lo1, lane, 2)
    grp = jnp.maximum(hi1, hi2) + jnp.maximum(jnp.minimum(hi1, hi2), jnp.maximum(lo1, lo2))
    grp = jnp.where(valid, grp, NEG_INF)
    best_val = jnp.max(grp, axis=-1, keepdims=True)
    big = jnp.int32(LANES)
    lane_grp = lax.shift_right_logical(lane, int(math.log2(EXPERTS_PER_GROUP)))
    best = jnp.min(jnp.where(grp == best_val, lane_grp, big), axis=-1, keepdims=True)
    masked = jnp.where((lane_grp == best) & valid, sel, NEG_INF)
    v1 = jnp.max(masked, axis=-1, keepdims=True)
    i1 = jnp.min(jnp.where(masked == v1, lane, big), axis=-1, keepdims=True)
    masked2 = jnp.where(lane == i1, NEG_INF, masked)
    v2 = jnp.max(masked2, axis=-1, keepdims=True)
    i2 = jnp.min(jnp.where(masked2 == v2, lane, big), axis=-1, keepdims=True)
    w1 = jnp.sum(jnp.where(lane == i1, scores, 0.0), axis=-1, keepdims=True)
    w2 = jnp.sum(jnp.where(lane == i2, scores, 0.0), axis=-1, keepdims=True)
    tot = w1 + w2
    return jnp.where(lane == i1, w1 / tot, 0.0) + jnp.where(lane == i2, w2 / tot, 0.0), best


MOE_CHUNK = 128


def _dot_tn(a, b):
    return lax.dot_general(a, b, (((0,), (0,)), ((), ())), preferred_element_type=F32)


def _moe_kernel(x_ref, g_ref, sc_ref, sh_ref, gate_ref, rwh_ref, rwl_ref, rb_ref, tri_ref, wgu_ref, wd_ref, o_ref,
                h_ref, c2_ref, grp_ref, rank_ref, tot_ref):
    grp_id = pl.program_id(1)
    tm = x_ref.shape[0]
    lane = lax.broadcasted_iota(I32, (tm, LANES), 1)

    @pl.when(grp_id == 0)
    def _():
        h = _rms(x_ref[...], g_ref[...]) * (1.0 + sc_ref[0]) + sh_ref[0]
        hh, hl = _split(h)
        h_ref[...] = hh
        comb, best = _route(_sigmoid(_dot3(hh, hl, rwh_ref[...], rwl_ref[...])), rb_ref[...])
        c_hi, c_lo = _split(comb)
        c2_ref[:, :LANES] = c_hi
        c2_ref[:, LANES:] = c_lo
        grp_ref[...] = jnp.broadcast_to(best, (tm, LANES))
        onehot = jnp.where(lane == best, 1.0, 0.0)
        offs = jnp.zeros((1, LANES), F32)
        for b in range(tm // MOE_CHUNK):
            blk = onehot[b * MOE_CHUNK:(b + 1) * MOE_CHUNK, :]
            ranks = _dot(tri_ref[...], blk.astype(BF16)) + offs
            own = jnp.sum(blk * ranks, axis=-1, keepdims=True)
            rank_ref[b * MOE_CHUNK:(b + 1) * MOE_CHUNK, :] = jnp.broadcast_to(own, (MOE_CHUNK, LANES))
            offs = offs + jnp.sum(blk, axis=0, keepdims=True)
        tot_ref[0:1, :] = offs
        o_ref[...] = jnp.zeros(o_ref.shape, F32)

    lane1 = lax.broadcasted_iota(I32, (1, LANES), 1)
    n_tok = jnp.sum(jnp.where(lane1 == grp_id, tot_ref[0:1, :], 0.0)).astype(I32)
    n_chunks = lax.div(n_tok + (MOE_CHUNK - 1), jnp.int32(MOE_CHUNK))
    lane_f = lane.astype(F32)
    lane_c = lax.broadcasted_iota(I32, (MOE_CHUNK, LANES), 1)

    def chunk_body(c, carry):
        base = (c * MOE_CHUNK).astype(F32)
        pick = ((rank_ref[...] - base) == lane_f) & (grp_ref[...] == grp_id)
        st = jnp.where(pick, 1.0, 0.0).astype(BF16)
        xc = _dot_tn(st, h_ref[...]).astype(BF16)
        wc2 = _dot_tn(st, c2_ref[...])
        wc = wc2[:, :LANES] + wc2[:, LANES:]
        y = jnp.zeros((MOE_CHUNK, x_ref.shape[1]), F32)
        for el in range(EXPERTS_PER_GROUP):
            gu = _dot(xc, wgu_ref[el])
            hid = _silu(gu[:, :D_FF_EXPERT]) * gu[:, D_FF_EXPERT:]
            w_e = jnp.sum(jnp.where(lane_c == grp_id * EXPERTS_PER_GROUP + el, wc, 0.0), axis=-1, keepdims=True)
            y = y + w_e * _dot(hid.astype(BF16), wd_ref[el])
        y_hi, y_lo = _split(y)
        o_ref[...] += _dot(jnp.concatenate([st, st], axis=1), jnp.concatenate([y_hi, y_lo], axis=0))
        return carry
    _loop_steps(n_chunks, chunk_body, 0)

    @pl.when(grp_id == pl.num_programs(1) - 1)
    def _():
        o_ref[...] = x_ref[...] + gate_ref[0] * o_ref[...]


def _moe(x2, seq, g, scale, shift, gate, rw_hi, rw_lo, rb, wgu, wd):
    ntok, d = x2.shape
    bsz = scale.shape[0]
    tm = min(1024, seq)
    row = lambda i, e: (i, 0)
    per_b = lambda i, e: ((i * tm) // seq, 0, 0)
    vec = pl.BlockSpec((1, 1, d), per_b)
    const2 = lambda a: pl.BlockSpec(a.shape, lambda i, e: (0, 0))
    tri = jnp.asarray(np.tril(np.ones((MOE_CHUNK, MOE_CHUNK), np.float32), -1), BF16)
    return pl.pallas_call(
        _moe_kernel,
        grid=(ntok // tm, N_EXPERT_GROUPS),
        in_specs=[pl.BlockSpec((tm, d), row), pl.BlockSpec((1, d), lambda i, e: (0, 0)), vec, vec, vec,
                  const2(rw_hi), const2(rw_lo), const2(rb), const2(tri),
                  pl.BlockSpec((EXPERTS_PER_GROUP, d, 2 * D_FF_EXPERT), lambda i, e: (e, 0, 0)),
                  pl.BlockSpec((EXPERTS_PER_GROUP, D_FF_EXPERT, d), lambda i, e: (e, 0, 0))],
        out_specs=pl.BlockSpec((tm, d), row),
        out_shape=jax.ShapeDtypeStruct(x2.shape, F32),
        scratch_shapes=[pltpu.VMEM((tm, d), BF16), pltpu.VMEM((tm, 2 * LANES), BF16),
                        pltpu.VMEM((tm, LANES), I32), pltpu.VMEM((tm, LANES), F32), pltpu.VMEM((SUBLANES, LANES), F32)],
        compiler_params=_params("arbitrary", "arbitrary"),
        name="moe",
    )(x2, g.reshape(1, d), scale.reshape(bsz, 1, d), shift.reshape(bsz, 1, d), gate.reshape(bsz, 1, d),
      rw_hi, rw_lo, rb, tri, wgu, wd)


def _pad_lanes(v, width=LANES):
    v = v.reshape(1, -1)
    return jnp.pad(v, ((0, 0), (0, width - v.shape[1])))


def _layer_weights(w_in, sc_conv_w, g_v_norm_g, g_spatial_w, g_spatial_b, m_conv_w, m_conv_b, m_dt_bias,
                   m_a_log, m_d, m_norm_g, w_branch, w_out):
    splits = (A_Q, A_KV, A_KV, IDX_Q, IDX_DIM, IDX_HEADS, SC_W, SC_W, SC_W, 2 * GM_W, M_DINNER, M_XBC, M_HEADS,
              N_BRANCH * D_MODEL)
    pts = [int(p) for p in np.cumsum(splits)[:-1]]
    (q, k, v, qi, ki, wi, sc_b, sc_c, sc_x, g_uv, m_z, m_xbc, m_dt, gates) = jnp.split(w_in, pts, axis=-1)
    d = w_in.shape[0]
    w_main = jnp.concatenate([gates, g_uv, q, sc_b, sc_c, sc_x, m_z, m_xbc, k, v], axis=-1).astype(BF16)
    w_small = jnp.concatenate(
        [qi, ki, wi, jnp.zeros((d, LANES - IDX_DIM - IDX_HEADS), F32), m_dt, jnp.zeros((d, LANES - M_HEADS), F32)],
        axis=-1)
    ws_hi = w_small.astype(BF16)
    ws_lo = (w_small - ws_hi.astype(F32)).astype(BF16)
    tri = np.tril(np.ones((GM_CHUNK, GM_CHUNK), dtype=bool))
    return dict(
        w_main=w_main, ws_hi=ws_hi, ws_lo=ws_lo,
        sc_conv_w=sc_conv_w, g_v_norm_g=g_v_norm_g.reshape(1, GM_W),
        ws=jnp.where(tri[None], g_spatial_w, 0.0).astype(BF16), ws_b=g_spatial_b.T,
        m_conv_w=m_conv_w, m_conv_b=m_conv_b.reshape(1, M_XBC),
        m_dt_bias=_pad_lanes(m_dt_bias), m_a_log=_pad_lanes(m_a_log), m_d=_pad_lanes(m_d),
        m_norm_g=m_norm_g.reshape(1, M_DINNER), w_branch=w_branch.astype(BF16), w_out=w_out.astype(BF16))


def kernel(x, c, positions, ada_w, ada_b, norm1_g, w_in, q_norm_g, k_norm_g, sc_conv_w, g_v_norm_g, g_spatial_w,
           g_spatial_b, m_conv_w, m_conv_b, m_dt_bias, m_a_log, m_d, m_norm_g, w_branch, w_out, norm2_g, router_w,
           router_bias, exp_w_gate, exp_w_up, exp_w_down):
    bsz, seq, d = x.shape
    depth = ada_w.shape[0]
    mod = _modulation(c, ada_w, ada_b)
    rw = jnp.pad(router_w, ((0, 0), (0, LANES - N_EXPERTS)))
    rw_hi = rw.astype(BF16)
    rw_lo = (rw - rw_hi.astype(F32)).astype(BF16)
    rb = _pad_lanes(router_bias)
    pos2 = positions.reshape(bsz * seq, 1)
    x2 = x.reshape(bsz * seq, d)
    for l in range(depth):
        shift1, scale1, gate1, shift2, scale2, gate2 = jnp.split(mod[l], 6, axis=-1)
        lw = _layer_weights(w_in[l], sc_conv_w[l], g_v_norm_g[l], g_spatial_w[l], g_spatial_b[l], m_conv_w[l],
                            m_conv_b[l], m_dt_bias[l], m_a_log[l], m_d[l], m_norm_g[l], w_branch[l], w_out[l])
        proj, small = _inproj(x2, seq, norm1_g[l], scale1, shift1, lw["w_main"], lw["ws_hi"], lw["ws_lo"])
        q_pad, k_pad, v_t, qi_cat, ki_cat = _prep(proj, small, pos2, seq, q_norm_g[l], k_norm_g[l])
        o_a = _dsa(qi_cat, small, q_pad, ki_cat, k_pad, v_t, bsz, seq)
        x2 = _mix(x2, proj, small, o_a, gate1, lw, bsz, seq)
        wgu = jnp.concatenate([exp_w_gate[l], exp_w_up[l]], axis=-1).astype(BF16)
        x2 = _moe(x2, seq, norm2_g[l], scale2, shift2, gate2, rw_hi, rw_lo, rb, wgu, exp_w_down[l].astype(BF16))
    return x2.reshape(bsz, seq, d)
```

```python
import functools
import math

import jax
import jax.numpy as jnp
import numpy as np
from jax import lax
from jax.experimental import pallas as pl
from jax.experimental.pallas import tpu as pltpu

F32 = jnp.float32
BF16 = jnp.bfloat16
I32 = jnp.int32

D_MODEL = 1024
RMS_EPS = 1e-6
ROPE_THETA = 500000.0
Q_BLOCK = 128
A_HEADS = 8
A_KV_HEADS = 2
A_HEAD_DIM = 64
A_ROT = A_HEAD_DIM // 4
A_REP = A_HEADS // A_KV_HEADS
IDX_HEADS = 8
IDX_DIM = 32
IDX_ROT = IDX_DIM // 4
IDX_TOPK_MAX = 256
SC_W = 512
SC_KERNEL = 3
GM_W = 512
GM_GROUPS = 4
GM_GROUP_W = GM_W // GM_GROUPS
GM_CHUNK = 128
M_DINNER = 512
M_HEAD_DIM = 64
M_HEADS = M_DINNER // M_HEAD_DIM
M_GROUPS = 2
M_STATE = 64
M_CONV = 4
M_CHUNK = 128
M_BC = M_GROUPS * M_STATE
M_XBC = M_DINNER + 2 * M_BC
N_BRANCH = 4
BRANCH_W = 512
N_EXPERTS = 16
N_EXPERT_GROUPS = 4
EXPERTS_PER_GROUP = N_EXPERTS // N_EXPERT_GROUPS
D_FF_EXPERT = 512
A_Q = A_HEADS * A_HEAD_DIM
A_KV = A_KV_HEADS * A_HEAD_DIM
IDX_Q = IDX_HEADS * IDX_DIM

LANES = 128
SUBLANES = 8
VMEM_LIMIT = 56 * 1024 * 1024

P_GATES = 0
P_GUV = 4096
P_Q = 5120
P_SCB = 5632
P_SCC = 6144
P_SCX = 6656
P_MZ = 7168
P_XBC = 7680
P_K = 8448
P_V = 8576
P_COLS = 8704
S_QI = 0
S_KW = 256
S_DT = 384
S_COLS = 512

INT_MIN = -2147483648
NEG_INF = float("-inf")
NEG_INF_KEY = -2139095041


def _dot(a, b):
    return jnp.dot(a, b, preferred_element_type=F32)


def _dot_nt(a, b):
    return lax.dot_general(a, b, (((1,), (1,)), ((), ())), preferred_element_type=F32)


def _dot_f32(a, b):
    return jnp.dot(a, b, preferred_element_type=F32, precision=lax.Precision.HIGHEST)


def _split(x):
    hi = x.astype(BF16)
    lo = (x - hi.astype(F32)).astype(BF16)
    return hi, lo


def _dot3(a_hi, a_lo, b_hi, b_lo):
    return _dot(a_hi, b_hi) + _dot(a_lo, b_hi) + _dot(a_hi, b_lo)


def _sigmoid(x):
    return jax.nn.sigmoid(x)


def _silu(x):
    return x * _sigmoid(x)


def _softplus(x):
    return jnp.maximum(x, 0.0) + jnp.log1p(jnp.exp(-jnp.abs(x)))


def _gelu(x):
    return 0.5 * x * (1.0 + lax.erf(x * np.float32(np.sqrt(0.5))))


def _rms(x, g_row):
    ms = jnp.mean(x * x, axis=-1, keepdims=True)
    return x * lax.rsqrt(ms + RMS_EPS) * g_row


def _params(*sem):
    return pltpu.CompilerParams(dimension_semantics=sem, vmem_limit_bytes=VMEM_LIMIT)


def _mod_kernel(c_ref, w_ref, b_ref, o_ref):
    o_ref[0] = _dot_f32(_silu(c_ref[...]), w_ref[0]) + b_ref[0]


def _modulation(c, ada_w, ada_b):
    depth, d, n = ada_w.shape
    bsz = c.shape[0]
    tn = 768
    return pl.pallas_call(
        _mod_kernel,
        grid=(depth, n // tn),
        in_specs=[pl.BlockSpec((bsz, d), lambda l, j: (0, 0)),
                  pl.BlockSpec((1, d, tn), lambda l, j: (l, 0, j)),
                  pl.BlockSpec((1, 1, tn), lambda l, j: (l, 0, j))],
        out_specs=pl.BlockSpec((1, bsz, tn), lambda l, j: (l, 0, j)),
        out_shape=jax.ShapeDtypeStruct((depth, bsz, n), F32),
        compiler_params=_params("arbitrary", "arbitrary"),
        name="modulation",
    )(c, ada_w, ada_b.reshape(depth, 1, n))


def _inproj_kernel(x_ref, g_ref, sc_ref, sh_ref, w_ref, wsh_ref, wsl_ref, o_ref, os_ref, hh_ref, hl_ref):
    @pl.when(pl.program_id(1) == 0)
    def _():
        h = _rms(x_ref[...], g_ref[...]) * (1.0 + sc_ref[0]) + sh_ref[0]
        hh, hl = _split(h)
        hh_ref[...] = hh
        hl_ref[...] = hl
        os_ref[...] = _dot3(hh, hl, wsh_ref[...], wsl_ref[...])

    o_ref[...] = _dot(hh_ref[...], w_ref[...]).astype(BF16)


def _inproj(x2, seq, g, scale, shift, w_main, ws_hi, ws_lo):
    ntok, d = x2.shape
    tm = min(1024, seq)
    tn = P_COLS // 4
    bsz = scale.shape[0]
    row = lambda i, j: (i, 0)
    per_b = lambda i, j: ((i * tm) // seq, 0, 0)
    return pl.pallas_call(
        _inproj_kernel,
        grid=(ntok // tm, P_COLS // tn),
        in_specs=[pl.BlockSpec((tm, d), row),
                  pl.BlockSpec((1, d), lambda i, j: (0, 0)),
                  pl.BlockSpec((1, 1, d), per_b),
                  pl.BlockSpec((1, 1, d), per_b),
                  pl.BlockSpec((d, tn), lambda i, j: (0, j)),
                  pl.BlockSpec((d, S_COLS), lambda i, j: (0, 0)),
                  pl.BlockSpec((d, S_COLS), lambda i, j: (0, 0))],
        out_specs=[pl.BlockSpec((tm, tn), lambda i, j: (i, j)),
                   pl.BlockSpec((tm, S_COLS), row)],
        out_shape=[jax.ShapeDtypeStruct((ntok, P_COLS), BF16),
                   jax.ShapeDtypeStruct((ntok, S_COLS), F32)],
        scratch_shapes=[pltpu.VMEM((tm, d), BF16), pltpu.VMEM((tm, d), BF16)],
        compiler_params=_params("arbitrary", "arbitrary"),
        name="inproj",
    )(x2, g.reshape(1, d), scale.reshape(bsz, 1, d), shift.reshape(bsz, 1, d), w_main, ws_hi, ws_lo)


ROPE_ZERO_LANE = A_ROT // 2 + IDX_ROT // 2


def _rope_angles(pos_f):
    half_a, half_i = A_ROT // 2, IDX_ROT // 2
    lane = lax.broadcasted_iota(I32, (1, LANES), 1)
    k = jnp.where(lane < half_a, lane, lane - half_a).astype(F32)
    rot = jnp.where(lane < half_a, np.float32(A_ROT), np.float32(IDX_ROT))
    inv_freq = jnp.exp(np.float32(-math.log(ROPE_THETA)) * k * 2.0 / rot)
    ang = pos_f * jnp.where(lane < half_a + half_i, inv_freq, 0.0)
    return jnp.cos(ang), jnp.sin(ang)


def _rope_tables(cos_sin, place_ref, period, rot, limit=LANES):
    half = rot // 2

    def expand(x):
        hi, lo = _split(x)
        return _dot(hi, place_ref[...]) + _dot(lo, place_ref[...])

    lane_all = lax.broadcasted_iota(I32, (1, LANES), 1)
    lane = lane_all % period
    cos = jnp.where(lane_all < limit, expand(cos_sin[0]), 1.0)
    sin = jnp.where(lane_all < limit, expand(cos_sin[1]), 0.0)
    sin_lo = jnp.where(lane < half, -sin, 0.0)
    sin_hi = jnp.where((lane >= half) & (lane < rot), sin, 0.0)
    return cos, sin_lo, sin_hi


def _rope(x, tables, half):
    cos, sin_lo, sin_hi = tables
    width = x.shape[-1]
    return (x * cos + pltpu.roll(x, width - half, 1) * sin_lo + pltpu.roll(x, half, 1) * sin_hi)


def _prep_kernel(q_ref, k_ref, v_ref, qi_ref, kw_ref, pos_ref, qg_ref, kg_ref, bdq_ref, bdk_ref,
                 pq_ref, pk_ref, pqi1_ref, pqi2_ref, pki1_ref, pki2_ref, pra_ref, pri_ref,
                 qo_ref, ko_ref, vto_ref, qio_ref, kio_ref):
    cos_sin = _rope_angles(pos_ref[...].astype(F32))
    att_tab = _rope_tables(cos_sin, pra_ref, A_HEAD_DIM, A_ROT)
    idx_tab = _rope_tables(cos_sin, pri_ref, IDX_DIM, IDX_ROT)
    kw_tab = _rope_tables(cos_sin, pri_ref, IDX_DIM, IDX_ROT, limit=IDX_DIM)

    def head_norm(x, bd_ref, g_ref):
        sq_hi, sq_lo = _split(x * x)
        ms = (_dot(sq_hi, bd_ref[...]) + _dot(sq_lo, bd_ref[...])) * np.float32(1.0 / A_HEAD_DIM)
        return x * lax.rsqrt(ms + RMS_EPS) * g_ref[...]

    def rope_wide(x, tab, half):
        pieces = [_rope(x[:, c * LANES:(c + 1) * LANES], tab, half) for c in range(x.shape[-1] // LANES)]
        return pieces[0] if len(pieces) == 1 else jnp.concatenate(pieces, axis=-1)

    def store_head_rows(o_ref, x):
        for blk in range(x.shape[0] // Q_BLOCK):
            for h in range(x.shape[1] // LANES):
                r0 = (blk * (x.shape[1] // LANES) + h) * Q_BLOCK
                o_ref[r0:r0 + Q_BLOCK, :] = x[blk * Q_BLOCK:(blk + 1) * Q_BLOCK, h * LANES:(h + 1) * LANES]

    q = rope_wide(head_norm(q_ref[...].astype(F32), bdq_ref, qg_ref), att_tab, A_ROT // 2)
    q = (q * np.float32(A_HEAD_DIM ** -0.5)).astype(BF16)
    store_head_rows(qo_ref, _dot(q, pq_ref[...]).astype(BF16))
    k = rope_wide(head_norm(k_ref[...].astype(F32), bdk_ref, kg_ref), att_tab, A_ROT // 2).astype(BF16)
    ko_ref[...] = _dot(k, pk_ref[...]).astype(BF16)
    v_pad = _dot(v_ref[...], pk_ref[...])
    ones_lane = (lax.broadcasted_iota(I32, (1, A_KV_HEADS * LANES), 1) % LANES) == A_HEAD_DIM
    vto_ref[...] = jnp.transpose(jnp.where(ones_lane, 1.0, v_pad)).astype(BF16)

    qi_hi, qi_lo = _split(rope_wide(qi_ref[...], idx_tab, IDX_ROT // 2))
    store_head_rows(qio_ref, (_dot(qi_hi, pqi1_ref[...]) + _dot(qi_lo, pqi2_ref[...])).astype(BF16))
    ki_hi, ki_lo = _split(_rope(kw_ref[...], kw_tab, IDX_ROT // 2))
    kio_ref[...] = (_dot(ki_hi, pki1_ref[...]) + _dot(ki_lo, pki2_ref[...])).astype(BF16)


@functools.lru_cache(maxsize=None)
def _prep_constants():
    def block_diag(width):
        i = np.arange(width)
        return (i[:, None] // A_HEAD_DIM == i[None, :] // A_HEAD_DIM).astype(np.float32)

    def pad_heads(n_heads):
        p = np.zeros((n_heads * A_HEAD_DIM, n_heads * LANES), np.float32)
        for h in range(n_heads):
            for d in range(A_HEAD_DIM):
                p[h * A_HEAD_DIM + d, h * LANES + d] = 1.0
        return p

    pqi1 = np.zeros((IDX_Q, IDX_HEADS * LANES), np.float32)
    pqi2 = np.zeros((IDX_Q, IDX_HEADS * LANES), np.float32)
    for h in range(IDX_HEADS):
        for d in range(IDX_DIM):
            pqi1[h * IDX_DIM + d, h * LANES + d] = 1.0
            pqi1[h * IDX_DIM + d, h * LANES + IDX_DIM + d] = 1.0
            pqi2[h * IDX_DIM + d, h * LANES + 2 * IDX_DIM + d] = 1.0
    pki1 = np.zeros((LANES, LANES), np.float32)
    pki2 = np.zeros((LANES, LANES), np.float32)
    for d in range(IDX_DIM):
        pki1[d, d] = 1.0
        pki1[d, 2 * IDX_DIM + d] = 1.0
        pki2[d, IDX_DIM + d] = 1.0
    def rope_place(period, rot, first_src):
        p = np.zeros((LANES, LANES), np.float32)
        for j in range(LANES):
            jj = j % period
            p[first_src + jj % (rot // 2) if jj < rot else ROPE_ZERO_LANE, j] = 1.0
        return p

    mats = (block_diag(A_Q), block_diag(A_KV), pad_heads(A_HEADS), pad_heads(A_KV_HEADS), pqi1, pqi2, pki1, pki2,
            rope_place(A_HEAD_DIM, A_ROT, 0), rope_place(IDX_DIM, IDX_ROT, A_ROT // 2))
    return tuple(np.asarray(m) for m in mats)


def _prep(proj, small, pos2, seq, q_g, k_g):
    ntok = proj.shape[0]
    tm = min(512, seq)
    consts = [jnp.asarray(m, BF16) for m in _prep_constants()]
    row = lambda w, c: pl.BlockSpec((tm, w), lambda i: (i, c))
    full = lambda a: pl.BlockSpec(a.shape, lambda i: (0, 0))
    qg = jnp.tile(q_g, A_HEADS).reshape(1, A_Q)
    kg = jnp.tile(k_g, A_KV_HEADS).reshape(1, A_KV)
    return pl.pallas_call(
        _prep_kernel,
        grid=(ntok // tm,),
        in_specs=[row(A_Q, P_Q // A_Q), row(A_KV, P_K // A_KV), row(A_KV, P_V // A_KV), row(IDX_Q, S_QI // IDX_Q),
                  row(LANES, S_KW // LANES), row(1, 0), full(qg), full(kg)] + [full(m) for m in consts],
        out_specs=[pl.BlockSpec((tm * A_HEADS, LANES), lambda i: (i, 0)), row(A_KV_HEADS * LANES, 0),
                   pl.BlockSpec((A_KV_HEADS * LANES, tm), lambda i: (0, i)),
                   pl.BlockSpec((tm * IDX_HEADS, LANES), lambda i: (i, 0)), row(LANES, 0)],
        out_shape=[jax.ShapeDtypeStruct((ntok * A_HEADS, LANES), BF16),
                   jax.ShapeDtypeStruct((ntok, A_KV_HEADS * LANES), BF16),
                   jax.ShapeDtypeStruct((A_KV_HEADS * LANES, ntok), BF16),
                   jax.ShapeDtypeStruct((ntok * IDX_HEADS, LANES), BF16),
                   jax.ShapeDtypeStruct((ntok, LANES), BF16)],
        compiler_params=_params("arbitrary"),
        name="attn_prep",
    )(proj, proj, proj, small, small, pos2, qg, kg, *consts)


def _sortable(x):
    bits = pltpu.bitcast(jnp.where(x == 0.0, 0.0, x), I32)
    return jnp.where(bits < 0, bits ^ jnp.int32(0x7FFFFFFF), bits)


KEY_STEP = 512
KEY_SUBS = KEY_STEP // Q_BLOCK


def _key_row0(step, sub):
    return pl.multiple_of(step * KEY_STEP + sub * Q_BLOCK, Q_BLOCK)


MXU_LOOP_WIDTHS = (4, 2, 1)


def _loop_steps(nstep, body, init, widths=(2, 1)):
    carry, start = init, jnp.int32(0)
    for w in widths:
        count = lax.div(nstep - start, jnp.int32(w))

        def multi(i, c, w=w, start=start):
            for k in range(w):
                c = body(start + i * w + k, c)
            return c
        carry = lax.fori_loop(0, count, multi, carry)
        start = start + count * w
    return carry


def _count_rows(pred_fn, nstep):
    def body(c, acc):
        for sub in range(KEY_SUBS):
            ones = jnp.where(pred_fn(_key_row0(c, sub)), jnp.int32(1), jnp.int32(0))
            acc = acc + jnp.sum(ones.reshape(Q_BLOCK // SUBLANES, SUBLANES, LANES), axis=0)
        return acc
    acc = _loop_steps(nstep, body, jnp.zeros((SUBLANES, LANES), I32), MXU_LOOP_WIDTHS)
    return jnp.sum(acc, axis=0, keepdims=True)


PAIR = 2 * LANES
N_PAIRS = A_HEADS // 2


def _dsa_kernel(qi_ref, kw_ref, q_ref, ki_ref, k_ref, vt_ref, o_ref, key_ref, s_ref, lim_ref, oacc_ref,
                *, topk, idx_bits):
    qb = pl.program_id(1)
    nstep = lax.div(qb, jnp.int32(KEY_SUBS)) + 1
    row_i = lax.broadcasted_iota(I32, (Q_BLOCK, LANES), 0)
    q_pos = qb * Q_BLOCK + lax.broadcasted_iota(I32, (Q_BLOCK, LANES), 1)

    w_rows = jnp.transpose(kw_ref[...])[IDX_DIM:IDX_DIM + IDX_HEADS, :]
    w_rows = w_rows * np.float32(IDX_HEADS ** -0.5) * np.float32(IDX_DIM ** -0.5)

    def rows(r0):
        return pl.ds(r0, Q_BLOCK)

    def score_body(c, carry):
        for sub in range(KEY_SUBS):
            r0 = _key_row0(c, sub)
            ki = ki_ref[rows(r0), :]
            s = jnp.zeros((Q_BLOCK, LANES), F32)
            for pair in range(IDX_HEADS // 2):
                d = _dot_nt(ki, qi_ref[pair * PAIR:(pair + 1) * PAIR, :])
                for j in range(2):
                    h = 2 * pair + j
                    s = s + jnp.maximum(d[:, j * LANES:(j + 1) * LANES], 0.0) * w_rows[h:h + 1, :]
            key_ref[rows(r0), :] = _sortable(jnp.where((r0 + row_i) <= q_pos, s, NEG_INF))
        return carry
    _loop_steps(nstep, score_body, 0, MXU_LOOP_WIDTHS)

    def bit_body(i, cand):
        trial = cand | lax.shift_left(jnp.int32(1), 31 - i)
        trial_s = trial ^ jnp.int32(INT_MIN)
        cnt = _count_rows(lambda r0: key_ref[rows(r0), :] >= trial_s, nstep)
        return jnp.where(cnt >= topk, trial, cand)
    cand = lax.fori_loop(0, 32, bit_body, jnp.zeros((1, LANES), I32))
    thr = cand ^ jnp.int32(INT_MIN)

    n_gt = _count_rows(lambda r0: key_ref[rows(r0), :] > thr, nstep)
    n_eq = _count_rows(lambda r0: key_ref[rows(r0), :] == thr, nstep)
    need = topk - n_gt
    excess = (n_eq > need) & (thr > jnp.int32(NEG_INF_KEY))
    lim_ref[0:1, :] = jnp.full((1, LANES), 2 ** 30, I32)

    @pl.when(jnp.max(jnp.where(excess, 1, 0)) > 0)
    def _():
        def idx_body(i, lim):
            trial = lim | lax.shift_left(jnp.int32(1), idx_bits - 1 - i)
            cnt = _count_rows(lambda r0: (key_ref[rows(r0), :] == thr) & ((r0 + row_i) < trial), nstep)
            return jnp.where(cnt < need, trial, lim)
        lim = lax.fori_loop(0, idx_bits, idx_body, jnp.zeros((1, LANES), I32))
        lim_ref[0:1, :] = jnp.where(excess, lim, 2 ** 30)
    tie_lim = lim_ref[0:1, :]

    def max_body(c, m_all):
        m_all = list(m_all)
        for sub in range(KEY_SUBS):
            r0 = _key_row0(c, sub)
            keys = key_ref[rows(r0), :]
            kpos = r0 + row_i
            sel = ((keys > thr) | ((keys == thr) & (kpos <= tie_lim))) & (kpos <= q_pos)
            bias = jnp.where(sel, 0.0, NEG_INF)
            bias2 = jnp.concatenate([bias, bias], axis=-1)
            for pair in range(N_PAIRS):
                g = (2 * pair) // A_REP
                s = _dot_nt(k_ref[rows(r0), g * LANES:(g + 1) * LANES], q_ref[pair * PAIR:(pair + 1) * PAIR, :]) + bias2
                s_ref[rows(r0), pair * PAIR:(pair + 1) * PAIR] = s
                m_all[pair] = jnp.maximum(m_all[pair], jnp.max(s, axis=0, keepdims=True))
        return tuple(m_all)
    m_fin = _loop_steps(nstep, max_body, tuple(jnp.full((1, PAIR), -1e30, F32) for _ in range(N_PAIRS)),
                        MXU_LOOP_WIDTHS)

    oacc_ref[...] = jnp.zeros(oacc_ref.shape, F32)

    def pv_body(c, carry):
        cols = pl.ds(pl.multiple_of(c * KEY_STEP, KEY_STEP), KEY_STEP)
        for pair in range(N_PAIRS):
            g = (2 * pair) // A_REP
            p = jnp.exp(s_ref[cols, pair * PAIR:(pair + 1) * PAIR] - m_fin[pair]).astype(BF16)
            oacc_ref[pair] += _dot(vt_ref[g * LANES:(g + 1) * LANES, cols], p)
        return carry
    _loop_steps(nstep, pv_body, 0, MXU_LOOP_WIDTHS)

    outs = []
    for pair in range(N_PAIRS):
        o_p = oacc_ref[pair]
        for j in range(2):
            o_h = o_p[:, j * LANES:(j + 1) * LANES]
            outs.append(o_h[0:A_HEAD_DIM, :] / o_h[A_HEAD_DIM:A_HEAD_DIM + 1, :])
    o_ref[...] = jnp.transpose(jnp.concatenate(outs, axis=0)).astype(BF16)


def _dsa(qi_cat, small, q_pad, ki_cat, k_pad, v_t, bsz, seq):
    assert seq % KEY_STEP == 0
    nb = seq // Q_BLOCK
    topk = min(IDX_TOPK_MAX, seq // 4)
    idx_bits = max(1, int(math.ceil(math.log2(seq))))
    qrow = lambda w, c: pl.BlockSpec((Q_BLOCK, w), lambda b, j: (b * nb + j, c))
    brow = lambda w: pl.BlockSpec((seq, w), lambda b, j: (b, 0))
    return pl.pallas_call(
        functools.partial(_dsa_kernel, topk=topk, idx_bits=idx_bits),
        grid=(bsz, nb),
        in_specs=[pl.BlockSpec((IDX_HEADS * Q_BLOCK, LANES), lambda b, j: (b * nb + j, 0)),
                  qrow(LANES, S_KW // LANES),
                  pl.BlockSpec((A_HEADS * Q_BLOCK, LANES), lambda b, j: (b * nb + j, 0)),
                  brow(LANES), brow(A_KV_HEADS * LANES),
                  pl.BlockSpec((A_KV_HEADS * LANES, seq), lambda b, j: (0, b))],
        out_specs=qrow(A_Q, 0),
        out_shape=jax.ShapeDtypeStruct((bsz * seq, A_Q), BF16),
        scratch_shapes=[pltpu.VMEM((seq, LANES), I32), pltpu.VMEM((seq, A_HEADS * LANES), F32),
                        pltpu.VMEM((SUBLANES, LANES), I32), pltpu.VMEM((N_PAIRS, LANES, PAIR), F32)],
        compiler_params=_params("arbitrary", "arbitrary"),
        name="sparse_attention",
    )(qi_cat, small, q_pad, ki_cat, k_pad, v_t)


HALO = SUBLANES


def _causal_conv(ext_ref, u, w_ref, taps, rows):
    ext_ref[HALO:HALO + rows, :] = u
    y = u * w_ref[taps - 1:taps, :]
    for j in range(taps - 1):
        shift = taps - 1 - j
        y = y + ext_ref[HALO - shift:HALO - shift + rows, :] * w_ref[j:j + 1, :]
    ext_ref[0:HALO, :] = ext_ref[rows:rows + HALO, :]
    return y


def _mix_kernel(x_ref, g0_ref, g1_ref, g2_ref, g3_ref, guv_ref, scb_ref, scc_ref, scx_ref, mz_ref, xbc_ref,
                dt_ref, oa_ref, gate_ref, scw_ref, gvg_ref, ws_ref, wsb_ref, mcw_ref, mcb_ref, dtb_ref, alog_ref,
                md_ref, mng_ref, wb_ref, wo_ref, o_ref, sc_ext, xbc_ext, state_ref, y_ref, *, rows):
    @pl.when(pl.program_id(1) == 0)
    def _():
        sc_ext[0:HALO, :] = jnp.zeros((HALO, SC_W), F32)
        xbc_ext[0:HALO, :] = jnp.zeros((HALO, M_XBC), F32)
        state_ref[...] = jnp.zeros(state_ref.shape, F32)

    nchunk = rows // M_CHUNK
    row_i = lax.broadcasted_iota(I32, (M_CHUNK, M_CHUNK), 0)
    col_i = lax.broadcasted_iota(I32, (M_CHUNK, M_CHUNK), 1)
    tri = row_i >= col_i

    o_b = scb_ref[...].astype(F32) * _causal_conv(
        sc_ext, scc_ref[...].astype(F32) * scx_ref[...].astype(F32), scw_ref, SC_KERNEL, rows)

    guv = _gelu(guv_ref[...].astype(F32))
    gu = guv[:, :GM_W]
    gv = _rms(guv[:, GM_W:], gvg_ref[...]).astype(BF16)
    mixed_rows = []
    for ci in range(nchunk):
        r0 = ci * GM_CHUNK
        cols = []
        for gi in range(GM_GROUPS):
            vb = gv[r0:r0 + GM_CHUNK, gi * GM_GROUP_W:(gi + 1) * GM_GROUP_W]
            cols.append(_dot(ws_ref[gi], vb) + wsb_ref[:, gi:gi + 1])
        mixed_rows.append(jnp.concatenate(cols, axis=-1))
    o_c = gu * jnp.concatenate(mixed_rows, axis=0)

    xbc = _silu(_causal_conv(xbc_ext, xbc_ref[...].astype(F32), mcw_ref, M_CONV, rows) + mcb_ref[...])
    head_lane = lax.broadcasted_iota(I32, (1, LANES), 1) < M_HEADS
    dt = jnp.where(head_lane, _softplus(dt_ref[...] + dtb_ref[...]), 0.0)
    a_row = jnp.where(head_lane, -jnp.exp(alog_ref[...]), 0.0)
    tri_f = jnp.where(tri, 1.0, 0.0).astype(F32)
    for ci in range(nchunk):
        r0 = ci * M_CHUNK
        dtc = dt[r0:r0 + M_CHUNK, :]
        acs = _dot_f32(tri_f, dtc * a_row)
        acs_t = jnp.transpose(acs)
        b_t = jnp.transpose(xbc[r0:r0 + M_CHUNK, M_DINNER:M_DINNER + M_BC]).astype(BF16)
        cm = xbc[r0:r0 + M_CHUNK, M_DINNER + M_BC:M_XBC].astype(BF16)
        for g in range(M_GROUPS):
            bg_t = b_t[g * M_STATE:(g + 1) * M_STATE, :]
            cg = cm[:, g * M_STATE:(g + 1) * M_STATE]
            cb = _dot(cg, bg_t)
            for h in range(g * (M_HEADS // M_GROUPS), (g + 1) * (M_HEADS // M_GROUPS)):
                col = acs[:, h:h + 1]
                row = acs_t[h:h + 1, :]
                last = acs[M_CHUNK - 1:M_CHUNK, h:h + 1]
                decay = jnp.exp(jnp.where(tri, col - row, NEG_INF))
                xs_h = xbc[r0:r0 + M_CHUNK, h * M_HEAD_DIM:(h + 1) * M_HEAD_DIM]
                xdt = xs_h * dtc[:, h:h + 1]
                y = _dot((cb * decay).astype(BF16), xdt.astype(BF16))
                prev_t = state_ref[h]
                y = y + _dot(cg, prev_t.astype(BF16)) * jnp.exp(col)
                st_t = _dot(bg_t, (xdt * jnp.exp(last - col)).astype(BF16))
                state_ref[h] = prev_t * jnp.exp(last) + st_t
                y_ref[r0:r0 + M_CHUNK, h * M_HEAD_DIM:(h + 1) * M_HEAD_DIM] = y + md_ref[:, h:h + 1] * xs_h
    o_d = _rms(y_ref[...] * _silu(mz_ref[...].astype(F32)), mng_ref[...])

    branches = (oa_ref[...], o_b.astype(BF16), o_c.astype(BF16), o_d.astype(BF16))
    gate_refs = (g0_ref, g1_ref, g2_ref, g3_ref)
    merged = None
    for i in range(N_BRANCH):
        term = _sigmoid(gate_refs[i][...].astype(F32)) * _dot(branches[i], wb_ref[i])
        merged = term if merged is None else merged + term
    o_ref[...] = x_ref[...] + gate_ref[0] * _dot(merged.astype(BF16), wo_ref[...])


def _mix(x2, proj, small, o_a, gate1, lw, bsz, seq):
    d = x2.shape[1]
    rows = min(256, seq)
    ns = seq // rows
    tok = lambda w, c: pl.BlockSpec((rows, w), lambda b, s: (b * ns + s, c))
    full = lambda a: pl.BlockSpec(a.shape, lambda b, s: (0,) * a.ndim)
    weights = [lw["sc_conv_w"], lw["g_v_norm_g"], lw["ws"], lw["ws_b"], lw["m_conv_w"], lw["m_conv_b"],
               lw["m_dt_bias"], lw["m_a_log"], lw["m_d"], lw["m_norm_g"], lw["w_branch"], lw["w_out"]]
    return pl.pallas_call(
        functools.partial(_mix_kernel, rows=rows),
        grid=(bsz, ns),
        in_specs=[tok(d, 0)] + [tok(D_MODEL, i) for i in range(N_BRANCH)]
        + [tok(2 * GM_W, P_GUV // (2 * GM_W)), tok(SC_W, P_SCB // SC_W), tok(SC_W, P_SCC // SC_W),
           tok(SC_W, P_SCX // SC_W), tok(M_DINNER, P_MZ // M_DINNER), tok(M_XBC, P_XBC // M_XBC),
           tok(LANES, S_DT // LANES), tok(A_Q, 0),
           pl.BlockSpec((1, 1, d), lambda b, s: (b, 0, 0))]
        + [full(w) for w in weights],
        out_specs=tok(d, 0),
        out_shape=jax.ShapeDtypeStruct(x2.shape, F32),
        scratch_shapes=[pltpu.VMEM((HALO + rows, SC_W), F32), pltpu.VMEM((HALO + rows, M_XBC), F32),
                        pltpu.VMEM((M_HEADS, M_STATE, M_HEAD_DIM), F32), pltpu.VMEM((rows, M_DINNER), F32)],
        compiler_params=_params("arbitrary", "arbitrary"),
        name="mixers_merge",
    )(x2, proj, proj, proj, proj, proj, proj, proj, proj, proj, proj, small, o_a,
      gate1.reshape(bsz, 1, d), *weights)


ROUTE_ROWS = EXPERTS_PER_GROUP * SUBLANES
BEST_ROW = ROUTE_ROWS


def _route_row(e):
    return (e % EXPERTS_PER_GROUP) * SUBLANES + e // EXPERTS_PER_GROUP


def _route(scores_t, bias_col):
    ntok = scores_t.shape[1]
    row = lax.broadcasted_iota(I32, (SUBLANES, ntok), 0)
    members = range(EXPERTS_PER_GROUP)
    sc = [scores_t[i * SUBLANES:(i + 1) * SUBLANES, :] for i in members]
    x = [sc[i] + bias_col[i * SUBLANES:(i + 1) * SUBLANES, :] for i in members]
    hi1, lo1 = jnp.maximum(x[0], x[1]), jnp.minimum(x[0], x[1])
    hi2, lo2 = jnp.maximum(x[2], x[3]), jnp.minimum(x[2], x[3])
    grp = jnp.maximum(hi1, hi2) + jnp.maximum(jnp.minimum(hi1, hi2), jnp.maximum(lo1, lo2))
    grp = jnp.where(row < N_EXPERT_GROUPS, grp, NEG_INF)
    best_val = jnp.max(grp, axis=0, keepdims=True)
    best = jnp.min(jnp.where(grp == best_val, row, SUBLANES), axis=0, keepdims=True)
    in_best = row == best
    y = [jnp.max(jnp.where(in_best, x[i], NEG_INF), axis=0, keepdims=True) for i in members]
    s = [jnp.sum(jnp.where(in_best, sc[i], 0.0), axis=0, keepdims=True) for i in members]

    def first_argmax(vals):
        top = functools.reduce(jnp.maximum, vals)
        idx = jnp.full(top.shape, len(vals) - 1, I32)
        for i in reversed(range(len(vals) - 1)):
            idx = jnp.where(vals[i] == top, i, idx)
        return idx

    i1 = first_argmax(y)
    i2 = first_argmax([jnp.where(i1 == i, NEG_INF, y[i]) for i in members])
    w1 = sum(jnp.where(i1 == i, s[i], 0.0) for i in members)
    w2 = sum(jnp.where(i2 == i, s[i], 0.0) for i in members)
    tot = w1 + w2
    parts = [jnp.where(in_best, jnp.where(i1 == i, w1 / tot, 0.0) + jnp.where(i2 == i, w2 / tot, 0.0), 0.0)
             for i in members]
    parts.append(jnp.broadcast_to(best.astype(F32), (SUBLANES, ntok)))
    parts.append(jnp.zeros((LANES - (ROUTE_ROWS + SUBLANES), ntok), F32))
    return jnp.concatenate(parts, axis=0)


MOE_CHUNK = 128


def _dot_tn(a, b):
    return lax.dot_general(a, b, (((0,), (0,)), ((), ())), preferred_element_type=F32)


def _moe_kernel(x_ref, g_ref, sc_ref, sh_ref, gate_ref, rwh_ref, rwl_ref, rb_ref, tri_ref, wgu_ref, wd_ref, o_ref,
                h_ref, c2_ref, grp_ref, rank_ref, tot_ref):
    grp_id = pl.program_id(1)
    tm = x_ref.shape[0]
    lane = lax.broadcasted_iota(I32, (tm, LANES), 1)

    @pl.when(grp_id == 0)
    def _():
        h = _rms(x_ref[...], g_ref[...]) * (1.0 + sc_ref[0]) + sh_ref[0]
        hh, hl = _split(h)
        h_ref[...] = hh
        logits_t = (_dot_nt(rwh_ref[...], hh) + _dot_nt(rwh_ref[...], hl) + _dot_nt(rwl_ref[...], hh))
        routed = jnp.transpose(_route(_sigmoid(logits_t[:ROUTE_ROWS, :]), rb_ref[...]))
        best = routed[:, BEST_ROW:BEST_ROW + 1].astype(I32)
        comb = jnp.where(lane < ROUTE_ROWS, routed, 0.0)
        c_hi, c_lo = _split(comb)
        c2_ref[:, :LANES] = c_hi
        c2_ref[:, LANES:] = c_lo
        grp_ref[...] = jnp.broadcast_to(best, (tm, LANES))
        onehot = jnp.where(lane == best, 1.0, 0.0)
        offs = jnp.zeros((1, LANES), F32)
        for b in range(tm // MOE_CHUNK):
            blk = onehot[b * MOE_CHUNK:(b + 1) * MOE_CHUNK, :]
            ranks = _dot(tri_ref[...], blk.astype(BF16)) + offs
            own = jnp.sum(blk * ranks, axis=-1, keepdims=True)
            rank_ref[b * MOE_CHUNK:(b + 1) * MOE_CHUNK, :] = jnp.broadcast_to(own, (MOE_CHUNK, LANES))
            offs = offs + jnp.sum(blk, axis=0, keepdims=True)
        tot_ref[0:1, :] = offs
        o_ref[...] = jnp.zeros(o_ref.shape, F32)

    lane1 = lax.broadcasted_iota(I32, (1, LANES), 1)
    n_tok = jnp.sum(jnp.where(lane1 == grp_id, tot_ref[0:1, :], 0.0)).astype(I32)
    n_chunks = lax.div(n_tok + (MOE_CHUNK - 1), jnp.int32(MOE_CHUNK))
    lane_f = lane.astype(F32)
    lane_c = lax.broadcasted_iota(I32, (MOE_CHUNK, LANES), 1)

    def chunk_body(c, carry):
        base = (c * MOE_CHUNK).astype(F32)
        pick = ((rank_ref[...] - base) == lane_f) & (grp_ref[...] == grp_id)
        st = jnp.where(pick, 1.0, 0.0).astype(BF16)
        xc = _dot_tn(st, h_ref[...]).astype(BF16)
        wc2 = _dot_tn(st, c2_ref[...])
        wc = wc2[:, :LANES] + wc2[:, LANES:]
        y = jnp.zeros((MOE_CHUNK, x_ref.shape[1]), F32)
        for el in range(EXPERTS_PER_GROUP):
            gu = _dot(xc, wgu_ref[el])
            hid = _silu(gu[:, :D_FF_EXPERT]) * gu[:, D_FF_EXPERT:]
            w_e = jnp.sum(jnp.where(lane_c == el * SUBLANES + grp_id, wc, 0.0), axis=-1, keepdims=True)
            y = y + w_e * _dot(hid.astype(BF16), wd_ref[el])
        y_hi, y_lo = _split(y)
        o_ref[...] += _dot(jnp.concatenate([st, st], axis=1), jnp.concatenate([y_hi, y_lo], axis=0))
        return carry
    _loop_steps(n_chunks, chunk_body, 0)

    @pl.when(grp_id == pl.num_programs(1) - 1)
    def _():
        o_ref[...] = x_ref[...] + gate_ref[0] * o_ref[...]


def _moe(x2, seq, g, scale, shift, gate, rw_hi, rw_lo, rb, wgu, wd):
    ntok, d = x2.shape
    bsz = scale.shape[0]
    tm = min(1024, seq)
    row = lambda i, e: (i, 0)
    per_b = lambda i, e: ((i * tm) // seq, 0, 0)
    vec = pl.BlockSpec((1, 1, d), per_b)
    const2 = lambda a: pl.BlockSpec(a.shape, lambda i, e: (0, 0))
    tri = jnp.asarray(np.tril(np.ones((MOE_CHUNK, MOE_CHUNK), np.float32), -1), BF16)
    return pl.pallas_call(
        _moe_kernel,
        grid=(ntok // tm, N_EXPERT_GROUPS),
        in_specs=[pl.BlockSpec((tm, d), row), pl.BlockSpec((1, d), lambda i, e: (0, 0)), vec, vec, vec,
                  const2(rw_hi), const2(rw_lo), const2(rb), const2(tri),
                  pl.BlockSpec((EXPERTS_PER_GROUP, d, 2 * D_FF_EXPERT), lambda i, e: (e, 0, 0)),
                  pl.BlockSpec((EXPERTS_PER_GROUP, D_FF_EXPERT, d), lambda i, e: (e, 0, 0))],
        out_specs=pl.BlockSpec((tm, d), row),
        out_shape=jax.ShapeDtypeStruct(x2.shape, F32),
        scratch_shapes=[pltpu.VMEM((tm, d), BF16), pltpu.VMEM((tm, 2 * LANES), BF16),
                        pltpu.VMEM((tm, LANES), I32), pltpu.VMEM((tm, LANES), F32), pltpu.VMEM((SUBLANES, LANES), F32)],
        compiler_params=_params("arbitrary", "arbitrary"),
        name="moe",
    )(x2, g.reshape(1, d), scale.reshape(bsz, 1, d), shift.reshape(bsz, 1, d), gate.reshape(bsz, 1, d),
      rw_hi, rw_lo, rb, tri, wgu, wd)


def _pad_lanes(v, width=LANES):
    v = v.reshape(1, -1)
    return jnp.pad(v, ((0, 0), (0, width - v.shape[1])))


def _layer_weights(w_in, sc_conv_w, g_v_norm_g, g_spatial_w, g_spatial_b, m_conv_w, m_conv_b, m_dt_bias,
                   m_a_log, m_d, m_norm_g, w_branch, w_out):
    splits = (A_Q, A_KV, A_KV, IDX_Q, IDX_DIM, IDX_HEADS, SC_W, SC_W, SC_W, 2 * GM_W, M_DINNER, M_XBC, M_HEADS,
              N_BRANCH * D_MODEL)
    pts = [int(p) for p in np.cumsum(splits)[:-1]]
    (q, k, v, qi, ki, wi, sc_b, sc_c, sc_x, g_uv, m_z, m_xbc, m_dt, gates) = jnp.split(w_in, pts, axis=-1)
    d = w_in.shape[0]
    w_main = jnp.concatenate([gates, g_uv, q, sc_b, sc_c, sc_x, m_z, m_xbc, k, v], axis=-1).astype(BF16)
    w_small = jnp.concatenate(
        [qi, ki, wi, jnp.zeros((d, LANES - IDX_DIM - IDX_HEADS), F32), m_dt, jnp.zeros((d, LANES - M_HEADS), F32)],
        axis=-1)
    ws_hi = w_small.astype(BF16)
    ws_lo = (w_small - ws_hi.astype(F32)).astype(BF16)
    tri = np.tril(np.ones((GM_CHUNK, GM_CHUNK), dtype=bool))
    return dict(
        w_main=w_main, ws_hi=ws_hi, ws_lo=ws_lo,
        sc_conv_w=sc_conv_w, g_v_norm_g=g_v_norm_g.reshape(1, GM_W),
        ws=jnp.where(tri[None], g_spatial_w, 0.0).astype(BF16), ws_b=g_spatial_b.T,
        m_conv_w=m_conv_w, m_conv_b=m_conv_b.reshape(1, M_XBC),
        m_dt_bias=_pad_lanes(m_dt_bias), m_a_log=_pad_lanes(m_a_log), m_d=_pad_lanes(m_d),
        m_norm_g=m_norm_g.reshape(1, M_DINNER), w_branch=w_branch.astype(BF16), w_out=w_out.astype(BF16))


def kernel(x, c, positions, ada_w, ada_b, norm1_g, w_in, q_norm_g, k_norm_g, sc_conv_w, g_v_norm_g, g_spatial_w,
           g_spatial_b, m_conv_w, m_conv_b, m_dt_bias, m_a_log, m_d, m_norm_g, w_branch, w_out, norm2_g, router_w,
           router_bias, exp_w_gate, exp_w_up, exp_w_down):
    bsz, seq, d = x.shape
    depth = ada_w.shape[0]
    mod = _modulation(c, ada_w, ada_b)
    route_rows = np.array([_route_row(e) for e in range(N_EXPERTS)])
    rw = jnp.zeros((LANES, d), F32).at[route_rows].set(router_w.T)
    rw_hi = rw.astype(BF16)
    rw_lo = (rw - rw_hi.astype(F32)).astype(BF16)
    rb = jnp.zeros((ROUTE_ROWS, 1), F32).at[route_rows, 0].set(router_bias)
    pos2 = positions.reshape(bsz * seq, 1)
    x2 = x.reshape(bsz * seq, d)
    for l in range(depth):
        shift1, scale1, gate1, shift2, scale2, gate2 = jnp.split(mod[l], 6, axis=-1)
        lw = _layer_weights(w_in[l], sc_conv_w[l], g_v_norm_g[l], g_spatial_w[l], g_spatial_b[l], m_conv_w[l],
                            m_conv_b[l], m_dt_bias[l], m_a_log[l], m_d[l], m_norm_g[l], w_branch[l], w_out[l])
        proj, small = _inproj(x2, seq, norm1_g[l], scale1, shift1, lw["w_main"], lw["ws_hi"], lw["ws_lo"])
        q_pad, k_pad, v_t, qi_cat, ki_cat = _prep(proj, small, pos2, seq, q_norm_g[l], k_norm_g[l])
        o_a = _dsa(qi_cat, small, q_pad, ki_cat, k_pad, v_t, bsz, seq)
        x2 = _mix(x2, proj, small, o_a, gate1, lw, bsz, seq)
        wgu = jnp.concatenate([exp_w_gate[l], exp_w_up[l]], axis=-1).astype(BF16)
        x2 = _moe(x2, seq, norm2_g[l], scale2, shift2, gate2, rw_hi, rw_lo, rb, wgu, exp_w_down[l].astype(BF16))
    return x2.reshape(bsz, seq, d)
```

```python
import functools
import math

import jax
import jax.numpy as jnp
import numpy as np
from jax import lax
from jax.experimental import pallas as pl
from jax.experimental.pallas import tpu as pltpu

F32 = jnp.float32
BF16 = jnp.bfloat16
I32 = jnp.int32

D_MODEL = 1024
RMS_EPS = 1e-6
ROPE_THETA = 500000.0
Q_BLOCK = 128
A_HEADS = 8
A_KV_HEADS = 2
A_HEAD_DIM = 64
A_ROT = A_HEAD_DIM // 4
A_REP = A_HEADS // A_KV_HEADS
IDX_HEADS = 8
IDX_DIM = 32
IDX_ROT = IDX_DIM // 4
IDX_TOPK_MAX = 256
SC_W = 512
SC_KERNEL = 3
GM_W = 512
GM_GROUPS = 4
GM_GROUP_W = GM_W // GM_GROUPS
GM_CHUNK = 128
M_DINNER = 512
M_HEAD_DIM = 64
M_HEADS = M_DINNER // M_HEAD_DIM
M_GROUPS = 2
M_STATE = 64
M_CONV = 4
M_CHUNK = 128
M_BC = M_GROUPS * M_STATE
M_XBC = M_DINNER + 2 * M_BC
N_BRANCH = 4
BRANCH_W = 512
N_EXPERTS = 16
N_EXPERT_GROUPS = 4
EXPERTS_PER_GROUP = N_EXPERTS // N_EXPERT_GROUPS
D_FF_EXPERT = 512
A_Q = A_HEADS * A_HEAD_DIM
A_KV = A_KV_HEADS * A_HEAD_DIM
IDX_Q = IDX_HEADS * IDX_DIM

LANES = 128
SUBLANES = 8
VMEM_LIMIT = 56 * 1024 * 1024

P_GATES = 0
P_GUV = 4096
P_Q = 5120
P_SCB = 5632
P_SCC = 6144
P_SCX = 6656
P_MZ = 7168
P_XBC = 7680
P_K = 8448
P_V = 8576
P_COLS = 8704
S_QI = 0
S_KW = 256
S_DT = 384
S_COLS = 512

INT_MIN = -2147483648
NEG_INF = float("-inf")
NEG_INF_KEY = -2139095041


def _dot(a, b):
    return jnp.dot(a, b, preferred_element_type=F32)


def _dot_nt(a, b):
    return lax.dot_general(a, b, (((1,), (1,)), ((), ())), preferred_element_type=F32)


def _dot_f32(a, b):
    return jnp.dot(a, b, preferred_element_type=F32, precision=lax.Precision.HIGHEST)


def _split(x):
    hi = x.astype(BF16)
    lo = (x - hi.astype(F32)).astype(BF16)
    return hi, lo


def _dot3(a_hi, a_lo, b_hi, b_lo):
    return _dot(a_hi, b_hi) + _dot(a_lo, b_hi) + _dot(a_hi, b_lo)


def _sigmoid(x):
    return jax.nn.sigmoid(x)


def _silu(x):
    return x * _sigmoid(x)


def _softplus(x):
    return jnp.maximum(x, 0.0) + jnp.log1p(jnp.exp(-jnp.abs(x)))


def _gelu(x):
    return 0.5 * x * (1.0 + lax.erf(x * np.float32(np.sqrt(0.5))))


def _rms(x, g_row):
    ms = jnp.mean(x * x, axis=-1, keepdims=True)
    return x * lax.rsqrt(ms + RMS_EPS) * g_row


def _params(*sem):
    return pltpu.CompilerParams(dimension_semantics=sem, vmem_limit_bytes=VMEM_LIMIT)


def _mod_kernel(c_ref, w_ref, b_ref, o_ref):
    o_ref[0] = _dot_f32(_silu(c_ref[...]), w_ref[0]) + b_ref[0]


def _modulation(c, ada_w, ada_b):
    depth, d, n = ada_w.shape
    bsz = c.shape[0]
    tn = 768
    return pl.pallas_call(
        _mod_kernel,
        grid=(depth, n // tn),
        in_specs=[pl.BlockSpec((bsz, d), lambda l, j: (0, 0)),
                  pl.BlockSpec((1, d, tn), lambda l, j: (l, 0, j)),
                  pl.BlockSpec((1, 1, tn), lambda l, j: (l, 0, j))],
        out_specs=pl.BlockSpec((1, bsz, tn), lambda l, j: (l, 0, j)),
        out_shape=jax.ShapeDtypeStruct((depth, bsz, n), F32),
        compiler_params=_params("arbitrary", "arbitrary"),
        name="modulation",
    )(c, ada_w, ada_b.reshape(depth, 1, n))


def _inproj_kernel(x_ref, g_ref, sc_ref, sh_ref, w_ref, wsh_ref, wsl_ref, o_ref, os_ref, hh_ref, hl_ref):
    @pl.when(pl.program_id(1) == 0)
    def _():
        h = _rms(x_ref[...], g_ref[...]) * (1.0 + sc_ref[0]) + sh_ref[0]
        hh, hl = _split(h)
        hh_ref[...] = hh
        hl_ref[...] = hl
        os_ref[...] = _dot3(hh, hl, wsh_ref[...], wsl_ref[...])

    o_ref[...] = _dot(hh_ref[...], w_ref[...]).astype(BF16)


def _inproj(x2, seq, g, scale, shift, w_main, ws_hi, ws_lo):
    ntok, d = x2.shape
    tm = min(1024, seq)
    tn = P_COLS // 4
    bsz = scale.shape[0]
    row = lambda i, j: (i, 0)
    per_b = lambda i, j: ((i * tm) // seq, 0, 0)
    return pl.pallas_call(
        _inproj_kernel,
        grid=(ntok // tm, P_COLS // tn),
        in_specs=[pl.BlockSpec((tm, d), row),
                  pl.BlockSpec((1, d), lambda i, j: (0, 0)),
                  pl.BlockSpec((1, 1, d), per_b),
                  pl.BlockSpec((1, 1, d), per_b),
                  pl.BlockSpec((d, tn), lambda i, j: (0, j)),
                  pl.BlockSpec((d, S_COLS), lambda i, j: (0, 0)),
                  pl.BlockSpec((d, S_COLS), lambda i, j: (0, 0))],
        out_specs=[pl.BlockSpec((tm, tn), lambda i, j: (i, j)),
                   pl.BlockSpec((tm, S_COLS), row)],
        out_shape=[jax.ShapeDtypeStruct((ntok, P_COLS), BF16),
                   jax.ShapeDtypeStruct((ntok, S_COLS), F32)],
        scratch_shapes=[pltpu.VMEM((tm, d), BF16), pltpu.VMEM((tm, d), BF16)],
        compiler_params=_params("arbitrary", "arbitrary"),
        name="inproj",
    )(x2, g.reshape(1, d), scale.reshape(bsz, 1, d), shift.reshape(bsz, 1, d), w_main, ws_hi, ws_lo)


ROPE_ZERO_LANE = A_ROT // 2 + IDX_ROT // 2


def _rope_angles(pos_f):
    half_a, half_i = A_ROT // 2, IDX_ROT // 2
    lane = lax.broadcasted_iota(I32, (1, LANES), 1)
    k = jnp.where(lane < half_a, lane, lane - half_a).astype(F32)
    rot = jnp.where(lane < half_a, np.float32(A_ROT), np.float32(IDX_ROT))
    inv_freq = jnp.exp(np.float32(-math.log(ROPE_THETA)) * k * 2.0 / rot)
    ang = pos_f * jnp.where(lane < half_a + half_i, inv_freq, 0.0)
    return jnp.cos(ang), jnp.sin(ang)


def _rope_tables(cos_sin, place_ref, period, rot, limit=LANES):
    half = rot // 2

    def expand(x):
        hi, lo = _split(x)
        return _dot(hi, place_ref[...]) + _dot(lo, place_ref[...])

    lane_all = lax.broadcasted_iota(I32, (1, LANES), 1)
    lane = lane_all % period
    cos = jnp.where(lane_all < limit, expand(cos_sin[0]), 1.0)
    sin = jnp.where(lane_all < limit, expand(cos_sin[1]), 0.0)
    sin_lo = jnp.where(lane < half, -sin, 0.0)
    sin_hi = jnp.where((lane >= half) & (lane < rot), sin, 0.0)
    return cos, sin_lo, sin_hi


def _rope(x, tables, half):
    cos, sin_lo, sin_hi = tables
    width = x.shape[-1]
    return (x * cos + pltpu.roll(x, width - half, 1) * sin_lo + pltpu.roll(x, half, 1) * sin_hi)


def _prep_kernel(q_ref, k_ref, v_ref, qi_ref, kw_ref, pos_ref, qg_ref, kg_ref, bdq_ref, bdk_ref,
                 pq_ref, pk_ref, pqi1_ref, pqi2_ref, pki1_ref, pki2_ref, pra_ref, pri_ref,
                 qo_ref, ko_ref, vto_ref, qio_ref, kio_ref):
    cos_sin = _rope_angles(pos_ref[...].astype(F32))
    att_tab = _rope_tables(cos_sin, pra_ref, A_HEAD_DIM, A_ROT)
    idx_tab = _rope_tables(cos_sin, pri_ref, IDX_DIM, IDX_ROT)
    kw_tab = _rope_tables(cos_sin, pri_ref, IDX_DIM, IDX_ROT, limit=IDX_DIM)

    def head_norm(x, bd_ref, g_ref):
        sq_hi, sq_lo = _split(x * x)
        ms = (_dot(sq_hi, bd_ref[...]) + _dot(sq_lo, bd_ref[...])) * np.float32(1.0 / A_HEAD_DIM)
        return x * lax.rsqrt(ms + RMS_EPS) * g_ref[...]

    def rope_wide(x, tab, half):
        pieces = [_rope(x[:, c * LANES:(c + 1) * LANES], tab, half) for c in range(x.shape[-1] // LANES)]
        return pieces[0] if len(pieces) == 1 else jnp.concatenate(pieces, axis=-1)

    def store_head_rows(o_ref, x):
        for blk in range(x.shape[0] // Q_BLOCK):
            for h in range(x.shape[1] // LANES):
                r0 = (blk * (x.shape[1] // LANES) + h) * Q_BLOCK
                o_ref[r0:r0 + Q_BLOCK, :] = x[blk * Q_BLOCK:(blk + 1) * Q_BLOCK, h * LANES:(h + 1) * LANES]

    q = rope_wide(head_norm(q_ref[...].astype(F32), bdq_ref, qg_ref), att_tab, A_ROT // 2)
    q = (q * np.float32(A_HEAD_DIM ** -0.5)).astype(BF16)
    store_head_rows(qo_ref, _dot(q, pq_ref[...]).astype(BF16))
    k = rope_wide(head_norm(k_ref[...].astype(F32), bdk_ref, kg_ref), att_tab, A_ROT // 2).astype(BF16)
    ko_ref[...] = _dot(k, pk_ref[...]).astype(BF16)
    v_pad = _dot(v_ref[...], pk_ref[...])
    ones_lane = (lax.broadcasted_iota(I32, (1, A_KV_HEADS * LANES), 1) % LANES) == A_HEAD_DIM
    vto_ref[...] = jnp.transpose(jnp.where(ones_lane, 1.0, v_pad)).astype(BF16)

    qi_hi, qi_lo = _split(rope_wide(qi_ref[...], idx_tab, IDX_ROT // 2))
    store_head_rows(qio_ref, (_dot(qi_hi, pqi1_ref[...]) + _dot(qi_lo, pqi2_ref[...])).astype(BF16))
    ki_hi, ki_lo = _split(_rope(kw_ref[...], kw_tab, IDX_ROT // 2))
    kio_ref[...] = (_dot(ki_hi, pki1_ref[...]) + _dot(ki_lo, pki2_ref[...])).astype(BF16)


@functools.lru_cache(maxsize=None)
def _prep_constants():
    def block_diag(width):
        i = np.arange(width)
        return (i[:, None] // A_HEAD_DIM == i[None, :] // A_HEAD_DIM).astype(np.float32)

    def pad_heads(n_heads):
        p = np.zeros((n_heads * A_HEAD_DIM, n_heads * LANES), np.float32)
        for h in range(n_heads):
            for d in range(A_HEAD_DIM):
                p[h * A_HEAD_DIM + d, h * LANES + d] = 1.0
        return p

    pqi1 = np.zeros((IDX_Q, IDX_HEADS * LANES), np.float32)
    pqi2 = np.zeros((IDX_Q, IDX_HEADS * LANES), np.float32)
    for h in range(IDX_HEADS):
        for d in range(IDX_DIM):
            pqi1[h * IDX_DIM + d, h * LANES + d] = 1.0
            pqi1[h * IDX_DIM + d, h * LANES + IDX_DIM + d] = 1.0
            pqi2[h * IDX_DIM + d, h * LANES + 2 * IDX_DIM + d] = 1.0
    pki1 = np.zeros((LANES, LANES), np.float32)
    pki2 = np.zeros((LANES, LANES), np.float32)
    for d in range(IDX_DIM):
        pki1[d, d] = 1.0
        pki1[d, 2 * IDX_DIM + d] = 1.0
        pki2[d, IDX_DIM + d] = 1.0
    def rope_place(period, rot, first_src):
        p = np.zeros((LANES, LANES), np.float32)
        for j in range(LANES):
            jj = j % period
            p[first_src + jj % (rot // 2) if jj < rot else ROPE_ZERO_LANE, j] = 1.0
        return p

    mats = (block_diag(A_Q), block_diag(A_KV), pad_heads(A_HEADS), pad_heads(A_KV_HEADS), pqi1, pqi2, pki1, pki2,
            rope_place(A_HEAD_DIM, A_ROT, 0), rope_place(IDX_DIM, IDX_ROT, A_ROT // 2))
    return tuple(np.asarray(m) for m in mats)


def _prep(proj, small, pos2, seq, q_g, k_g):
    ntok = proj.shape[0]
    tm = min(512, seq)
    consts = [jnp.asarray(m, BF16) for m in _prep_constants()]
    row = lambda w, c: pl.BlockSpec((tm, w), lambda i: (i, c))
    full = lambda a: pl.BlockSpec(a.shape, lambda i: (0, 0))
    qg = jnp.tile(q_g, A_HEADS).reshape(1, A_Q)
    kg = jnp.tile(k_g, A_KV_HEADS).reshape(1, A_KV)
    return pl.pallas_call(
        _prep_kernel,
        grid=(ntok // tm,),
        in_specs=[row(A_Q, P_Q // A_Q), row(A_KV, P_K // A_KV), row(A_KV, P_V // A_KV), row(IDX_Q, S_QI // IDX_Q),
                  row(LANES, S_KW // LANES), row(1, 0), full(qg), full(kg)] + [full(m) for m in consts],
        out_specs=[pl.BlockSpec((tm * A_HEADS, LANES), lambda i: (i, 0)), row(A_KV_HEADS * LANES, 0),
                   pl.BlockSpec((A_KV_HEADS * LANES, tm), lambda i: (0, i)),
                   pl.BlockSpec((tm * IDX_HEADS, LANES), lambda i: (i, 0)), row(LANES, 0)],
        out_shape=[jax.ShapeDtypeStruct((ntok * A_HEADS, LANES), BF16),
                   jax.ShapeDtypeStruct((ntok, A_KV_HEADS * LANES), BF16),
                   jax.ShapeDtypeStruct((A_KV_HEADS * LANES, ntok), BF16),
                   jax.ShapeDtypeStruct((ntok * IDX_HEADS, LANES), BF16),
                   jax.ShapeDtypeStruct((ntok, LANES), BF16)],
        compiler_params=_params("arbitrary"),
        name="attn_prep",
    )(proj, proj, proj, small, small, pos2, qg, kg, *consts)


def _sortable(x):
    bits = pltpu.bitcast(jnp.where(x == 0.0, 0.0, x), I32)
    return jnp.where(bits < 0, bits ^ jnp.int32(0x7FFFFFFF), bits)


KEY_STEP = 512
KEY_SUBS = KEY_STEP // Q_BLOCK


def _key_row0(step, sub):
    return pl.multiple_of(step * KEY_STEP + sub * Q_BLOCK, Q_BLOCK)


MXU_LOOP_WIDTHS = (4, 2, 1)


def _loop_steps(nstep, body, init, widths=(2, 1)):
    carry, start = init, jnp.int32(0)
    for w in widths:
        count = lax.div(nstep - start, jnp.int32(w))

        def multi(i, c, w=w, start=start):
            for k in range(w):
                c = body(start + i * w + k, c)
            return c
        carry = lax.fori_loop(0, count, multi, carry)
        start = start + count * w
    return carry


def _count_rows(pred_fn, nstep):
    def body(c, acc):
        for sub in range(KEY_SUBS):
            ones = jnp.where(pred_fn(_key_row0(c, sub)), jnp.int32(1), jnp.int32(0))
            acc = acc + jnp.sum(ones.reshape(Q_BLOCK // SUBLANES, SUBLANES, LANES), axis=0)
        return acc
    acc = _loop_steps(nstep, body, jnp.zeros((SUBLANES, LANES), I32), MXU_LOOP_WIDTHS)
    return jnp.sum(acc, axis=0, keepdims=True)


PAIR = 2 * LANES
N_PAIRS = A_HEADS // 2


def _dsa_kernel(qi_ref, kw_ref, q_ref, ki_ref, k_ref, vt_ref, o_ref, key_ref, s_ref, lim_ref, oacc_ref,
                *, topk, idx_bits):
    qb = pl.program_id(1)
    nstep = lax.div(qb, jnp.int32(KEY_SUBS)) + 1
    row_i = lax.broadcasted_iota(I32, (Q_BLOCK, LANES), 0)
    q_pos = qb * Q_BLOCK + lax.broadcasted_iota(I32, (Q_BLOCK, LANES), 1)

    w_rows = jnp.transpose(kw_ref[...])[IDX_DIM:IDX_DIM + IDX_HEADS, :]
    w_rows = w_rows * np.float32(IDX_HEADS ** -0.5) * np.float32(IDX_DIM ** -0.5)

    def rows(r0):
        return pl.ds(r0, Q_BLOCK)

    def score_body(c, carry):
        for sub in range(KEY_SUBS):
            r0 = _key_row0(c, sub)
            ki = ki_ref[rows(r0), :]
            s = jnp.zeros((Q_BLOCK, LANES), F32)
            for pair in range(IDX_HEADS // 2):
                d = _dot_nt(ki, qi_ref[pair * PAIR:(pair + 1) * PAIR, :])
                for j in range(2):
                    h = 2 * pair + j
                    s = s + jnp.maximum(d[:, j * LANES:(j + 1) * LANES], 0.0) * w_rows[h:h + 1, :]
            key_ref[rows(r0), :] = _sortable(jnp.where((r0 + row_i) <= q_pos, s, NEG_INF))
        return carry
    _loop_steps(nstep, score_body, 0, MXU_LOOP_WIDTHS)

    def bit_body(i, cand):
        trial = cand | lax.shift_left(jnp.int32(1), 31 - i)
        trial_s = trial ^ jnp.int32(INT_MIN)
        cnt = _count_rows(lambda r0: key_ref[rows(r0), :] >= trial_s, nstep)
        return jnp.where(cnt >= topk, trial, cand)
    cand = lax.fori_loop(0, 32, bit_body, jnp.zeros((1, LANES), I32))
    thr = cand ^ jnp.int32(INT_MIN)

    n_gt = _count_rows(lambda r0: key_ref[rows(r0), :] > thr, nstep)
    n_eq = _count_rows(lambda r0: key_ref[rows(r0), :] == thr, nstep)
    need = topk - n_gt
    excess = (n_eq > need) & (thr > jnp.int32(NEG_INF_KEY))
    lim_ref[0:1, :] = jnp.full((1, LANES), 2 ** 30, I32)

    @pl.when(jnp.max(jnp.where(excess, 1, 0)) > 0)
    def _():
        def idx_body(i, lim):
            trial = lim | lax.shift_left(jnp.int32(1), idx_bits - 1 - i)
            cnt = _count_rows(lambda r0: (key_ref[rows(r0), :] == thr) & ((r0 + row_i) < trial), nstep)
            return jnp.where(cnt < need, trial, lim)
        lim = lax.fori_loop(0, idx_bits, idx_body, jnp.zeros((1, LANES), I32))
        lim_ref[0:1, :] = jnp.where(excess, lim, 2 ** 30)
    tie_lim = lim_ref[0:1, :]

    def max_body(c, m_all):
        m_all = list(m_all)
        for sub in range(KEY_SUBS):
            r0 = _key_row0(c, sub)
            keys = key_ref[rows(r0), :]
            kpos = r0 + row_i
            sel = ((keys > thr) | ((keys == thr) & (kpos <= tie_lim))) & (kpos <= q_pos)
            bias = jnp.where(sel, 0.0, NEG_INF)
            bias2 = jnp.concatenate([bias, bias], axis=-1)
            for pair in range(N_PAIRS):
                g = (2 * pair) // A_REP
                s = _dot_nt(k_ref[rows(r0), g * LANES:(g + 1) * LANES], q_ref[pair * PAIR:(pair + 1) * PAIR, :]) + bias2
                s_ref[pair, rows(r0), :] = s
                m_all[pair] = jnp.maximum(m_all[pair], jnp.max(s, axis=0, keepdims=True))
        return tuple(m_all)
    m_fin = _loop_steps(nstep, max_body, tuple(jnp.full((1, PAIR), -1e30, F32) for _ in range(N_PAIRS)),
                        MXU_LOOP_WIDTHS)

    oacc_ref[...] = jnp.zeros(oacc_ref.shape, F32)

    def pv_body(c, carry):
        cols = pl.ds(pl.multiple_of(c * KEY_STEP, KEY_STEP), KEY_STEP)
        for pair in range(N_PAIRS):
            g = (2 * pair) // A_REP
            p = jnp.exp(s_ref[pair, cols, :] - m_fin[pair]).astype(BF16)
            oacc_ref[pair] += _dot(vt_ref[g * LANES:(g + 1) * LANES, cols], p)
        return carry
    _loop_steps(nstep, pv_body, 0, MXU_LOOP_WIDTHS)

    outs = []
    for pair in range(N_PAIRS):
        o_p = oacc_ref[pair]
        for j in range(2):
            o_h = o_p[:, j * LANES:(j + 1) * LANES]
            outs.append(o_h[0:A_HEAD_DIM, :] / o_h[A_HEAD_DIM:A_HEAD_DIM + 1, :])
    o_ref[...] = jnp.transpose(jnp.concatenate(outs, axis=0)).astype(BF16)


def _dsa(qi_cat, small, q_pad, ki_cat, k_pad, v_t, bsz, seq):
    assert seq % KEY_STEP == 0
    nb = seq // Q_BLOCK
    topk = min(IDX_TOPK_MAX, seq // 4)
    idx_bits = max(1, int(math.ceil(math.log2(seq))))
    qrow = lambda w, c: pl.BlockSpec((Q_BLOCK, w), lambda b, j: (b * nb + j, c))
    brow = lambda w: pl.BlockSpec((seq, w), lambda b, j: (b, 0))
    return pl.pallas_call(
        functools.partial(_dsa_kernel, topk=topk, idx_bits=idx_bits),
        grid=(bsz, nb),
        in_specs=[pl.BlockSpec((IDX_HEADS * Q_BLOCK, LANES), lambda b, j: (b * nb + j, 0)),
                  qrow(LANES, S_KW // LANES),
                  pl.BlockSpec((A_HEADS * Q_BLOCK, LANES), lambda b, j: (b * nb + j, 0)),
                  brow(LANES), brow(A_KV_HEADS * LANES),
                  pl.BlockSpec((A_KV_HEADS * LANES, seq), lambda b, j: (0, b))],
        out_specs=qrow(A_Q, 0),
        out_shape=jax.ShapeDtypeStruct((bsz * seq, A_Q), BF16),
        scratch_shapes=[pltpu.VMEM((seq, LANES), I32), pltpu.VMEM((N_PAIRS, seq, PAIR), F32),
                        pltpu.VMEM((SUBLANES, LANES), I32), pltpu.VMEM((N_PAIRS, LANES, PAIR), F32)],
        compiler_params=_params("arbitrary", "arbitrary"),
        name="sparse_attention",
    )(qi_cat, small, q_pad, ki_cat, k_pad, v_t)


HALO = SUBLANES


def _causal_conv(ext_ref, u, w_ref, taps, rows):
    ext_ref[HALO:HALO + rows, :] = u
    y = u * w_ref[taps - 1:taps, :]
    for j in range(taps - 1):
        shift = taps - 1 - j
        y = y + ext_ref[HALO - shift:HALO - shift + rows, :] * w_ref[j:j + 1, :]
    ext_ref[0:HALO, :] = ext_ref[rows:rows + HALO, :]
    return y


def _mix_kernel(x_ref, g0_ref, g1_ref, g2_ref, g3_ref, guv_ref, scb_ref, scc_ref, scx_ref, mz_ref, xbc_ref,
                dt_ref, oa_ref, gate_ref, scw_ref, gvg_ref, ws_ref, wsb_ref, mcw_ref, mcb_ref, dtb_ref, alog_ref,
                md_ref, mng_ref, wb_ref, wo_ref, o_ref, sc_ext, xbc_ext, state_ref, y_ref, *, rows):
    @pl.when(pl.program_id(1) == 0)
    def _():
        sc_ext[0:HALO, :] = jnp.zeros((HALO, SC_W), F32)
        xbc_ext[0:HALO, :] = jnp.zeros((HALO, M_XBC), F32)
        state_ref[...] = jnp.zeros(state_ref.shape, F32)

    nchunk = rows // M_CHUNK
    row_i = lax.broadcasted_iota(I32, (M_CHUNK, M_CHUNK), 0)
    col_i = lax.broadcasted_iota(I32, (M_CHUNK, M_CHUNK), 1)
    tri = row_i >= col_i

    o_b = scb_ref[...].astype(F32) * _causal_conv(
        sc_ext, scc_ref[...].astype(F32) * scx_ref[...].astype(F32), scw_ref, SC_KERNEL, rows)

    guv = _gelu(guv_ref[...].astype(F32))
    gu = guv[:, :GM_W]
    gv = _rms(guv[:, GM_W:], gvg_ref[...]).astype(BF16)
    mixed_rows = []
    for ci in range(nchunk):
        r0 = ci * GM_CHUNK
        cols = []
        for gi in range(GM_GROUPS):
            vb = gv[r0:r0 + GM_CHUNK, gi * GM_GROUP_W:(gi + 1) * GM_GROUP_W]
            cols.append(_dot(ws_ref[gi], vb) + wsb_ref[:, gi:gi + 1])
        mixed_rows.append(jnp.concatenate(cols, axis=-1))
    o_c = gu * jnp.concatenate(mixed_rows, axis=0)

    xbc = _silu(_causal_conv(xbc_ext, xbc_ref[...].astype(F32), mcw_ref, M_CONV, rows) + mcb_ref[...])
    head_lane = lax.broadcasted_iota(I32, (1, LANES), 1) < M_HEADS
    dt = jnp.where(head_lane, _softplus(dt_ref[...] + dtb_ref[...]), 0.0)
    a_row = jnp.where(head_lane, -jnp.exp(alog_ref[...]), 0.0)
    tri_f = jnp.where(tri, 1.0, 0.0).astype(F32)
    for ci in range(nchunk):
        r0 = ci * M_CHUNK
        dtc = dt[r0:r0 + M_CHUNK, :]
        acs = _dot_f32(tri_f, dtc * a_row)
        acs_t = jnp.transpose(acs)
        b_t = jnp.transpose(xbc[r0:r0 + M_CHUNK, M_DINNER:M_DINNER + M_BC]).astype(BF16)
        cm = xbc[r0:r0 + M_CHUNK, M_DINNER + M_BC:M_XBC].astype(BF16)
        for g in range(M_GROUPS):
            bg_t = b_t[g * M_STATE:(g + 1) * M_STATE, :]
            cg = cm[:, g * M_STATE:(g + 1) * M_STATE]
            cb = _dot(cg, bg_t)
            for h in range(g * (M_HEADS // M_GROUPS), (g + 1) * (M_HEADS // M_GROUPS)):
                col = acs[:, h:h + 1]
                row = acs_t[h:h + 1, :]
                last = acs[M_CHUNK - 1:M_CHUNK, h:h + 1]
                decay = jnp.exp(jnp.where(tri, col - row, NEG_INF))
                xs_h = xbc[r0:r0 + M_CHUNK, h * M_HEAD_DIM:(h + 1) * M_HEAD_DIM]
                xdt = xs_h * dtc[:, h:h + 1]
                y = _dot((cb * decay).astype(BF16), xdt.astype(BF16))
                prev_t = state_ref[h]
                y = y + _dot(cg, prev_t.astype(BF16)) * jnp.exp(col)
                st_t = _dot(bg_t, (xdt * jnp.exp(last - col)).astype(BF16))
                state_ref[h] = prev_t * jnp.exp(last) + st_t
                y_ref[r0:r0 + M_CHUNK, h * M_HEAD_DIM:(h + 1) * M_HEAD_DIM] = y + md_ref[:, h:h + 1] * xs_h
    o_d = _rms(y_ref[...] * _silu(mz_ref[...].astype(F32)), mng_ref[...])

    branches = (oa_ref[...], o_b.astype(BF16), o_c.astype(BF16), o_d.astype(BF16))
    gate_refs = (g0_ref, g1_ref, g2_ref, g3_ref)
    merged = None
    for i in range(N_BRANCH):
        term = _sigmoid(gate_refs[i][...].astype(F32)) * _dot(branches[i], wb_ref[i])
        merged = term if merged is None else merged + term
    o_ref[...] = x_ref[...] + gate_ref[0] * _dot(merged.astype(BF16), wo_ref[...])


def _mix(x2, proj, small, o_a, gate1, lw, bsz, seq):
    d = x2.shape[1]
    rows = min(256, seq)
    ns = seq // rows
    tok = lambda w, c: pl.BlockSpec((rows, w), lambda b, s: (b * ns + s, c))
    full = lambda a: pl.BlockSpec(a.shape, lambda b, s: (0,) * a.ndim)
    weights = [lw["sc_conv_w"], lw["g_v_norm_g"], lw["ws"], lw["ws_b"], lw["m_conv_w"], lw["m_conv_b"],
               lw["m_dt_bias"], lw["m_a_log"], lw["m_d"], lw["m_norm_g"], lw["w_branch"], lw["w_out"]]
    return pl.pallas_call(
        functools.partial(_mix_kernel, rows=rows),
        grid=(bsz, ns),
        in_specs=[tok(d, 0)] + [tok(D_MODEL, i) for i in range(N_BRANCH)]
        + [tok(2 * GM_W, P_GUV // (2 * GM_W)), tok(SC_W, P_SCB // SC_W), tok(SC_W, P_SCC // SC_W),
           tok(SC_W, P_SCX // SC_W), tok(M_DINNER, P_MZ // M_DINNER), tok(M_XBC, P_XBC // M_XBC),
           tok(LANES, S_DT // LANES), tok(A_Q, 0),
           pl.BlockSpec((1, 1, d), lambda b, s: (b, 0, 0))]
        + [full(w) for w in weights],
        out_specs=tok(d, 0),
        out_shape=jax.ShapeDtypeStruct(x2.shape, F32),
        scratch_shapes=[pltpu.VMEM((HALO + rows, SC_W), F32), pltpu.VMEM((HALO + rows, M_XBC), F32),
                        pltpu.VMEM((M_HEADS, M_STATE, M_HEAD_DIM), F32), pltpu.VMEM((rows, M_DINNER), F32)],
        compiler_params=_params("arbitrary", "arbitrary"),
        name="mixers_merge",
    )(x2, proj, proj, proj, proj, proj, proj, proj, proj, proj, proj, small, o_a,
      gate1.reshape(bsz, 1, d), *weights)


ROUTE_ROWS = EXPERTS_PER_GROUP * SUBLANES
BEST_ROW = ROUTE_ROWS


def _route_row(e):
    return (e % EXPERTS_PER_GROUP) * SUBLANES + e // EXPERTS_PER_GROUP


def _route(scores_t, bias_col):
    ntok = scores_t.shape[1]
    row = lax.broadcasted_iota(I32, (SUBLANES, ntok), 0)
    members = range(EXPERTS_PER_GROUP)
    sc = [scores_t[i * SUBLANES:(i + 1) * SUBLANES, :] for i in members]
    x = [sc[i] + bias_col[i * SUBLANES:(i + 1) * SUBLANES, :] for i in members]
    hi1, lo1 = jnp.maximum(x[0], x[1]), jnp.minimum(x[0], x[1])
    hi2, lo2 = jnp.maximum(x[2], x[3]), jnp.minimum(x[2], x[3])
    grp = jnp.maximum(hi1, hi2) + jnp.maximum(jnp.minimum(hi1, hi2), jnp.maximum(lo1, lo2))
    grp = jnp.where(row < N_EXPERT_GROUPS, grp, NEG_INF)
    best_val = jnp.max(grp, axis=0, keepdims=True)
    best = jnp.min(jnp.where(grp == best_val, row, SUBLANES), axis=0, keepdims=True)
    in_best = row == best
    y = [jnp.max(jnp.where(in_best, x[i], NEG_INF), axis=0, keepdims=True) for i in members]
    s = [jnp.sum(jnp.where(in_best, sc[i], 0.0), axis=0, keepdims=True) for i in members]

    def first_argmax(vals):
        top = functools.reduce(jnp.maximum, vals)
        idx = jnp.full(top.shape, len(vals) - 1, I32)
        for i in reversed(range(len(vals) - 1)):
            idx = jnp.where(vals[i] == top, i, idx)
        return idx

    i1 = first_argmax(y)
    i2 = first_argmax([jnp.where(i1 == i, NEG_INF, y[i]) for i in members])
    w1 = sum(jnp.where(i1 == i, s[i], 0.0) for i in members)
    w2 = sum(jnp.where(i2 == i, s[i], 0.0) for i in members)
    tot = w1 + w2
    parts = [jnp.where(in_best, jnp.where(i1 == i, w1 / tot, 0.0) + jnp.where(i2 == i, w2 / tot, 0.0), 0.0)
             for i in members]
    parts.append(jnp.broadcast_to(best.astype(F32), (SUBLANES, ntok)))
    parts.append(jnp.zeros((LANES - (ROUTE_ROWS + SUBLANES), ntok), F32))
    return jnp.concatenate(parts, axis=0)


MOE_CHUNK = 128


def _dot_tn(a, b):
    return lax.dot_general(a, b, (((0,), (0,)), ((), ())), preferred_element_type=F32)


def _moe_kernel(x_ref, g_ref, sc_ref, sh_ref, gate_ref, rwh_ref, rwl_ref, rb_ref, tri_ref, wgu_ref, wd_ref, o_ref,
                h_ref, c2_ref, grp_ref, rank_ref, tot_ref):
    grp_id = pl.program_id(1)
    tm = x_ref.shape[0]
    lane = lax.broadcasted_iota(I32, (tm, LANES), 1)

    @pl.when(grp_id == 0)
    def _():
        h = _rms(x_ref[...], g_ref[...]) * (1.0 + sc_ref[0]) + sh_ref[0]
        hh, hl = _split(h)
        h_ref[...] = hh
        logits_t = (_dot_nt(rwh_ref[...], hh) + _dot_nt(rwh_ref[...], hl) + _dot_nt(rwl_ref[...], hh))
        routed = jnp.transpose(_route(_sigmoid(logits_t[:ROUTE_ROWS, :]), rb_ref[...]))
        best = routed[:, BEST_ROW:BEST_ROW + 1].astype(I32)
        comb = jnp.where(lane < ROUTE_ROWS, routed, 0.0)
        c_hi, c_lo = _split(comb)
        c2_ref[:, :LANES] = c_hi
        c2_ref[:, LANES:] = c_lo
        grp_ref[...] = jnp.broadcast_to(best, (tm, LANES))
        onehot = jnp.where(lane == best, 1.0, 0.0)
        offs = jnp.zeros((1, LANES), F32)
        for b in range(tm // MOE_CHUNK):
            blk = onehot[b * MOE_CHUNK:(b + 1) * MOE_CHUNK, :]
            ranks = _dot(tri_ref[...], blk.astype(BF16)) + offs
            own = jnp.sum(blk * ranks, axis=-1, keepdims=True)
            rank_ref[b * MOE_CHUNK:(b + 1) * MOE_CHUNK, :] = jnp.broadcast_to(own, (MOE_CHUNK, LANES))
            offs = offs + jnp.sum(blk, axis=0, keepdims=True)
        tot_ref[0:1, :] = offs
        o_ref[...] = jnp.zeros(o_ref.shape, F32)

    lane1 = lax.broadcasted_iota(I32, (1, LANES), 1)
    n_tok = jnp.sum(jnp.where(lane1 == grp_id, tot_ref[0:1, :], 0.0)).astype(I32)
    n_chunks = lax.div(n_tok + (MOE_CHUNK - 1), jnp.int32(MOE_CHUNK))
    lane_f = lane.astype(F32)
    lane_c = lax.broadcasted_iota(I32, (MOE_CHUNK, LANES), 1)

    def chunk_body(c, carry):
        base = (c * MOE_CHUNK).astype(F32)
        pick = ((rank_ref[...] - base) == lane_f) & (grp_ref[...] == grp_id)
        st = jnp.where(pick, 1.0, 0.0).astype(BF16)
        xc = _dot_tn(st, h_ref[...]).astype(BF16)
        wc2 = _dot_tn(st, c2_ref[...])
        wc = wc2[:, :LANES] + wc2[:, LANES:]
        y = jnp.zeros((MOE_CHUNK, x_ref.shape[1]), F32)
        for el in range(EXPERTS_PER_GROUP):
            gu = _dot(xc, wgu_ref[el])
            hid = _silu(gu[:, :D_FF_EXPERT]) * gu[:, D_FF_EXPERT:]
            w_e = jnp.sum(jnp.where(lane_c == el * SUBLANES + grp_id, wc, 0.0), axis=-1, keepdims=True)
            y = y + w_e * _dot(hid.astype(BF16), wd_ref[el])
        y_hi, y_lo = _split(y)
        o_ref[...] += _dot(jnp.concatenate([st, st], axis=1), jnp.concatenate([y_hi, y_lo], axis=0))
        return carry
    _loop_steps(n_chunks, chunk_body, 0)

    @pl.when(grp_id == pl.num_programs(1) - 1)
    def _():
        o_ref[...] = x_ref[...] + gate_ref[0] * o_ref[...]


def _moe(x2, seq, g, scale, shift, gate, rw_hi, rw_lo, rb, wgu, wd):
    ntok, d = x2.shape
    bsz = scale.shape[0]
    tm = min(1024, seq)
    row = lambda i, e: (i, 0)
    per_b = lambda i, e: ((i * tm) // seq, 0, 0)
    vec = pl.BlockSpec((1, 1, d), per_b)
    const2 = lambda a: pl.BlockSpec(a.shape, lambda i, e: (0, 0))
    tri = jnp.asarray(np.tril(np.ones((MOE_CHUNK, MOE_CHUNK), np.float32), -1), BF16)
    return pl.pallas_call(
        _moe_kernel,
        grid=(ntok // tm, N_EXPERT_GROUPS),
        in_specs=[pl.BlockSpec((tm, d), row), pl.BlockSpec((1, d), lambda i, e: (0, 0)), vec, vec, vec,
                  const2(rw_hi), const2(rw_lo), const2(rb), const2(tri),
                  pl.BlockSpec((EXPERTS_PER_GROUP, d, 2 * D_FF_EXPERT), lambda i, e: (e, 0, 0)),
                  pl.BlockSpec((EXPERTS_PER_GROUP, D_FF_EXPERT, d), lambda i, e: (e, 0, 0))],
        out_specs=pl.BlockSpec((tm, d), row),
        out_shape=jax.ShapeDtypeStruct(x2.shape, F32),
        scratch_shapes=[pltpu.VMEM((tm, d), BF16), pltpu.VMEM((tm, 2 * LANES), BF16),
                        pltpu.VMEM((tm, LANES), I32), pltpu.VMEM((tm, LANES), F32), pltpu.VMEM((SUBLANES, LANES), F32)],
        compiler_params=_params("arbitrary", "arbitrary"),
        name="moe",
    )(x2, g.reshape(1, d), scale.reshape(bsz, 1, d), shift.reshape(bsz, 1, d), gate.reshape(bsz, 1, d),
      rw_hi, rw_lo, rb, tri, wgu, wd)


def _pad_lanes(v, width=LANES):
    v = v.reshape(1, -1)
    return jnp.pad(v, ((0, 0), (0, width - v.shape[1])))


def _layer_weights(w_in, sc_conv_w, g_v_norm_g, g_spatial_w, g_spatial_b, m_conv_w, m_conv_b, m_dt_bias,
                   m_a_log, m_d, m_norm_g, w_branch, w_out):
    splits = (A_Q, A_KV, A_KV, IDX_Q, IDX_DIM, IDX_HEADS, SC_W, SC_W, SC_W, 2 * GM_W, M_DINNER, M_XBC, M_HEADS,
              N_BRANCH * D_MODEL)
    pts = [int(p) for p in np.cumsum(splits)[:-1]]
    (q, k, v, qi, ki, wi, sc_b, sc_c, sc_x, g_uv, m_z, m_xbc, m_dt, gates) = jnp.split(w_in, pts, axis=-1)
    d = w_in.shape[0]
    w_main = jnp.concatenate([gates, g_uv, q, sc_b, sc_c, sc_x, m_z, m_xbc, k, v], axis=-1).astype(BF16)
    w_small = jnp.concatenate(
        [qi, ki, wi, jnp.zeros((d, LANES - IDX_DIM - IDX_HEADS), F32), m_dt, jnp.zeros((d, LANES - M_HEADS), F32)],
        axis=-1)
    ws_hi = w_small.astype(BF16)
    ws_lo = (w_small - ws_hi.astype(F32)).astype(BF16)
    tri = np.tril(np.ones((GM_CHUNK, GM_CHUNK), dtype=bool))
    return dict(
        w_main=w_main, ws_hi=ws_hi, ws_lo=ws_lo,
        sc_conv_w=sc_conv_w, g_v_norm_g=g_v_norm_g.reshape(1, GM_W),
        ws=jnp.where(tri[None], g_spatial_w, 0.0).astype(BF16), ws_b=g_spatial_b.T,
        m_conv_w=m_conv_w, m_conv_b=m_conv_b.reshape(1, M_XBC),
        m_dt_bias=_pad_lanes(m_dt_bias), m_a_log=_pad_lanes(m_a_log), m_d=_pad_lanes(m_d),
        m_norm_g=m_norm_g.reshape(1, M_DINNER), w_branch=w_branch.astype(BF16), w_out=w_out.astype(BF16))


def kernel(x, c, positions, ada_w, ada_b, norm1_g, w_in, q_norm_g, k_norm_g, sc_conv_w, g_v_norm_g, g_spatial_w,
           g_spatial_b, m_conv_w, m_conv_b, m_dt_bias, m_a_log, m_d, m_norm_g, w_branch, w_out, norm2_g, router_w,
           router_bias, exp_w_gate, exp_w_up, exp_w_down):
    bsz, seq, d = x.shape
    depth = ada_w.shape[0]
    mod = _modulation(c, ada_w, ada_b)
    route_rows = np.array([_route_row(e) for e in range(N_EXPERTS)])
    rw = jnp.zeros((LANES, d), F32).at[route_rows].set(router_w.T)
    rw_hi = rw.astype(BF16)
    rw_lo = (rw - rw_hi.astype(F32)).astype(BF16)
    rb = jnp.zeros((ROUTE_ROWS, 1), F32).at[route_rows, 0].set(router_bias)
    pos2 = positions.reshape(bsz * seq, 1)
    x2 = x.reshape(bsz * seq, d)
    for l in range(depth):
        shift1, scale1, gate1, shift2, scale2, gate2 = jnp.split(mod[l], 6, axis=-1)
        lw = _layer_weights(w_in[l], sc_conv_w[l], g_v_norm_g[l], g_spatial_w[l], g_spatial_b[l], m_conv_w[l],
                            m_conv_b[l], m_dt_bias[l], m_a_log[l], m_d[l], m_norm_g[l], w_branch[l], w_out[l])
        proj, small = _inproj(x2, seq, norm1_g[l], scale1, shift1, lw["w_main"], lw["ws_hi"], lw["ws_lo"])
        q_pad, k_pad, v_t, qi_cat, ki_cat = _prep(proj, small, pos2, seq, q_norm_g[l], k_norm_g[l])
        o_a = _dsa(qi_cat, small, q_pad, ki_cat, k_pad, v_t, bsz, seq)
        x2 = _mix(x2, proj, small, o_a, gate1, lw, bsz, seq)
        wgu = jnp.concatenate([exp_w_gate[l], exp_w_up[l]], axis=-1).astype(BF16)
        x2 = _moe(x2, seq, norm2_g[l], scale2, shift2, gate2, rw_hi, rw_lo, rb, wgu, exp_w_down[l].astype(BF16))
    return x2.reshape(bsz, seq, d)
```

```python
import functools
import math

import jax
import jax.numpy as jnp
import numpy as np
from jax import lax
from jax.experimental import pallas as pl
from jax.experimental.pallas import tpu as pltpu

F32 = jnp.float32
BF16 = jnp.bfloat16
I32 = jnp.int32

D_MODEL = 1024
RMS_EPS = 1e-6
ROPE_THETA = 500000.0
Q_BLOCK = 128
A_HEADS = 8
A_KV_HEADS = 2
A_HEAD_DIM = 64
A_ROT = A_HEAD_DIM // 4
A_REP = A_HEADS // A_KV_HEADS
IDX_HEADS = 8
IDX_DIM = 32
IDX_ROT = IDX_DIM // 4
IDX_TOPK_MAX = 256
SC_W = 512
SC_KERNEL = 3
GM_W = 512
GM_GROUPS = 4
GM_GROUP_W = GM_W // GM_GROUPS
GM_CHUNK = 128
M_DINNER = 512
M_HEAD_DIM = 64
M_HEADS = M_DINNER // M_HEAD_DIM
M_GROUPS = 2
M_STATE = 64
M_CONV = 4
M_CHUNK = 128
M_BC = M_GROUPS * M_STATE
M_XBC = M_DINNER + 2 * M_BC
N_BRANCH = 4
BRANCH_W = 512
N_EXPERTS = 16
N_EXPERT_GROUPS = 4
EXPERTS_PER_GROUP = N_EXPERTS // N_EXPERT_GROUPS
D_FF_EXPERT = 512
A_Q = A_HEADS * A_HEAD_DIM
A_KV = A_KV_HEADS * A_HEAD_DIM
IDX_Q = IDX_HEADS * IDX_DIM

LANES = 128
SUBLANES = 8
VMEM_LIMIT = 56 * 1024 * 1024

P_GATES = 0
P_GUV = 4096
P_Q = 5120
P_SCB = 5632
P_SCC = 6144
P_SCX = 6656
P_MZ = 7168
P_XBC = 7680
P_K = 8448
P_V = 8576
P_COLS = 8704
S_QI = 0
S_KW = 256
S_DT = 384
S_COLS = 512

INT_MIN = -2147483648
NEG_INF = float("-inf")
NEG_INF_KEY = -2139095041


def _dot(a, b):
    return jnp.dot(a, b, preferred_element_type=F32)


def _dot_nt(a, b):
    return lax.dot_general(a, b, (((1,), (1,)), ((), ())), preferred_element_type=F32)


def _dot_f32(a, b):
    return jnp.dot(a, b, preferred_element_type=F32, precision=lax.Precision.HIGHEST)


def _split(x):
    hi = x.astype(BF16)
    lo = (x - hi.astype(F32)).astype(BF16)
    return hi, lo


def _dot3(a_hi, a_lo, b_hi, b_lo):
    return _dot(a_hi, b_hi) + _dot(a_lo, b_hi) + _dot(a_hi, b_lo)


def _sigmoid(x):
    return jax.nn.sigmoid(x)


def _silu(x):
    return x * _sigmoid(x)


def _softplus(x):
    return jnp.maximum(x, 0.0) + jnp.log1p(jnp.exp(-jnp.abs(x)))


def _gelu(x):
    return 0.5 * x * (1.0 + lax.erf(x * np.float32(np.sqrt(0.5))))


def _rms(x, g_row):
    ms = jnp.mean(x * x, axis=-1, keepdims=True)
    return x * lax.rsqrt(ms + RMS_EPS) * g_row


def _params(*sem):
    return pltpu.CompilerParams(dimension_semantics=sem, vmem_limit_bytes=VMEM_LIMIT)


def _mod_kernel(c_ref, w_ref, b_ref, o_ref):
    o_ref[0] = _dot_f32(_silu(c_ref[...]), w_ref[0]) + b_ref[0]


def _modulation(c, ada_w, ada_b):
    depth, d, n = ada_w.shape
    bsz = c.shape[0]
    tn = 768
    return pl.pallas_call(
        _mod_kernel,
        grid=(depth, n // tn),
        in_specs=[pl.BlockSpec((bsz, d), lambda l, j: (0, 0)),
                  pl.BlockSpec((1, d, tn), lambda l, j: (l, 0, j)),
                  pl.BlockSpec((1, 1, tn), lambda l, j: (l, 0, j))],
        out_specs=pl.BlockSpec((1, bsz, tn), lambda l, j: (l, 0, j)),
        out_shape=jax.ShapeDtypeStruct((depth, bsz, n), F32),
        compiler_params=_params("arbitrary", "arbitrary"),
        name="modulation",
    )(c, ada_w, ada_b.reshape(depth, 1, n))


def _inproj_kernel(x_ref, g_ref, sc_ref, sh_ref, w_ref, wsh_ref, wsl_ref, o_ref, os_ref, hh_ref, hl_ref):
    @pl.when(pl.program_id(1) == 0)
    def _():
        h = _rms(x_ref[...], g_ref[...]) * (1.0 + sc_ref[0]) + sh_ref[0]
        hh, hl = _split(h)
        hh_ref[...] = hh
        hl_ref[...] = hl
        os_ref[...] = _dot3(hh, hl, wsh_ref[...], wsl_ref[...])

    o_ref[...] = _dot(hh_ref[...], w_ref[...]).astype(BF16)


def _inproj(x2, seq, g, scale, shift, w_main, ws_hi, ws_lo):
    ntok, d = x2.shape
    tm = min(1024, seq)
    tn = P_COLS // 4
    bsz = scale.shape[0]
    row = lambda i, j: (i, 0)
    per_b = lambda i, j: ((i * tm) // seq, 0, 0)
    return pl.pallas_call(
        _inproj_kernel,
        grid=(ntok // tm, P_COLS // tn),
        in_specs=[pl.BlockSpec((tm, d), row),
                  pl.BlockSpec((1, d), lambda i, j: (0, 0)),
                  pl.BlockSpec((1, 1, d), per_b),
                  pl.BlockSpec((1, 1, d), per_b),
                  pl.BlockSpec((d, tn), lambda i, j: (0, j)),
                  pl.BlockSpec((d, S_COLS), lambda i, j: (0, 0)),
                  pl.BlockSpec((d, S_COLS), lambda i, j: (0, 0))],
        out_specs=[pl.BlockSpec((tm, tn), lambda i, j: (i, j)),
                   pl.BlockSpec((tm, S_COLS), row)],
        out_shape=[jax.ShapeDtypeStruct((ntok, P_COLS), BF16),
                   jax.ShapeDtypeStruct((ntok, S_COLS), F32)],
        scratch_shapes=[pltpu.VMEM((tm, d), BF16), pltpu.VMEM((tm, d), BF16)],
        compiler_params=_params("arbitrary", "arbitrary"),
        name="inproj",
    )(x2, g.reshape(1, d), scale.reshape(bsz, 1, d), shift.reshape(bsz, 1, d), w_main, ws_hi, ws_lo)


ROPE_ZERO_LANE = A_ROT // 2 + IDX_ROT // 2


def _rope_angles(pos_f):
    half_a, half_i = A_ROT // 2, IDX_ROT // 2
    lane = lax.broadcasted_iota(I32, (1, LANES), 1)
    k = jnp.where(lane < half_a, lane, lane - half_a).astype(F32)
    rot = jnp.where(lane < half_a, np.float32(A_ROT), np.float32(IDX_ROT))
    inv_freq = jnp.exp(np.float32(-math.log(ROPE_THETA)) * k * 2.0 / rot)
    ang = pos_f * jnp.where(lane < half_a + half_i, inv_freq, 0.0)
    return jnp.cos(ang), jnp.sin(ang)


def _rope_tables(cos_sin, place_ref, period, rot, limit=LANES):
    half = rot // 2

    def expand(x):
        hi, lo = _split(x)
        return _dot(hi, place_ref[...]) + _dot(lo, place_ref[...])

    lane_all = lax.broadcasted_iota(I32, (1, LANES), 1)
    lane = lane_all % period
    cos = jnp.where(lane_all < limit, expand(cos_sin[0]), 1.0)
    sin = jnp.where(lane_all < limit, expand(cos_sin[1]), 0.0)
    sin_lo = jnp.where(lane < half, -sin, 0.0)
    sin_hi = jnp.where((lane >= half) & (lane < rot), sin, 0.0)
    return cos, sin_lo, sin_hi


def _rope(x, tables, half):
    cos, sin_lo, sin_hi = tables
    width = x.shape[-1]
    return (x * cos + pltpu.roll(x, width - half, 1) * sin_lo + pltpu.roll(x, half, 1) * sin_hi)


def _prep_kernel(q_ref, k_ref, v_ref, qi_ref, kw_ref, pos_ref, qg_ref, kg_ref, bdq_ref, bdk_ref,
                 pq_ref, pk_ref, pqi1_ref, pqi2_ref, pki1_ref, pki2_ref, pra_ref, pri_ref,
                 qo_ref, ko_ref, vto_ref, qio_ref, kio_ref):
    cos_sin = _rope_angles(pos_ref[...].astype(F32))
    att_tab = _rope_tables(cos_sin, pra_ref, A_HEAD_DIM, A_ROT)
    idx_tab = _rope_tables(cos_sin, pri_ref, IDX_DIM, IDX_ROT)
    kw_tab = _rope_tables(cos_sin, pri_ref, IDX_DIM, IDX_ROT, limit=IDX_DIM)

    def head_norm(x, bd_ref, g_ref):
        sq_hi, sq_lo = _split(x * x)
        ms = (_dot(sq_hi, bd_ref[...]) + _dot(sq_lo, bd_ref[...])) * np.float32(1.0 / A_HEAD_DIM)
        return x * lax.rsqrt(ms + RMS_EPS) * g_ref[...]

    def rope_wide(x, tab, half):
        pieces = [_rope(x[:, c * LANES:(c + 1) * LANES], tab, half) for c in range(x.shape[-1] // LANES)]
        return pieces[0] if len(pieces) == 1 else jnp.concatenate(pieces, axis=-1)

    def store_head_rows(o_ref, x):
        for blk in range(x.shape[0] // Q_BLOCK):
            for h in range(x.shape[1] // LANES):
                r0 = (blk * (x.shape[1] // LANES) + h) * Q_BLOCK
                o_ref[r0:r0 + Q_BLOCK, :] = x[blk * Q_BLOCK:(blk + 1) * Q_BLOCK, h * LANES:(h + 1) * LANES]

    q = rope_wide(head_norm(q_ref[...].astype(F32), bdq_ref, qg_ref), att_tab, A_ROT // 2)
    q = (q * np.float32(A_HEAD_DIM ** -0.5)).astype(BF16)
    store_head_rows(qo_ref, _dot(q, pq_ref[...]).astype(BF16))
    k = rope_wide(head_norm(k_ref[...].astype(F32), bdk_ref, kg_ref), att_tab, A_ROT // 2).astype(BF16)
    ko_ref[...] = _dot(k, pk_ref[...]).astype(BF16)
    v_pad = _dot(v_ref[...], pk_ref[...])
    ones_lane = (lax.broadcasted_iota(I32, (1, A_KV_HEADS * LANES), 1) % LANES) == A_HEAD_DIM
    vto_ref[...] = jnp.transpose(jnp.where(ones_lane, 1.0, v_pad)).astype(BF16)

    qi_hi, qi_lo = _split(rope_wide(qi_ref[...], idx_tab, IDX_ROT // 2))
    store_head_rows(qio_ref, (_dot(qi_hi, pqi1_ref[...]) + _dot(qi_lo, pqi2_ref[...])).astype(BF16))
    ki_hi, ki_lo = _split(_rope(kw_ref[...], kw_tab, IDX_ROT // 2))
    kio_ref[...] = (_dot(ki_hi, pki1_ref[...]) + _dot(ki_lo, pki2_ref[...])).astype(BF16)


@functools.lru_cache(maxsize=None)
def _prep_constants():
    def block_diag(width):
        i = np.arange(width)
        return (i[:, None] // A_HEAD_DIM == i[None, :] // A_HEAD_DIM).astype(np.float32)

    def pad_heads(n_heads):
        p = np.zeros((n_heads * A_HEAD_DIM, n_heads * LANES), np.float32)
        for h in range(n_heads):
            for d in range(A_HEAD_DIM):
                p[h * A_HEAD_DIM + d, h * LANES + d] = 1.0
        return p

    pqi1 = np.zeros((IDX_Q, IDX_HEADS * LANES), np.float32)
    pqi2 = np.zeros((IDX_Q, IDX_HEADS * LANES), np.float32)
    for h in range(IDX_HEADS):
        for d in range(IDX_DIM):
            pqi1[h * IDX_DIM + d, h * LANES + d] = 1.0
            pqi1[h * IDX_DIM + d, h * LANES + IDX_DIM + d] = 1.0
            pqi2[h * IDX_DIM + d, h * LANES + 2 * IDX_DIM + d] = 1.0
    pki1 = np.zeros((LANES, LANES), np.float32)
    pki2 = np.zeros((LANES, LANES), np.float32)
    for d in range(IDX_DIM):
        pki1[d, d] = 1.0
        pki1[d, 2 * IDX_DIM + d] = 1.0
        pki2[d, IDX_DIM + d] = 1.0
    def rope_place(period, rot, first_src):
        p = np.zeros((LANES, LANES), np.float32)
        for j in range(LANES):
            jj = j % period
            p[first_src + jj % (rot // 2) if jj < rot else ROPE_ZERO_LANE, j] = 1.0
        return p

    mats = (block_diag(A_Q), block_diag(A_KV), pad_heads(A_HEADS), pad_heads(A_KV_HEADS), pqi1, pqi2, pki1, pki2,
            rope_place(A_HEAD_DIM, A_ROT, 0), rope_place(IDX_DIM, IDX_ROT, A_ROT // 2))
    return tuple(np.asarray(m) for m in mats)


def _prep(proj, small, pos2, seq, q_g, k_g):
    ntok = proj.shape[0]
    tm = min(512, seq)
    consts = [jnp.asarray(m, BF16) for m in _prep_constants()]
    row = lambda w, c: pl.BlockSpec((tm, w), lambda i: (i, c))
    full = lambda a: pl.BlockSpec(a.shape, lambda i: (0, 0))
    qg = jnp.tile(q_g, A_HEADS).reshape(1, A_Q)
    kg = jnp.tile(k_g, A_KV_HEADS).reshape(1, A_KV)
    return pl.pallas_call(
        _prep_kernel,
        grid=(ntok // tm,),
        in_specs=[row(A_Q, P_Q // A_Q), row(A_KV, P_K // A_KV), row(A_KV, P_V // A_KV), row(IDX_Q, S_QI // IDX_Q),
                  row(LANES, S_KW // LANES), row(1, 0), full(qg), full(kg)] + [full(m) for m in consts],
        out_specs=[pl.BlockSpec((tm * A_HEADS, LANES), lambda i: (i, 0)), row(A_KV_HEADS * LANES, 0),
                   pl.BlockSpec((A_KV_HEADS * LANES, tm), lambda i: (0, i)),
                   pl.BlockSpec((tm * IDX_HEADS, LANES), lambda i: (i, 0)), row(LANES, 0)],
        out_shape=[jax.ShapeDtypeStruct((ntok * A_HEADS, LANES), BF16),
                   jax.ShapeDtypeStruct((ntok, A_KV_HEADS * LANES), BF16),
                   jax.ShapeDtypeStruct((A_KV_HEADS * LANES, ntok), BF16),
                   jax.ShapeDtypeStruct((ntok * IDX_HEADS, LANES), BF16),
                   jax.ShapeDtypeStruct((ntok, LANES), BF16)],
        compiler_params=_params("arbitrary"),
        name="attn_prep",
    )(proj, proj, proj, small, small, pos2, qg, kg, *consts)


def _sortable(x):
    bits = pltpu.bitcast(jnp.where(x == 0.0, 0.0, x), I32)
    return jnp.where(bits < 0, bits ^ jnp.int32(0x7FFFFFFF), bits)


KEY_STEP = 512
KEY_SUBS = KEY_STEP // Q_BLOCK


def _key_row0(step, sub):
    return pl.multiple_of(step * KEY_STEP + sub * Q_BLOCK, Q_BLOCK)


MXU_LOOP_WIDTHS = (4, 2, 1)


def _loop_steps(nstep, body, init, widths=(2, 1)):
    carry, start = init, jnp.int32(0)
    for w in widths:
        count = lax.div(nstep - start, jnp.int32(w))

        def multi(i, c, w=w, start=start):
            for k in range(w):
                c = body(start + i * w + k, c)
            return c
        carry = lax.fori_loop(0, count, multi, carry)
        start = start + count * w
    return carry


def _count_rows(pred_fn, nstep):
    def body(c, acc):
        for sub in range(KEY_SUBS):
            ones = jnp.where(pred_fn(_key_row0(c, sub)), jnp.int32(1), jnp.int32(0))
            acc = acc + jnp.sum(ones.reshape(Q_BLOCK // SUBLANES, SUBLANES, LANES), axis=0)
        return acc
    acc = _loop_steps(nstep, body, jnp.zeros((SUBLANES, LANES), I32), MXU_LOOP_WIDTHS)
    return jnp.sum(acc, axis=0, keepdims=True)


PAIR = 2 * LANES
N_PAIRS = A_HEADS // 2


def _dsa_kernel(qi_ref, kw_ref, q_ref, ki_ref, k_ref, vt_ref, o_ref, key_ref, s_ref, lim_ref, oacc_ref,
                *, topk, idx_bits):
    qb = pl.program_id(1)
    nstep = lax.div(qb, jnp.int32(KEY_SUBS)) + 1
    row_i = lax.broadcasted_iota(I32, (Q_BLOCK, LANES), 0)
    q_pos = qb * Q_BLOCK + lax.broadcasted_iota(I32, (Q_BLOCK, LANES), 1)

    w_rows = jnp.transpose(kw_ref[...])[IDX_DIM:IDX_DIM + IDX_HEADS, :]
    w_rows = w_rows * np.float32(IDX_HEADS ** -0.5) * np.float32(IDX_DIM ** -0.5)

    def rows(r0):
        return pl.ds(r0, Q_BLOCK)

    def score_body(c, carry):
        for sub in range(KEY_SUBS):
            r0 = _key_row0(c, sub)
            ki = ki_ref[rows(r0), :]
            s = jnp.zeros((Q_BLOCK, LANES), F32)
            for pair in range(IDX_HEADS // 2):
                d = _dot_nt(ki, qi_ref[pair * PAIR:(pair + 1) * PAIR, :])
                for j in range(2):
                    h = 2 * pair + j
                    s = s + jnp.maximum(d[:, j * LANES:(j + 1) * LANES], 0.0) * w_rows[h:h + 1, :]
            key_ref[rows(r0), :] = _sortable(jnp.where((r0 + row_i) <= q_pos, s, NEG_INF))
        return carry
    _loop_steps(nstep, score_body, 0, MXU_LOOP_WIDTHS)

    def bit_body(i, carry):
        cand, n_ge = carry
        trial = cand | lax.shift_left(jnp.int32(1), 31 - i)
        trial_s = trial ^ jnp.int32(INT_MIN)
        cnt = _count_rows(lambda r0: key_ref[rows(r0), :] >= trial_s, nstep)
        take = cnt >= topk
        return jnp.where(take, trial, cand), jnp.where(take, cnt, n_ge)
    n_pass = jnp.where((qb + 1) * Q_BLOCK > topk, 32, 0)
    cand, n_ge = lax.fori_loop(0, n_pass, bit_body,
                               (jnp.zeros((1, LANES), I32), jnp.broadcast_to(nstep * KEY_STEP, (1, LANES))))
    thr = cand ^ jnp.int32(INT_MIN)

    n_gt = _count_rows(lambda r0: key_ref[rows(r0), :] > thr, nstep)
    n_eq = n_ge - n_gt
    need = topk - n_gt
    excess = (n_eq > need) & (thr > jnp.int32(NEG_INF_KEY))
    lim_ref[0:1, :] = jnp.full((1, LANES), 2 ** 30, I32)

    @pl.when(jnp.max(jnp.where(excess, 1, 0)) > 0)
    def _():
        def idx_body(i, lim):
            trial = lim | lax.shift_left(jnp.int32(1), idx_bits - 1 - i)
            cnt = _count_rows(lambda r0: (key_ref[rows(r0), :] == thr) & ((r0 + row_i) < trial), nstep)
            return jnp.where(cnt < need, trial, lim)
        lim = lax.fori_loop(0, idx_bits, idx_body, jnp.zeros((1, LANES), I32))
        lim_ref[0:1, :] = jnp.where(excess, lim, 2 ** 30)
    tie_lim = lim_ref[0:1, :]

    def max_body(c, m_all):
        m_all = list(m_all)
        for sub in range(KEY_SUBS):
            r0 = _key_row0(c, sub)
            keys = key_ref[rows(r0), :]
            kpos = r0 + row_i
            sel = ((keys > thr) | ((keys == thr) & (kpos <= tie_lim))) & (kpos <= q_pos)
            bias = jnp.where(sel, 0.0, NEG_INF)
            bias2 = jnp.concatenate([bias, bias], axis=-1)
            for pair in range(N_PAIRS):
                g = (2 * pair) // A_REP
                s = _dot_nt(k_ref[rows(r0), g * LANES:(g + 1) * LANES], q_ref[pair * PAIR:(pair + 1) * PAIR, :]) + bias2
                s_ref[rows(r0), pair * PAIR:(pair + 1) * PAIR] = s
                m_all[pair] = jnp.maximum(m_all[pair], jnp.max(s, axis=0, keepdims=True))
        return tuple(m_all)
    m_fin = _loop_steps(nstep, max_body, tuple(jnp.full((1, PAIR), -1e30, F32) for _ in range(N_PAIRS)),
                        MXU_LOOP_WIDTHS)

    oacc_ref[...] = jnp.zeros(oacc_ref.shape, F32)

    def pv_body(c, carry):
        cols = pl.ds(pl.multiple_of(c * KEY_STEP, KEY_STEP), KEY_STEP)
        for pair in range(N_PAIRS):
            g = (2 * pair) // A_REP
            p = jnp.exp(s_ref[cols, pair * PAIR:(pair + 1) * PAIR] - m_fin[pair]).astype(BF16)
            oacc_ref[pair] += _dot(vt_ref[g * LANES:(g + 1) * LANES, cols], p)
        return carry
    _loop_steps(nstep, pv_body, 0, MXU_LOOP_WIDTHS)

    outs = []
    for pair in range(N_PAIRS):
        o_p = oacc_ref[pair]
        for j in range(2):
            o_h = o_p[:, j * LANES:(j + 1) * LANES]
            outs.append(o_h[0:A_HEAD_DIM, :] / o_h[A_HEAD_DIM:A_HEAD_DIM + 1, :])
    o_ref[...] = jnp.transpose(jnp.concatenate(outs, axis=0)).astype(BF16)


def _dsa(qi_cat, small, q_pad, ki_cat, k_pad, v_t, bsz, seq):
    assert seq % KEY_STEP == 0
    nb = seq // Q_BLOCK
    topk = min(IDX_TOPK_MAX, seq // 4)
    idx_bits = max(1, int(math.ceil(math.log2(seq))))
    qrow = lambda w, c: pl.BlockSpec((Q_BLOCK, w), lambda b, j: (b * nb + j, c))
    brow = lambda w: pl.BlockSpec((seq, w), lambda b, j: (b, 0))
    return pl.pallas_call(
        functools.partial(_dsa_kernel, topk=topk, idx_bits=idx_bits),
        grid=(bsz, nb),
        in_specs=[pl.BlockSpec((IDX_HEADS * Q_BLOCK, LANES), lambda b, j: (b * nb + j, 0)),
                  qrow(LANES, S_KW // LANES),
                  pl.BlockSpec((A_HEADS * Q_BLOCK, LANES), lambda b, j: (b * nb + j, 0)),
                  brow(LANES), brow(A_KV_HEADS * LANES),
                  pl.BlockSpec((A_KV_HEADS * LANES, seq), lambda b, j: (0, b))],
        out_specs=qrow(A_Q, 0),
        out_shape=jax.ShapeDtypeStruct((bsz * seq, A_Q), BF16),
        scratch_shapes=[pltpu.VMEM((seq, LANES), I32), pltpu.VMEM((seq, A_HEADS * LANES), F32),
                        pltpu.VMEM((SUBLANES, LANES), I32), pltpu.VMEM((N_PAIRS, LANES, PAIR), F32)],
        compiler_params=_params("arbitrary", "arbitrary"),
        name="sparse_attention",
    )(qi_cat, small, q_pad, ki_cat, k_pad, v_t)


HALO = SUBLANES


def _causal_conv(ext_ref, u, w_ref, taps, rows):
    ext_ref[HALO:HALO + rows, :] = u
    y = u * w_ref[taps - 1:taps, :]
    for j in range(taps - 1):
        shift = taps - 1 - j
        y = y + ext_ref[HALO - shift:HALO - shift + rows, :] * w_ref[j:j + 1, :]
    ext_ref[0:HALO, :] = ext_ref[rows:rows + HALO, :]
    return y


def _mix_kernel(x_ref, g0_ref, g1_ref, g2_ref, g3_ref, guv_ref, scb_ref, scc_ref, scx_ref, mz_ref, xbc_ref,
                dt_ref, oa_ref, gate_ref, scw_ref, gvg_ref, ws_ref, wsb_ref, mcw_ref, mcb_ref, dtb_ref, alog_ref,
                md_ref, mng_ref, wb_ref, wo_ref, o_ref, sc_ext, xbc_ext, state_ref, y_ref, *, rows):
    @pl.when(pl.program_id(1) == 0)
    def _():
        sc_ext[0:HALO, :] = jnp.zeros((HALO, SC_W), F32)
        xbc_ext[0:HALO, :] = jnp.zeros((HALO, M_XBC), F32)
        state_ref[...] = jnp.zeros(state_ref.shape, F32)

    nchunk = rows // M_CHUNK
    row_i = lax.broadcasted_iota(I32, (M_CHUNK, M_CHUNK), 0)
    col_i = lax.broadcasted_iota(I32, (M_CHUNK, M_CHUNK), 1)
    tri = row_i >= col_i

    o_b = scb_ref[...].astype(F32) * _causal_conv(
        sc_ext, scc_ref[...].astype(F32) * scx_ref[...].astype(F32), scw_ref, SC_KERNEL, rows)

    guv = _gelu(guv_ref[...].astype(F32))
    gu = guv[:, :GM_W]
    gv = _rms(guv[:, GM_W:], gvg_ref[...]).astype(BF16)
    mixed_rows = []
    for ci in range(nchunk):
        r0 = ci * GM_CHUNK
        cols = []
        for gi in range(GM_GROUPS):
            vb = gv[r0:r0 + GM_CHUNK, gi * GM_GROUP_W:(gi + 1) * GM_GROUP_W]
            cols.append(_dot(ws_ref[gi], vb) + wsb_ref[:, gi:gi + 1])
        mixed_rows.append(jnp.concatenate(cols, axis=-1))
    o_c = gu * jnp.concatenate(mixed_rows, axis=0)

    xbc = _silu(_causal_conv(xbc_ext, xbc_ref[...].astype(F32), mcw_ref, M_CONV, rows) + mcb_ref[...])
    head_lane = lax.broadcasted_iota(I32, (1, LANES), 1) < M_HEADS
    dt = jnp.where(head_lane, _softplus(dt_ref[...] + dtb_ref[...]), 0.0)
    a_row = jnp.where(head_lane, -jnp.exp(alog_ref[...]), 0.0)
    tri_f = jnp.where(tri, 1.0, 0.0).astype(F32)
    for ci in range(nchunk):
        r0 = ci * M_CHUNK
        dtc = dt[r0:r0 + M_CHUNK, :]
        acs = _dot_f32(tri_f, dtc * a_row)
        acs_t = jnp.transpose(acs)
        b_t = jnp.transpose(xbc[r0:r0 + M_CHUNK, M_DINNER:M_DINNER + M_BC]).astype(BF16)
        cm = xbc[r0:r0 + M_CHUNK, M_DINNER + M_BC:M_XBC].astype(BF16)
        for g in range(M_GROUPS):
            bg_t = b_t[g * M_STATE:(g + 1) * M_STATE, :]
            cg = cm[:, g * M_STATE:(g + 1) * M_STATE]
            cb = _dot(cg, bg_t)
            for h in range(g * (M_HEADS // M_GROUPS), (g + 1) * (M_HEADS // M_GROUPS)):
                col = acs[:, h:h + 1]
                row = acs_t[h:h + 1, :]
                last = acs[M_CHUNK - 1:M_CHUNK, h:h + 1]
                decay = jnp.exp(jnp.where(tri, col - row, NEG_INF))
                xs_h = xbc[r0:r0 + M_CHUNK, h * M_HEAD_DIM:(h + 1) * M_HEAD_DIM]
                xdt = xs_h * dtc[:, h:h + 1]
                y = _dot((cb * decay).astype(BF16), xdt.astype(BF16))
                prev_t = state_ref[h]
                y = y + _dot(cg, prev_t.astype(BF16)) * jnp.exp(col)
                st_t = _dot(bg_t, (xdt * jnp.exp(last - col)).astype(BF16))
                state_ref[h] = prev_t * jnp.exp(last) + st_t
                y_ref[r0:r0 + M_CHUNK, h * M_HEAD_DIM:(h + 1) * M_HEAD_DIM] = y + md_ref[:, h:h + 1] * xs_h
    o_d = _rms(y_ref[...] * _silu(mz_ref[...].astype(F32)), mng_ref[...])

    branches = (oa_ref[...], o_b.astype(BF16), o_c.astype(BF16), o_d.astype(BF16))
    gate_refs = (g0_ref, g1_ref, g2_ref, g3_ref)
    merged = None
    for i in range(N_BRANCH):
        term = _sigmoid(gate_refs[i][...].astype(F32)) * _dot(branches[i], wb_ref[i])
        merged = term if merged is None else merged + term
    o_ref[...] = x_ref[...] + gate_ref[0] * _dot(merged.astype(BF16), wo_ref[...])


def _mix(x2, proj, small, o_a, gate1, lw, bsz, seq):
    d = x2.shape[1]
    rows = min(256, seq)
    ns = seq // rows
    tok = lambda w, c: pl.BlockSpec((rows, w), lambda b, s: (b * ns + s, c))
    full = lambda a: pl.BlockSpec(a.shape, lambda b, s: (0,) * a.ndim)
    weights = [lw["sc_conv_w"], lw["g_v_norm_g"], lw["ws"], lw["ws_b"], lw["m_conv_w"], lw["m_conv_b"],
               lw["m_dt_bias"], lw["m_a_log"], lw["m_d"], lw["m_norm_g"], lw["w_branch"], lw["w_out"]]
    return pl.pallas_call(
        functools.partial(_mix_kernel, rows=rows),
        grid=(bsz, ns),
        in_specs=[tok(d, 0)] + [tok(D_MODEL, i) for i in range(N_BRANCH)]
        + [tok(2 * GM_W, P_GUV // (2 * GM_W)), tok(SC_W, P_SCB // SC_W), tok(SC_W, P_SCC // SC_W),
           tok(SC_W, P_SCX // SC_W), tok(M_DINNER, P_MZ // M_DINNER), tok(M_XBC, P_XBC // M_XBC),
           tok(LANES, S_DT // LANES), tok(A_Q, 0),
           pl.BlockSpec((1, 1, d), lambda b, s: (b, 0, 0))]
        + [full(w) for w in weights],
        out_specs=tok(d, 0),
        out_shape=jax.ShapeDtypeStruct(x2.shape, F32),
        scratch_shapes=[pltpu.VMEM((HALO + rows, SC_W), F32), pltpu.VMEM((HALO + rows, M_XBC), F32),
                        pltpu.VMEM((M_HEADS, M_STATE, M_HEAD_DIM), F32), pltpu.VMEM((rows, M_DINNER), F32)],
        compiler_params=_params("arbitrary", "arbitrary"),
        name="mixers_merge",
    )(x2, proj, proj, proj, proj, proj, proj, proj, proj, proj, proj, small, o_a,
      gate1.reshape(bsz, 1, d), *weights)


ROUTE_ROWS = EXPERTS_PER_GROUP * SUBLANES
BEST_ROW = ROUTE_ROWS


def _route_row(e):
    return (e % EXPERTS_PER_GROUP) * SUBLANES + e // EXPERTS_PER_GROUP


def _route(scores_t, bias_col):
    ntok = scores_t.shape[1]
    row = lax.broadcasted_iota(I32, (SUBLANES, ntok), 0)
    members = range(EXPERTS_PER_GROUP)
    sc = [scores_t[i * SUBLANES:(i + 1) * SUBLANES, :] for i in members]
    x = [sc[i] + bias_col[i * SUBLANES:(i + 1) * SUBLANES, :] for i in members]
    hi1, lo1 = jnp.maximum(x[0], x[1]), jnp.minimum(x[0], x[1])
    hi2, lo2 = jnp.maximum(x[2], x[3]), jnp.minimum(x[2], x[3])
    grp = jnp.maximum(hi1, hi2) + jnp.maximum(jnp.minimum(hi1, hi2), jnp.maximum(lo1, lo2))
    grp = jnp.where(row < N_EXPERT_GROUPS, grp, NEG_INF)
    best_val = jnp.max(grp, axis=0, keepdims=True)
    best = jnp.min(jnp.where(grp == best_val, row, SUBLANES), axis=0, keepdims=True)
    in_best = row == best
    y = [jnp.max(jnp.where(in_best, x[i], NEG_INF), axis=0, keepdims=True) for i in members]
    s = [jnp.sum(jnp.where(in_best, sc[i], 0.0), axis=0, keepdims=True) for i in members]

    def first_argmax(vals):
        top = functools.reduce(jnp.maximum, vals)
        idx = jnp.full(top.shape, len(vals) - 1, I32)
        for i in reversed(range(len(vals) - 1)):
            idx = jnp.where(vals[i] == top, i, idx)
        return idx

    i1 = first_argmax(y)
    i2 = first_argmax([jnp.where(i1 == i, NEG_INF, y[i]) for i in members])
    w1 = sum(jnp.where(i1 == i, s[i], 0.0) for i in members)
    w2 = sum(jnp.where(i2 == i, s[i], 0.0) for i in members)
    tot = w1 + w2
    parts = [jnp.where(in_best, jnp.where(i1 == i, w1 / tot, 0.0) + jnp.where(i2 == i, w2 / tot, 0.0), 0.0)
             for i in members]
    parts.append(jnp.broadcast_to(best.astype(F32), (SUBLANES, ntok)))
    parts.append(jnp.zeros((LANES - (ROUTE_ROWS + SUBLANES), ntok), F32))
    return jnp.concatenate(parts, axis=0)


MOE_CHUNK = 128


def _dot_tn(a, b):
    return lax.dot_general(a, b, (((0,), (0,)), ((), ())), preferred_element_type=F32)


def _moe_kernel(x_ref, g_ref, sc_ref, sh_ref, gate_ref, rwh_ref, rwl_ref, rb_ref, tri_ref, wgu_ref, wd_ref, o_ref,
                h_ref, c2_ref, grp_ref, rank_ref, tot_ref):
    grp_id = pl.program_id(1)
    tm = x_ref.shape[0]
    lane = lax.broadcasted_iota(I32, (tm, LANES), 1)

    @pl.when(grp_id == 0)
    def _():
        h = _rms(x_ref[...], g_ref[...]) * (1.0 + sc_ref[0]) + sh_ref[0]
        hh, hl = _split(h)
        h_ref[...] = hh
        logits_t = (_dot_nt(rwh_ref[...], hh) + _dot_nt(rwh_ref[...], hl) + _dot_nt(rwl_ref[...], hh))
        routed = jnp.transpose(_route(_sigmoid(logits_t[:ROUTE_ROWS, :]), rb_ref[...]))
        best = routed[:, BEST_ROW:BEST_ROW + 1].astype(I32)
        comb = jnp.where(lane < ROUTE_ROWS, routed, 0.0)
        c_hi, c_lo = _split(comb)
        c2_ref[:, :LANES] = c_hi
        c2_ref[:, LANES:] = c_lo
        grp_ref[...] = jnp.broadcast_to(best, (tm, LANES))
        onehot = jnp.where(lane == best, 1.0, 0.0)
        offs = jnp.zeros((1, LANES), F32)
        for b in range(tm // MOE_CHUNK):
            blk = onehot[b * MOE_CHUNK:(b + 1) * MOE_CHUNK, :]
            ranks = _dot(tri_ref[...], blk.astype(BF16)) + offs
            own = jnp.sum(blk * ranks, axis=-1, keepdims=True)
            rank_ref[b * MOE_CHUNK:(b + 1) * MOE_CHUNK, :] = jnp.broadcast_to(own, (MOE_CHUNK, LANES))
            offs = offs + jnp.sum(blk, axis=0, keepdims=True)
        tot_ref[0:1, :] = offs
        o_ref[...] = jnp.zeros(o_ref.shape, F32)

    lane1 = lax.broadcasted_iota(I32, (1, LANES), 1)
    n_tok = jnp.sum(jnp.where(lane1 == grp_id, tot_ref[0:1, :], 0.0)).astype(I32)
    n_chunks = lax.div(n_tok + (MOE_CHUNK - 1), jnp.int32(MOE_CHUNK))
    lane_f = lane.astype(F32)
    lane_c = lax.broadcasted_iota(I32, (MOE_CHUNK, LANES), 1)

    def chunk_body(c, carry):
        base = (c * MOE_CHUNK).astype(F32)
        pick = ((rank_ref[...] - base) == lane_f) & (grp_ref[...] == grp_id)
        st = jnp.where(pick, 1.0, 0.0).astype(BF16)
        xc = _dot_tn(st, h_ref[...]).astype(BF16)
        wc2 = _dot_tn(st, c2_ref[...])
        wc = wc2[:, :LANES] + wc2[:, LANES:]
        y = jnp.zeros((MOE_CHUNK, x_ref.shape[1]), F32)
        for el in range(EXPERTS_PER_GROUP):
            gu = _dot(xc, wgu_ref[el])
            hid = _silu(gu[:, :D_FF_EXPERT]) * gu[:, D_FF_EXPERT:]
            w_e = jnp.sum(jnp.where(lane_c == el * SUBLANES + grp_id, wc, 0.0), axis=-1, keepdims=True)
            y = y + w_e * _dot(hid.astype(BF16), wd_ref[el])
        y_hi, y_lo = _split(y)
        o_ref[...] += _dot(jnp.concatenate([st, st], axis=1), jnp.concatenate([y_hi, y_lo], axis=0))
        return carry
    _loop_steps(n_chunks, chunk_body, 0)

    @pl.when(grp_id == pl.num_programs(1) - 1)
    def _():
        o_ref[...] = x_ref[...] + gate_ref[0] * o_ref[...]


def _moe(x2, seq, g, scale, shift, gate, rw_hi, rw_lo, rb, wgu, wd):
    ntok, d = x2.shape
    bsz = scale.shape[0]
    tm = min(1024, seq)
    row = lambda i, e: (i, 0)
    per_b = lambda i, e: ((i * tm) // seq, 0, 0)
    vec = pl.BlockSpec((1, 1, d), per_b)
    const2 = lambda a: pl.BlockSpec(a.shape, lambda i, e: (0, 0))
    tri = jnp.asarray(np.tril(np.ones((MOE_CHUNK, MOE_CHUNK), np.float32), -1), BF16)
    return pl.pallas_call(
        _moe_kernel,
        grid=(ntok // tm, N_EXPERT_GROUPS),
        in_specs=[pl.BlockSpec((tm, d), row), pl.BlockSpec((1, d), lambda i, e: (0, 0)), vec, vec, vec,
                  const2(rw_hi), const2(rw_lo), const2(rb), const2(tri),
                  pl.BlockSpec((EXPERTS_PER_GROUP, d, 2 * D_FF_EXPERT), lambda i, e: (e, 0, 0)),
                  pl.BlockSpec((EXPERTS_PER_GROUP, D_FF_EXPERT, d), lambda i, e: (e, 0, 0))],
        out_specs=pl.BlockSpec((tm, d), row),
        out_shape=jax.ShapeDtypeStruct(x2.shape, F32),
        scratch_shapes=[pltpu.VMEM((tm, d), BF16), pltpu.VMEM((tm, 2 * LANES), BF16),
                        pltpu.VMEM((tm, LANES), I32), pltpu.VMEM((tm, LANES), F32), pltpu.VMEM((SUBLANES, LANES), F32)],
        compiler_params=_params("arbitrary", "arbitrary"),
        name="moe",
    )(x2, g.reshape(1, d), scale.reshape(bsz, 1, d), shift.reshape(bsz, 1, d), gate.reshape(bsz, 1, d),
      rw_hi, rw_lo, rb, tri, wgu, wd)


def _pad_lanes(v, width=LANES):
    v = v.reshape(1, -1)
    return jnp.pad(v, ((0, 0), (0, width - v.shape[1])))


def _layer_weights(w_in, sc_conv_w, g_v_norm_g, g_spatial_w, g_spatial_b, m_conv_w, m_conv_b, m_dt_bias,
                   m_a_log, m_d, m_norm_g, w_branch, w_out):
    splits = (A_Q, A_KV, A_KV, IDX_Q, IDX_DIM, IDX_HEADS, SC_W, SC_W, SC_W, 2 * GM_W, M_DINNER, M_XBC, M_HEADS,
              N_BRANCH * D_MODEL)
    pts = [int(p) for p in np.cumsum(splits)[:-1]]
    (q, k, v, qi, ki, wi, sc_b, sc_c, sc_x, g_uv, m_z, m_xbc, m_dt, gates) = jnp.split(w_in, pts, axis=-1)
    d = w_in.shape[0]
    w_main = jnp.concatenate([gates, g_uv, q, sc_b, sc_c, sc_x, m_z, m_xbc, k, v], axis=-1).astype(BF16)
    w_small = jnp.concatenate(
        [qi, ki, wi, jnp.zeros((d, LANES - IDX_DIM - IDX_HEADS), F32), m_dt, jnp.zeros((d, LANES - M_HEADS), F32)],
        axis=-1)
    ws_hi = w_small.astype(BF16)
    ws_lo = (w_small - ws_hi.astype(F32)).astype(BF16)
    tri = np.tril(np.ones((GM_CHUNK, GM_CHUNK), dtype=bool))
    return dict(
        w_main=w_main, ws_hi=ws_hi, ws_lo=ws_lo,
        sc_conv_w=sc_conv_w, g_v_norm_g=g_v_norm_g.reshape(1, GM_W),
        ws=jnp.where(tri[None], g_spatial_w, 0.0).astype(BF16), ws_b=g_spatial_b.T,
        m_conv_w=m_conv_w, m_conv_b=m_conv_b.reshape(1, M_XBC),
        m_dt_bias=_pad_lanes(m_dt_bias), m_a_log=_pad_lanes(m_a_log), m_d=_pad_lanes(m_d),
        m_norm_g=m_norm_g.reshape(1, M_DINNER), w_branch=w_branch.astype(BF16), w_out=w_out.astype(BF16))


def kernel(x, c, positions, ada_w, ada_b, norm1_g, w_in, q_norm_g, k_norm_g, sc_conv_w, g_v_norm_g, g_spatial_w,
           g_spatial_b, m_conv_w, m_conv_b, m_dt_bias, m_a_log, m_d, m_norm_g, w_branch, w_out, norm2_g, router_w,
           router_bias, exp_w_gate, exp_w_up, exp_w_down):
    bsz, seq, d = x.shape
    depth = ada_w.shape[0]
    mod = _modulation(c, ada_w, ada_b)
    route_rows = np.array([_route_row(e) for e in range(N_EXPERTS)])
    rw = jnp.zeros((LANES, d), F32).at[route_rows].set(router_w.T)
    rw_hi = rw.astype(BF16)
    rw_lo = (rw - rw_hi.astype(F32)).astype(BF16)
    rb = jnp.zeros((ROUTE_ROWS, 1), F32).at[route_rows, 0].set(router_bias)
    pos2 = positions.reshape(bsz * seq, 1)
    x2 = x.reshape(bsz * seq, d)
    for l in range(depth):
        shift1, scale1, gate1, shift2, scale2, gate2 = jnp.split(mod[l], 6, axis=-1)
        lw = _layer_weights(w_in[l], sc_conv_w[l], g_v_norm_g[l], g_spatial_w[l], g_spatial_b[l], m_conv_w[l],
                            m_conv_b[l], m_dt_bias[l], m_a_log[l], m_d[l], m_norm_g[l], w_branch[l], w_out[l])
        proj, small = _inproj(x2, seq, norm1_g[l], scale1, shift1, lw["w_main"], lw["ws_hi"], lw["ws_lo"])
        q_pad, k_pad, v_t, qi_cat, ki_cat = _prep(proj, small, pos2, seq, q_norm_g[l], k_norm_g[l])
        o_a = _dsa(qi_cat, small, q_pad, ki_cat, k_pad, v_t, bsz, seq)
        x2 = _mix(x2, proj, small, o_a, gate1, lw, bsz, seq)
        wgu = jnp.concatenate([exp_w_gate[l], exp_w_up[l]], axis=-1).astype(BF16)
        x2 = _moe(x2, seq, norm2_g[l], scale2, shift2, gate2, rw_hi, rw_lo, rb, wgu, exp_w_down[l].astype(BF16))
    return x2.reshape(bsz, seq, d)
```

```python
import functools
import math

import jax
import jax.numpy as jnp
import numpy as np
from jax import lax
from jax.experimental import pallas as pl
from jax.experimental.pallas import tpu as pltpu

F32 = jnp.float32
BF16 = jnp.bfloat16
I32 = jnp.int32

D_MODEL = 1024
RMS_EPS = 1e-6
ROPE_THETA = 500000.0
Q_BLOCK = 128
A_HEADS = 8
A_KV_HEADS = 2
A_HEAD_DIM = 64
A_ROT = A_HEAD_DIM // 4
A_REP = A_HEADS // A_KV_HEADS
IDX_HEADS = 8
IDX_DIM = 32
IDX_ROT = IDX_DIM // 4
IDX_TOPK_MAX = 256
SC_W = 512
SC_KERNEL = 3
GM_W = 512
GM_GROUPS = 4
GM_GROUP_W = GM_W // GM_GROUPS
GM_CHUNK = 128
M_DINNER = 512
M_HEAD_DIM = 64
M_HEADS = M_DINNER // M_HEAD_DIM
M_GROUPS = 2
M_STATE = 64
M_CONV = 4
M_CHUNK = 128
M_BC = M_GROUPS * M_STATE
M_XBC = M_DINNER + 2 * M_BC
N_BRANCH = 4
BRANCH_W = 512
N_EXPERTS = 16
N_EXPERT_GROUPS = 4
EXPERTS_PER_GROUP = N_EXPERTS // N_EXPERT_GROUPS
D_FF_EXPERT = 512
A_Q = A_HEADS * A_HEAD_DIM
A_KV = A_KV_HEADS * A_HEAD_DIM
IDX_Q = IDX_HEADS * IDX_DIM

LANES = 128
SUBLANES = 8
VMEM_LIMIT = 56 * 1024 * 1024

P_GATES = 0
P_GUV = 4096
P_Q = 5120
P_SCB = 5632
P_SCC = 6144
P_SCX = 6656
P_MZ = 7168
P_XBC = 7680
P_K = 8448
P_V = 8576
P_COLS = 8704
S_QI = 0
S_KW = 256
S_DT = 384
S_COLS = 512

INT_MIN = -2147483648
NEG_INF = float("-inf")
NEG_INF_KEY = -2139095041


def _dot(a, b):
    return jnp.dot(a, b, preferred_element_type=F32)


def _dot_nt(a, b):
    return lax.dot_general(a, b, (((1,), (1,)), ((), ())), preferred_element_type=F32)


def _dot_f32(a, b):
    return jnp.dot(a, b, preferred_element_type=F32, precision=lax.Precision.HIGHEST)


def _split(x):
    hi = x.astype(BF16)
    lo = (x - hi.astype(F32)).astype(BF16)
    return hi, lo


def _dot3(a_hi, a_lo, b_hi, b_lo):
    return _dot(a_hi, b_hi) + _dot(a_lo, b_hi) + _dot(a_hi, b_lo)


def _sigmoid(x):
    return jax.nn.sigmoid(x)


def _silu(x):
    return x * _sigmoid(x)


def _softplus(x):
    return jnp.maximum(x, 0.0) + jnp.log1p(jnp.exp(-jnp.abs(x)))


def _gelu(x):
    return 0.5 * x * (1.0 + lax.erf(x * np.float32(np.sqrt(0.5))))


def _rms(x, g_row):
    ms = jnp.mean(x * x, axis=-1, keepdims=True)
    return x * lax.rsqrt(ms + RMS_EPS) * g_row


def _params(*sem):
    return pltpu.CompilerParams(dimension_semantics=sem, vmem_limit_bytes=VMEM_LIMIT)


def _mod_kernel(c_ref, w_ref, b_ref, o_ref):
    o_ref[0] = _dot_f32(_silu(c_ref[...]), w_ref[0]) + b_ref[0]


def _modulation(c, ada_w, ada_b):
    depth, d, n = ada_w.shape
    bsz = c.shape[0]
    tn = 768
    return pl.pallas_call(
        _mod_kernel,
        grid=(depth, n // tn),
        in_specs=[pl.BlockSpec((bsz, d), lambda l, j: (0, 0)),
                  pl.BlockSpec((1, d, tn), lambda l, j: (l, 0, j)),
                  pl.BlockSpec((1, 1, tn), lambda l, j: (l, 0, j))],
        out_specs=pl.BlockSpec((1, bsz, tn), lambda l, j: (l, 0, j)),
        out_shape=jax.ShapeDtypeStruct((depth, bsz, n), F32),
        compiler_params=_params("arbitrary", "arbitrary"),
        name="modulation",
    )(c, ada_w, ada_b.reshape(depth, 1, n))


def _inproj_kernel(x_ref, g_ref, sc_ref, sh_ref, w_ref, wsh_ref, wsl_ref, o_ref, os_ref, hh_ref, hl_ref):
    @pl.when(pl.program_id(1) == 0)
    def _():
        h = _rms(x_ref[...], g_ref[...]) * (1.0 + sc_ref[0]) + sh_ref[0]
        hh, hl = _split(h)
        hh_ref[...] = hh
        hl_ref[...] = hl
        os_ref[...] = _dot3(hh, hl, wsh_ref[...], wsl_ref[...])

    o_ref[...] = _dot(hh_ref[...], w_ref[...]).astype(BF16)


def _inproj(x2, seq, g, scale, shift, w_main, ws_hi, ws_lo):
    ntok, d = x2.shape
    tm = min(1024, seq)
    tn = P_COLS // 4
    bsz = scale.shape[0]
    row = lambda i, j: (i, 0)
    per_b = lambda i, j: ((i * tm) // seq, 0, 0)
    return pl.pallas_call(
        _inproj_kernel,
        grid=(ntok // tm, P_COLS // tn),
        in_specs=[pl.BlockSpec((tm, d), row),
                  pl.BlockSpec((1, d), lambda i, j: (0, 0)),
                  pl.BlockSpec((1, 1, d), per_b),
                  pl.BlockSpec((1, 1, d), per_b),
                  pl.BlockSpec((d, tn), lambda i, j: (0, j)),
                  pl.BlockSpec((d, S_COLS), lambda i, j: (0, 0)),
                  pl.BlockSpec((d, S_COLS), lambda i, j: (0, 0))],
        out_specs=[pl.BlockSpec((tm, tn), lambda i, j: (i, j)),
                   pl.BlockSpec((tm, S_COLS), row)],
        out_shape=[jax.ShapeDtypeStruct((ntok, P_COLS), BF16),
                   jax.ShapeDtypeStruct((ntok, S_COLS), F32)],
        scratch_shapes=[pltpu.VMEM((tm, d), BF16), pltpu.VMEM((tm, d), BF16)],
        compiler_params=_params("arbitrary", "arbitrary"),
        name="inproj",
    )(x2, g.reshape(1, d), scale.reshape(bsz, 1, d), shift.reshape(bsz, 1, d), w_main, ws_hi, ws_lo)


ROPE_ZERO_LANE = A_ROT // 2 + IDX_ROT // 2


def _rope_angles(pos_f):
    half_a, half_i = A_ROT // 2, IDX_ROT // 2
    lane = lax.broadcasted_iota(I32, (1, LANES), 1)
    k = jnp.where(lane < half_a, lane, lane - half_a).astype(F32)
    rot = jnp.where(lane < half_a, np.float32(A_ROT), np.float32(IDX_ROT))
    inv_freq = jnp.exp(np.float32(-math.log(ROPE_THETA)) * k * 2.0 / rot)
    ang = pos_f * jnp.where(lane < half_a + half_i, inv_freq, 0.0)
    return jnp.cos(ang), jnp.sin(ang)


def _rope_tables(cos_sin, place_ref, period, rot, limit=LANES):
    half = rot // 2

    def expand(x):
        hi, lo = _split(x)
        return _dot(hi, place_ref[...]) + _dot(lo, place_ref[...])

    lane_all = lax.broadcasted_iota(I32, (1, LANES), 1)
    lane = lane_all % period
    cos = jnp.where(lane_all < limit, expand(cos_sin[0]), 1.0)
    sin = jnp.where(lane_all < limit, expand(cos_sin[1]), 0.0)
    sin_lo = jnp.where(lane < half, -sin, 0.0)
    sin_hi = jnp.where((lane >= half) & (lane < rot), sin, 0.0)
    return cos, sin_lo, sin_hi


def _rope(x, tables, half):
    cos, sin_lo, sin_hi = tables
    width = x.shape[-1]
    return (x * cos + pltpu.roll(x, width - half, 1) * sin_lo + pltpu.roll(x, half, 1) * sin_hi)


def _prep_kernel(q_ref, k_ref, v_ref, qi_ref, kw_ref, pos_ref, qg_ref, kg_ref, bdq_ref, bdk_ref,
                 pq_ref, pk_ref, pqi1_ref, pqi2_ref, pki1_ref, pki2_ref, pra_ref, pri_ref,
                 qo_ref, ko_ref, vto_ref, qio_ref, kio_ref):
    cos_sin = _rope_angles(pos_ref[...].astype(F32))
    att_tab = _rope_tables(cos_sin, pra_ref, A_HEAD_DIM, A_ROT)
    idx_tab = _rope_tables(cos_sin, pri_ref, IDX_DIM, IDX_ROT)
    kw_tab = _rope_tables(cos_sin, pri_ref, IDX_DIM, IDX_ROT, limit=IDX_DIM)

    def head_norm(x, bd_ref, g_ref):
        sq_hi, sq_lo = _split(x * x)
        ms = (_dot(sq_hi, bd_ref[...]) + _dot(sq_lo, bd_ref[...])) * np.float32(1.0 / A_HEAD_DIM)
        return x * lax.rsqrt(ms + RMS_EPS) * g_ref[...]

    def rope_wide(x, tab, half):
        pieces = [_rope(x[:, c * LANES:(c + 1) * LANES], tab, half) for c in range(x.shape[-1] // LANES)]
        return pieces[0] if len(pieces) == 1 else jnp.concatenate(pieces, axis=-1)

    def store_head_rows(o_ref, x):
        for blk in range(x.shape[0] // Q_BLOCK):
            for h in range(x.shape[1] // LANES):
                r0 = (blk * (x.shape[1] // LANES) + h) * Q_BLOCK
                o_ref[r0:r0 + Q_BLOCK, :] = x[blk * Q_BLOCK:(blk + 1) * Q_BLOCK, h * LANES:(h + 1) * LANES]

    q = rope_wide(head_norm(q_ref[...].astype(F32), bdq_ref, qg_ref), att_tab, A_ROT // 2)
    q = (q * np.float32(A_HEAD_DIM ** -0.5)).astype(BF16)
    store_head_rows(qo_ref, _dot(q, pq_ref[...]).astype(BF16))
    k = rope_wide(head_norm(k_ref[...].astype(F32), bdk_ref, kg_ref), att_tab, A_ROT // 2).astype(BF16)
    ko_ref[...] = _dot(k, pk_ref[...]).astype(BF16)
    v_pad = _dot(v_ref[...], pk_ref[...])
    ones_lane = (lax.broadcasted_iota(I32, (1, A_KV_HEADS * LANES), 1) % LANES) == A_HEAD_DIM
    vto_ref[...] = jnp.transpose(jnp.where(ones_lane, 1.0, v_pad)).astype(BF16)

    qi_hi, qi_lo = _split(rope_wide(qi_ref[...], idx_tab, IDX_ROT // 2))
    store_head_rows(qio_ref, (_dot(qi_hi, pqi1_ref[...]) + _dot(qi_lo, pqi2_ref[...])).astype(BF16))
    ki_hi, ki_lo = _split(_rope(kw_ref[...], kw_tab, IDX_ROT // 2))
    kio_ref[...] = (_dot(ki_hi, pki1_ref[...]) + _dot(ki_lo, pki2_ref[...])).astype(BF16)


@functools.lru_cache(maxsize=None)
def _prep_constants():
    def block_diag(width):
        i = np.arange(width)
        return (i[:, None] // A_HEAD_DIM == i[None, :] // A_HEAD_DIM).astype(np.float32)

    def pad_heads(n_heads):
        p = np.zeros((n_heads * A_HEAD_DIM, n_heads * LANES), np.float32)
        for h in range(n_heads):
            for d in range(A_HEAD_DIM):
                p[h * A_HEAD_DIM + d, h * LANES + d] = 1.0
        return p

    pqi1 = np.zeros((IDX_Q, IDX_HEADS * LANES), np.float32)
    pqi2 = np.zeros((IDX_Q, IDX_HEADS * LANES), np.float32)
    for h in range(IDX_HEADS):
        for d in range(IDX_DIM):
            pqi1[h * IDX_DIM + d, h * LANES + d] = 1.0
            pqi1[h * IDX_DIM + d, h * LANES + IDX_DIM + d] = 1.0
            pqi2[h * IDX_DIM + d, h * LANES + 2 * IDX_DIM + d] = 1.0
    pki1 = np.zeros((LANES, LANES), np.float32)
    pki2 = np.zeros((LANES, LANES), np.float32)
    for d in range(IDX_DIM):
        pki1[d, d] = 1.0
        pki1[d, 2 * IDX_DIM + d] = 1.0
        pki2[d, IDX_DIM + d] = 1.0
    def rope_place(period, rot, first_src):
        p = np.zeros((LANES, LANES), np.float32)
        for j in range(LANES):
            jj = j % period
            p[first_src + jj % (rot // 2) if jj < rot else ROPE_ZERO_LANE, j] = 1.0
        return p

    mats = (block_diag(A_Q), block_diag(A_KV), pad_heads(A_HEADS), pad_heads(A_KV_HEADS), pqi1, pqi2, pki1, pki2,
            rope_place(A_HEAD_DIM, A_ROT, 0), rope_place(IDX_DIM, IDX_ROT, A_ROT // 2))
    return tuple(np.asarray(m) for m in mats)


def _prep(proj, small, pos2, seq, q_g, k_g):
    ntok = proj.shape[0]
    tm = min(512, seq)
    consts = [jnp.asarray(m, BF16) for m in _prep_constants()]
    row = lambda w, c: pl.BlockSpec((tm, w), lambda i: (i, c))
    full = lambda a: pl.BlockSpec(a.shape, lambda i: (0, 0))
    qg = jnp.tile(q_g, A_HEADS).reshape(1, A_Q)
    kg = jnp.tile(k_g, A_KV_HEADS).reshape(1, A_KV)
    return pl.pallas_call(
        _prep_kernel,
        grid=(ntok // tm,),
        in_specs=[row(A_Q, P_Q // A_Q), row(A_KV, P_K // A_KV), row(A_KV, P_V // A_KV), row(IDX_Q, S_QI // IDX_Q),
                  row(LANES, S_KW // LANES), row(1, 0), full(qg), full(kg)] + [full(m) for m in consts],
        out_specs=[pl.BlockSpec((tm * A_HEADS, LANES), lambda i: (i, 0)), row(A_KV_HEADS * LANES, 0),
                   pl.BlockSpec((A_KV_HEADS * LANES, tm), lambda i: (0, i)),
                   pl.BlockSpec((tm * IDX_HEADS, LANES), lambda i: (i, 0)), row(LANES, 0)],
        out_shape=[jax.ShapeDtypeStruct((ntok * A_HEADS, LANES), BF16),
                   jax.ShapeDtypeStruct((ntok, A_KV_HEADS * LANES), BF16),
                   jax.ShapeDtypeStruct((A_KV_HEADS * LANES, ntok), BF16),
                   jax.ShapeDtypeStruct((ntok * IDX_HEADS, LANES), BF16),
                   jax.ShapeDtypeStruct((ntok, LANES), BF16)],
        compiler_params=_params("arbitrary"),
        name="attn_prep",
    )(proj, proj, proj, small, small, pos2, qg, kg, *consts)


def _sortable(x):
    bits = pltpu.bitcast(jnp.where(x == 0.0, 0.0, x), I32)
    return jnp.where(bits < 0, bits ^ jnp.int32(0x7FFFFFFF), bits)


KEY_STEP = 512
KEY_SUBS = KEY_STEP // Q_BLOCK


def _key_row0(step, sub):
    return pl.multiple_of(step * KEY_STEP + sub * Q_BLOCK, Q_BLOCK)


MXU_LOOP_WIDTHS = (4, 2, 1)


def _loop_steps(nstep, body, init, widths=(2, 1)):
    carry, start = init, jnp.int32(0)
    for w in widths:
        count = lax.div(nstep - start, jnp.int32(w))

        def multi(i, c, w=w, start=start):
            for k in range(w):
                c = body(start + i * w + k, c)
            return c
        carry = lax.fori_loop(0, count, multi, carry)
        start = start + count * w
    return carry


def _count_rows(pred_fn, nstep):
    def body(c, acc):
        for sub in range(KEY_SUBS):
            ones = jnp.where(pred_fn(_key_row0(c, sub)), jnp.int32(1), jnp.int32(0))
            acc = acc + jnp.sum(ones.reshape(Q_BLOCK // SUBLANES, SUBLANES, LANES), axis=0)
        return acc
    acc = _loop_steps(nstep, body, jnp.zeros((SUBLANES, LANES), I32), MXU_LOOP_WIDTHS)
    return jnp.sum(acc, axis=0, keepdims=True)


RADIX_HEAD = 26
PAIR = 2 * LANES
N_PAIRS = A_HEADS // 2


def _dsa_kernel(qi_ref, kw_ref, q_ref, ki_ref, k_ref, vt_ref, o_ref, key_ref, s_ref, lim_ref, oacc_ref,
                *, topk, idx_bits):
    qb = pl.program_id(1)
    nstep = lax.div(qb, jnp.int32(KEY_SUBS)) + 1
    row_i = lax.broadcasted_iota(I32, (Q_BLOCK, LANES), 0)
    q_pos = qb * Q_BLOCK + lax.broadcasted_iota(I32, (Q_BLOCK, LANES), 1)

    w_rows = jnp.transpose(kw_ref[...])[IDX_DIM:IDX_DIM + IDX_HEADS, :]
    w_rows = w_rows * np.float32(IDX_HEADS ** -0.5) * np.float32(IDX_DIM ** -0.5)

    def rows(r0):
        return pl.ds(r0, Q_BLOCK)

    def score_body(c, carry):
        for sub in range(KEY_SUBS):
            r0 = _key_row0(c, sub)
            ki = ki_ref[rows(r0), :]
            s = jnp.zeros((Q_BLOCK, LANES), F32)
            for pair in range(IDX_HEADS // 2):
                d = _dot_nt(ki, qi_ref[pair * PAIR:(pair + 1) * PAIR, :])
                for j in range(2):
                    h = 2 * pair + j
                    s = s + jnp.maximum(d[:, j * LANES:(j + 1) * LANES], 0.0) * w_rows[h:h + 1, :]
            key_ref[rows(r0), :] = _sortable(jnp.where((r0 + row_i) <= q_pos, s, NEG_INF))
        return carry
    _loop_steps(nstep, score_body, 0, MXU_LOOP_WIDTHS)

    def bit_body(i, carry):
        cand, n_ge, n_rej = carry
        trial = cand | lax.shift_left(jnp.int32(1), 31 - i)
        trial_s = trial ^ jnp.int32(INT_MIN)
        cnt = _count_rows(lambda r0: key_ref[rows(r0), :] >= trial_s, nstep)
        take = cnt >= topk
        return jnp.where(take, trial, cand), jnp.where(take, cnt, n_ge), jnp.where(take, n_rej, cnt)

    def count_above(cand):
        thr = cand ^ jnp.int32(INT_MIN)
        return _count_rows(lambda r0: key_ref[rows(r0), :] > thr, nstep)

    n_head = jnp.where((qb + 1) * Q_BLOCK > topk, RADIX_HEAD, 0)
    cand, n_ge, n_rej = lax.fori_loop(
        0, n_head, bit_body,
        (jnp.zeros((1, LANES), I32), jnp.broadcast_to(nstep * KEY_STEP, (1, LANES)), jnp.zeros((1, LANES), I32)))
    n_gt = count_above(cand)
    settled = (n_ge == topk) | (n_rej == n_gt)
    lim_ref[1:2, :] = cand
    lim_ref[2:3, :] = n_ge
    lim_ref[3:4, :] = n_gt

    @pl.when(jnp.logical_and(n_head > 0, jnp.min(jnp.where(settled, 1, 0)) == 0))
    def _():
        cand_t, n_ge_t, _ = lax.fori_loop(RADIX_HEAD, 32, bit_body, (cand, n_ge, n_rej))
        lim_ref[1:2, :] = cand_t
        lim_ref[2:3, :] = n_ge_t
        lim_ref[3:4, :] = count_above(cand_t)
    thr = lim_ref[1:2, :] ^ jnp.int32(INT_MIN)
    n_ge = lim_ref[2:3, :]
    n_gt = lim_ref[3:4, :]

    n_eq = n_ge - n_gt
    need = topk - n_gt
    excess = (n_eq > need) & (thr > jnp.int32(NEG_INF_KEY))
    lim_ref[0:1, :] = jnp.full((1, LANES), 2 ** 30, I32)

    @pl.when(jnp.max(jnp.where(excess, 1, 0)) > 0)
    def _():
        def idx_body(i, lim):
            trial = lim | lax.shift_left(jnp.int32(1), idx_bits - 1 - i)
            cnt = _count_rows(lambda r0: (key_ref[rows(r0), :] == thr) & ((r0 + row_i) < trial), nstep)
            return jnp.where(cnt < need, trial, lim)
        lim = lax.fori_loop(0, idx_bits, idx_body, jnp.zeros((1, LANES), I32))
        lim_ref[0:1, :] = jnp.where(excess, lim, 2 ** 30)
    tie_lim = lim_ref[0:1, :]

    def max_body(c, m_all):
        m_all = list(m_all)
        for sub in range(KEY_SUBS):
            r0 = _key_row0(c, sub)
            keys = key_ref[rows(r0), :]
            kpos = r0 + row_i
            sel = ((keys > thr) | ((keys == thr) & (kpos <= tie_lim))) & (kpos <= q_pos)
            bias = jnp.where(sel, 0.0, NEG_INF)
            bias2 = jnp.concatenate([bias, bias], axis=-1)
            for pair in range(N_PAIRS):
                g = (2 * pair) // A_REP
                s = _dot_nt(k_ref[rows(r0), g * LANES:(g + 1) * LANES], q_ref[pair * PAIR:(pair + 1) * PAIR, :]) + bias2
                s_ref[rows(r0), pair * PAIR:(pair + 1) * PAIR] = s
                m_all[pair] = jnp.maximum(m_all[pair], jnp.max(s, axis=0, keepdims=True))
        return tuple(m_all)
    m_fin = _loop_steps(nstep, max_body, tuple(jnp.full((1, PAIR), -1e30, F32) for _ in range(N_PAIRS)),
                        MXU_LOOP_WIDTHS)

    oacc_ref[...] = jnp.zeros(oacc_ref.shape, F32)

    def pv_body(c, carry):
        cols = pl.ds(pl.multiple_of(c * KEY_STEP, KEY_STEP), KEY_STEP)
        for pair in range(N_PAIRS):
            g = (2 * pair) // A_REP
            p = jnp.exp(s_ref[cols, pair * PAIR:(pair + 1) * PAIR] - m_fin[pair]).astype(BF16)
            oacc_ref[pair] += _dot(vt_ref[g * LANES:(g + 1) * LANES, cols], p)
        return carry
    _loop_steps(nstep, pv_body, 0, MXU_LOOP_WIDTHS)

    outs = []
    for pair in range(N_PAIRS):
        o_p = oacc_ref[pair]
        for j in range(2):
            o_h = o_p[:, j * LANES:(j + 1) * LANES]
            outs.append(o_h[0:A_HEAD_DIM, :] / o_h[A_HEAD_DIM:A_HEAD_DIM + 1, :])
    o_ref[...] = jnp.transpose(jnp.concatenate(outs, axis=0)).astype(BF16)


def _dsa(qi_cat, small, q_pad, ki_cat, k_pad, v_t, bsz, seq):
    assert seq % KEY_STEP == 0
    nb = seq // Q_BLOCK
    topk = min(IDX_TOPK_MAX, seq // 4)
    idx_bits = max(1, int(math.ceil(math.log2(seq))))
    qrow = lambda w, c: pl.BlockSpec((Q_BLOCK, w), lambda b, j: (b * nb + j, c))
    brow = lambda w: pl.BlockSpec((seq, w), lambda b, j: (b, 0))
    return pl.pallas_call(
        functools.partial(_dsa_kernel, topk=topk, idx_bits=idx_bits),
        grid=(bsz, nb),
        in_specs=[pl.BlockSpec((IDX_HEADS * Q_BLOCK, LANES), lambda b, j: (b * nb + j, 0)),
                  qrow(LANES, S_KW // LANES),
                  pl.BlockSpec((A_HEADS * Q_BLOCK, LANES), lambda b, j: (b * nb + j, 0)),
                  brow(LANES), brow(A_KV_HEADS * LANES),
                  pl.BlockSpec((A_KV_HEADS * LANES, seq), lambda b, j: (0, b))],
        out_specs=qrow(A_Q, 0),
        out_shape=jax.ShapeDtypeStruct((bsz * seq, A_Q), BF16),
        scratch_shapes=[pltpu.VMEM((seq, LANES), I32), pltpu.VMEM((seq, A_HEADS * LANES), F32),
                        pltpu.VMEM((SUBLANES, LANES), I32), pltpu.VMEM((N_PAIRS, LANES, PAIR), F32)],
        compiler_params=_params("arbitrary", "arbitrary"),
        name="sparse_attention",
    )(qi_cat, small, q_pad, ki_cat, k_pad, v_t)


HALO = SUBLANES


def _causal_conv(ext_ref, u, w_ref, taps, rows):
    ext_ref[HALO:HALO + rows, :] = u
    y = u * w_ref[taps - 1:taps, :]
    for j in range(taps - 1):
        shift = taps - 1 - j
        y = y + ext_ref[HALO - shift:HALO - shift + rows, :] * w_ref[j:j + 1, :]
    ext_ref[0:HALO, :] = ext_ref[rows:rows + HALO, :]
    return y


def _mix_kernel(x_ref, g0_ref, g1_ref, g2_ref, g3_ref, guv_ref, scb_ref, scc_ref, scx_ref, mz_ref, xbc_ref,
                dt_ref, oa_ref, gate_ref, scw_ref, gvg_ref, ws_ref, wsb_ref, mcw_ref, mcb_ref, dtb_ref, alog_ref,
                md_ref, mng_ref, wb_ref, wo_ref, o_ref, sc_ext, xbc_ext, state_ref, y_ref, *, rows):
    @pl.when(pl.program_id(1) == 0)
    def _():
        sc_ext[0:HALO, :] = jnp.zeros((HALO, SC_W), F32)
        xbc_ext[0:HALO, :] = jnp.zeros((HALO, M_XBC), F32)
        state_ref[...] = jnp.zeros(state_ref.shape, F32)

    nchunk = rows // M_CHUNK
    row_i = lax.broadcasted_iota(I32, (M_CHUNK, M_CHUNK), 0)
    col_i = lax.broadcasted_iota(I32, (M_CHUNK, M_CHUNK), 1)
    tri = row_i >= col_i

    o_b = scb_ref[...].astype(F32) * _causal_conv(
        sc_ext, scc_ref[...].astype(F32) * scx_ref[...].astype(F32), scw_ref, SC_KERNEL, rows)

    guv = _gelu(guv_ref[...].astype(F32))
    gu = guv[:, :GM_W]
    gv = _rms(guv[:, GM_W:], gvg_ref[...]).astype(BF16)
    mixed_rows = []
    for ci in range(nchunk):
        r0 = ci * GM_CHUNK
        cols = []
        for gi in range(GM_GROUPS):
            vb = gv[r0:r0 + GM_CHUNK, gi * GM_GROUP_W:(gi + 1) * GM_GROUP_W]
            cols.append(_dot(ws_ref[gi], vb) + wsb_ref[:, gi:gi + 1])
        mixed_rows.append(jnp.concatenate(cols, axis=-1))
    o_c = gu * jnp.concatenate(mixed_rows, axis=0)

    xbc = _silu(_causal_conv(xbc_ext, xbc_ref[...].astype(F32), mcw_ref, M_CONV, rows) + mcb_ref[...])
    head_lane = lax.broadcasted_iota(I32, (1, LANES), 1) < M_HEADS
    dt = jnp.where(head_lane, _softplus(dt_ref[...] + dtb_ref[...]), 0.0)
    a_row = jnp.where(head_lane, -jnp.exp(alog_ref[...]), 0.0)
    tri_f = jnp.where(tri, 1.0, 0.0).astype(F32)
    for ci in range(nchunk):
        r0 = ci * M_CHUNK
        dtc = dt[r0:r0 + M_CHUNK, :]
        acs = _dot_f32(tri_f, dtc * a_row)
        acs_t = jnp.transpose(acs)
        b_t = jnp.transpose(xbc[r0:r0 + M_CHUNK, M_DINNER:M_DINNER + M_BC]).astype(BF16)
        cm = xbc[r0:r0 + M_CHUNK, M_DINNER + M_BC:M_XBC].astype(BF16)
        for g in range(M_GROUPS):
            bg_t = b_t[g * M_STATE:(g + 1) * M_STATE, :]
            cg = cm[:, g * M_STATE:(g + 1) * M_STATE]
            cb = _dot(cg, bg_t)
            for h in range(g * (M_HEADS // M_GROUPS), (g + 1) * (M_HEADS // M_GROUPS)):
                col = acs[:, h:h + 1]
                row = acs_t[h:h + 1, :]
                last = acs[M_CHUNK - 1:M_CHUNK, h:h + 1]
                decay = jnp.exp(jnp.where(tri, col - row, NEG_INF))
                xs_h = xbc[r0:r0 + M_CHUNK, h * M_HEAD_DIM:(h + 1) * M_HEAD_DIM]
                xdt = xs_h * dtc[:, h:h + 1]
                y = _dot((cb * decay).astype(BF16), xdt.astype(BF16))
                prev_t = state_ref[h]
                y = y + _dot(cg, prev_t.astype(BF16)) * jnp.exp(col)
                st_t = _dot(bg_t, (xdt * jnp.exp(last - col)).astype(BF16))
                state_ref[h] = prev_t * jnp.exp(last) + st_t
                y_ref[r0:r0 + M_CHUNK, h * M_HEAD_DIM:(h + 1) * M_HEAD_DIM] = y + md_ref[:, h:h + 1] * xs_h
    o_d = _rms(y_ref[...] * _silu(mz_ref[...].astype(F32)), mng_ref[...])

    branches = (oa_ref[...], o_b.astype(BF16), o_c.astype(BF16), o_d.astype(BF16))
    gate_refs = (g0_ref, g1_ref, g2_ref, g3_ref)
    merged = None
    for i in range(N_BRANCH):
        term = _sigmoid(gate_refs[i][...].astype(F32)) * _dot(branches[i], wb_ref[i])
        merged = term if merged is None else merged + term
    o_ref[...] = x_ref[...] + gate_ref[0] * _dot(merged.astype(BF16), wo_ref[...])


def _mix(x2, proj, small, o_a, gate1, lw, bsz, seq):
    d = x2.shape[1]
    rows = min(256, seq)
    ns = seq // rows
    tok = lambda w, c: pl.BlockSpec((rows, w), lambda b, s: (b * ns + s, c))
    full = lambda a: pl.BlockSpec(a.shape, lambda b, s: (0,) * a.ndim)
    weights = [lw["sc_conv_w"], lw["g_v_norm_g"], lw["ws"], lw["ws_b"], lw["m_conv_w"], lw["m_conv_b"],
               lw["m_dt_bias"], lw["m_a_log"], lw["m_d"], lw["m_norm_g"], lw["w_branch"], lw["w_out"]]
    return pl.pallas_call(
        functools.partial(_mix_kernel, rows=rows),
        grid=(bsz, ns),
        in_specs=[tok(d, 0)] + [tok(D_MODEL, i) for i in range(N_BRANCH)]
        + [tok(2 * GM_W, P_GUV // (2 * GM_W)), tok(SC_W, P_SCB // SC_W), tok(SC_W, P_SCC // SC_W),
           tok(SC_W, P_SCX // SC_W), tok(M_DINNER, P_MZ // M_DINNER), tok(M_XBC, P_XBC // M_XBC),
           tok(LANES, S_DT // LANES), tok(A_Q, 0),
           pl.BlockSpec((1, 1, d), lambda b, s: (b, 0, 0))]
        + [full(w) for w in weights],
        out_specs=tok(d, 0),
        out_shape=jax.ShapeDtypeStruct(x2.shape, F32),
        scratch_shapes=[pltpu.VMEM((HALO + rows, SC_W), F32), pltpu.VMEM((HALO + rows, M_XBC), F32),
                        pltpu.VMEM((M_HEADS, M_STATE, M_HEAD_DIM), F32), pltpu.VMEM((rows, M_DINNER), F32)],
        compiler_params=_params("arbitrary", "arbitrary"),
        name="mixers_merge",
    )(x2, proj, proj, proj, proj, proj, proj, proj, proj, proj, proj, small, o_a,
      gate1.reshape(bsz, 1, d), *weights)


ROUTE_ROWS = EXPERTS_PER_GROUP * SUBLANES
BEST_ROW = ROUTE_ROWS


def _route_row(e):
    return (e % EXPERTS_PER_GROUP) * SUBLANES + e // EXPERTS_PER_GROUP


def _route(scores_t, bias_col):
    ntok = scores_t.shape[1]
    row = lax.broadcasted_iota(I32, (SUBLANES, ntok), 0)
    members = range(EXPERTS_PER_GROUP)
    sc = [scores_t[i * SUBLANES:(i + 1) * SUBLANES, :] for i in members]
    x = [sc[i] + bias_col[i * SUBLANES:(i + 1) * SUBLANES, :] for i in members]
    hi1, lo1 = jnp.maximum(x[0], x[1]), jnp.minimum(x[0], x[1])
    hi2, lo2 = jnp.maximum(x[2], x[3]), jnp.minimum(x[2], x[3])
    grp = jnp.maximum(hi1, hi2) + jnp.maximum(jnp.minimum(hi1, hi2), jnp.maximum(lo1, lo2))
    grp = jnp.where(row < N_EXPERT_GROUPS, grp, NEG_INF)
    best_val = jnp.max(grp, axis=0, keepdims=True)
    best = jnp.min(jnp.where(grp == best_val, row, SUBLANES), axis=0, keepdims=True)
    in_best = row == best
    y = [jnp.max(jnp.where(in_best, x[i], NEG_INF), axis=0, keepdims=True) for i in members]
    s = [jnp.sum(jnp.where(in_best, sc[i], 0.0), axis=0, keepdims=True) for i in members]

    def first_argmax(vals):
        top = functools.reduce(jnp.maximum, vals)
        idx = jnp.full(top.shape, len(vals) - 1, I32)
        for i in reversed(range(len(vals) - 1)):
            idx = jnp.where(vals[i] == top, i, idx)
        return idx

    i1 = first_argmax(y)
    i2 = first_argmax([jnp.where(i1 == i, NEG_INF, y[i]) for i in members])
    w1 = sum(jnp.where(i1 == i, s[i], 0.0) for i in members)
    w2 = sum(jnp.where(i2 == i, s[i], 0.0) for i in members)
    tot = w1 + w2
    parts = [jnp.where(in_best, jnp.where(i1 == i, w1 / tot, 0.0) + jnp.where(i2 == i, w2 / tot, 0.0), 0.0)
             for i in members]
    parts.append(jnp.broadcast_to(best.astype(F32), (SUBLANES, ntok)))
    parts.append(jnp.zeros((LANES - (ROUTE_ROWS + SUBLANES), ntok), F32))
    return jnp.concatenate(parts, axis=0)


MOE_CHUNK = 128


def _dot_tn(a, b):
    return lax.dot_general(a, b, (((0,), (0,)), ((), ())), preferred_element_type=F32)


def _moe_kernel(x_ref, g_ref, sc_ref, sh_ref, gate_ref, rwh_ref, rwl_ref, rb_ref, tri_ref, wgu_ref, wd_ref, o_ref,
                h_ref, c2_ref, grp_ref, rank_ref, tot_ref):
    grp_id = pl.program_id(1)
    tm = x_ref.shape[0]
    lane = lax.broadcasted_iota(I32, (tm, LANES), 1)

    @pl.when(grp_id == 0)
    def _():
        h = _rms(x_ref[...], g_ref[...]) * (1.0 + sc_ref[0]) + sh_ref[0]
        hh, hl = _split(h)
        h_ref[...] = hh
        logits_t = (_dot_nt(rwh_ref[...], hh) + _dot_nt(rwh_ref[...], hl) + _dot_nt(rwl_ref[...], hh))
        routed = jnp.transpose(_route(_sigmoid(logits_t[:ROUTE_ROWS, :]), rb_ref[...]))
        best = routed[:, BEST_ROW:BEST_ROW + 1].astype(I32)
        comb = jnp.where(lane < ROUTE_ROWS, routed, 0.0)
        c_hi, c_lo = _split(comb)
        c2_ref[:, :LANES] = c_hi
        c2_ref[:, LANES:] = c_lo
        grp_ref[...] = jnp.broadcast_to(best, (tm, LANES))
        onehot = jnp.where(lane == best, 1.0, 0.0)
        offs = jnp.zeros((1, LANES), F32)
        for b in range(tm // MOE_CHUNK):
            blk = onehot[b * MOE_CHUNK:(b + 1) * MOE_CHUNK, :]
            ranks = _dot(tri_ref[...], blk.astype(BF16)) + offs
            own = jnp.sum(blk * ranks, axis=-1, keepdims=True)
            rank_ref[b * MOE_CHUNK:(b + 1) * MOE_CHUNK, :] = jnp.broadcast_to(own, (MOE_CHUNK, LANES))
            offs = offs + jnp.sum(blk, axis=0, keepdims=True)
        tot_ref[0:1, :] = offs
        o_ref[...] = jnp.zeros(o_ref.shape, F32)

    lane1 = lax.broadcasted_iota(I32, (1, LANES), 1)
    n_tok = jnp.sum(jnp.where(lane1 == grp_id, tot_ref[0:1, :], 0.0)).astype(I32)
    n_chunks = lax.div(n_tok + (MOE_CHUNK - 1), jnp.int32(MOE_CHUNK))
    lane_f = lane.astype(F32)
    lane_c = lax.broadcasted_iota(I32, (MOE_CHUNK, LANES), 1)

    def chunk_body(c, carry):
        base = (c * MOE_CHUNK).astype(F32)
        pick = ((rank_ref[...] - base) == lane_f) & (grp_ref[...] == grp_id)
        st = jnp.where(pick, 1.0, 0.0).astype(BF16)
        xc = _dot_tn(st, h_ref[...]).astype(BF16)
        wc2 = _dot_tn(st, c2_ref[...])
        wc = wc2[:, :LANES] + wc2[:, LANES:]
        y = jnp.zeros((MOE_CHUNK, x_ref.shape[1]), F32)
        for el in range(EXPERTS_PER_GROUP):
            gu = _dot(xc, wgu_ref[el])
            hid = _silu(gu[:, :D_FF_EXPERT]) * gu[:, D_FF_EXPERT:]
            w_e = jnp.sum(jnp.where(lane_c == el * SUBLANES + grp_id, wc, 0.0), axis=-1, keepdims=True)
            y = y + w_e * _dot(hid.astype(BF16), wd_ref[el])
        y_hi, y_lo = _split(y)
        o_ref[...] += _dot(jnp.concatenate([st, st], axis=1), jnp.concatenate([y_hi, y_lo], axis=0))
        return carry
    _loop_steps(n_chunks, chunk_body, 0)

    @pl.when(grp_id == pl.num_programs(1) - 1)
    def _():
        o_ref[...] = x_ref[...] + gate_ref[0] * o_ref[...]


def _moe(x2, seq, g, scale, shift, gate, rw_hi, rw_lo, rb, wgu, wd):
    ntok, d = x2.shape
    bsz = scale.shape[0]
    tm = min(1024, seq)
    row = lambda i, e: (i, 0)
    per_b = lambda i, e: ((i * tm) // seq, 0, 0)
    vec = pl.BlockSpec((1, 1, d), per_b)
    const2 = lambda a: pl.BlockSpec(a.shape, lambda i, e: (0, 0))
    tri = jnp.asarray(np.tril(np.ones((MOE_CHUNK, MOE_CHUNK), np.float32), -1), BF16)
    return pl.pallas_call(
        _moe_kernel,
        grid=(ntok // tm, N_EXPERT_GROUPS),
        in_specs=[pl.BlockSpec((tm, d), row), pl.BlockSpec((1, d), lambda i, e: (0, 0)), vec, vec, vec,
                  const2(rw_hi), const2(rw_lo), const2(rb), const2(tri),
                  pl.BlockSpec((EXPERTS_PER_GROUP, d, 2 * D_FF_EXPERT), lambda i, e: (e, 0, 0)),
                  pl.BlockSpec((EXPERTS_PER_GROUP, D_FF_EXPERT, d), lambda i, e: (e, 0, 0))],
        out_specs=pl.BlockSpec((tm, d), row),
        out_shape=jax.ShapeDtypeStruct(x2.shape, F32),
        scratch_shapes=[pltpu.VMEM((tm, d), BF16), pltpu.VMEM((tm, 2 * LANES), BF16),
                        pltpu.VMEM((tm, LANES), I32), pltpu.VMEM((tm, LANES), F32), pltpu.VMEM((SUBLANES, LANES), F32)],
        compiler_params=_params("arbitrary", "arbitrary"),
        name="moe",
    )(x2, g.reshape(1, d), scale.reshape(bsz, 1, d), shift.reshape(bsz, 1, d), gate.reshape(bsz, 1, d),
      rw_hi, rw_lo, rb, tri, wgu, wd)


def _pad_lanes(v, width=LANES):
    v = v.reshape(1, -1)
    return jnp.pad(v, ((0, 0), (0, width - v.shape[1])))


def _layer_weights(w_in, sc_conv_w, g_v_norm_g, g_spatial_w, g_spatial_b, m_conv_w, m_conv_b, m_dt_bias,
                   m_a_log, m_d, m_norm_g, w_branch, w_out):
    splits = (A_Q, A_KV, A_KV, IDX_Q, IDX_DIM, IDX_HEADS, SC_W, SC_W, SC_W, 2 * GM_W, M_DINNER, M_XBC, M_HEADS,
              N_BRANCH * D_MODEL)
    pts = [int(p) for p in np.cumsum(splits)[:-1]]
    (q, k, v, qi, ki, wi, sc_b, sc_c, sc_x, g_uv, m_z, m_xbc, m_dt, gates) = jnp.split(w_in, pts, axis=-1)
    d = w_in.shape[0]
    w_main = jnp.concatenate([gates, g_uv, q, sc_b, sc_c, sc_x, m_z, m_xbc, k, v], axis=-1).astype(BF16)
    w_small = jnp.concatenate(
        [qi, ki, wi, jnp.zeros((d, LANES - IDX_DIM - IDX_HEADS), F32), m_dt, jnp.zeros((d, LANES - M_HEADS), F32)],
        axis=-1)
    ws_hi = w_small.astype(BF16)
    ws_lo = (w_small - ws_hi.astype(F32)).astype(BF16)
    tri = np.tril(np.ones((GM_CHUNK, GM_CHUNK), dtype=bool))
    return dict(
        w_main=w_main, ws_hi=ws_hi, ws_lo=ws_lo,
        sc_conv_w=sc_conv_w, g_v_norm_g=g_v_norm_g.reshape(1, GM_W),
        ws=jnp.where(tri[None], g_spatial_w, 0.0).astype(BF16), ws_b=g_spatial_b.T,
        m_conv_w=m_conv_w, m_conv_b=m_conv_b.reshape(1, M_XBC),
        m_dt_bias=_pad_lanes(m_dt_bias), m_a_log=_pad_lanes(m_a_log), m_d=_pad_lanes(m_d),
        m_norm_g=m_norm_g.reshape(1, M_DINNER), w_branch=w_branch.astype(BF16), w_out=w_out.astype(BF16))


def kernel(x, c, positions, ada_w, ada_b, norm1_g, w_in, q_norm_g, k_norm_g, sc_conv_w, g_v_norm_g, g_spatial_w,
           g_spatial_b, m_conv_w, m_conv_b, m_dt_bias, m_a_log, m_d, m_norm_g, w_branch, w_out, norm2_g, router_w,
           router_bias, exp_w_gate, exp_w_up, exp_w_down):
    bsz, seq, d = x.shape
    depth = ada_w.shape[0]
    mod = _modulation(c, ada_w, ada_b)
    route_rows = np.array([_route_row(e) for e in range(N_EXPERTS)])
    rw = jnp.zeros((LANES, d), F32).at[route_rows].set(router_w.T)
    rw_hi = rw.astype(BF16)
    rw_lo = (rw - rw_hi.astype(F32)).astype(BF16)
    rb = jnp.zeros((ROUTE_ROWS, 1), F32).at[route_rows, 0].set(router_bias)
    pos2 = positions.reshape(bsz * seq, 1)
    x2 = x.reshape(bsz * seq, d)
    for l in range(depth):
        shift1, scale1, gate1, shift2, scale2, gate2 = jnp.split(mod[l], 6, axis=-1)
        lw = _layer_weights(w_in[l], sc_conv_w[l], g_v_norm_g[l], g_spatial_w[l], g_spatial_b[l], m_conv_w[l],
                            m_conv_b[l], m_dt_bias[l], m_a_log[l], m_d[l], m_norm_g[l], w_branch[l], w_out[l])
        proj, small = _inproj(x2, seq, norm1_g[l], scale1, shift1, lw["w_main"], lw["ws_hi"], lw["ws_lo"])
        q_pad, k_pad, v_t, qi_cat, ki_cat = _prep(proj, small, pos2, seq, q_norm_g[l], k_norm_g[l])
        o_a = _dsa(qi_cat, small, q_pad, ki_cat, k_pad, v_t, bsz, seq)
        x2 = _mix(x2, proj, small, o_a, gate1, lw, bsz, seq)
        wgu = jnp.concatenate([exp_w_gate[l], exp_w_up[l]], axis=-1).astype(BF16)
        x2 = _moe(x2, seq, norm2_g[l], scale2, shift2, gate2, rw_hi, rw_lo, rb, wgu, exp_w_down[l].astype(BF16))
    return x2.reshape(bsz, seq, d)
```

```python
import functools
import math

import jax
import jax.numpy as jnp
import numpy as np
from jax import lax
from jax.experimental import pallas as pl
from jax.experimental.pallas import tpu as pltpu

F32 = jnp.float32
BF16 = jnp.bfloat16
I32 = jnp.int32

D_MODEL = 1024
RMS_EPS = 1e-6
ROPE_THETA = 500000.0
Q_BLOCK = 128
A_HEADS = 8
A_KV_HEADS = 2
A_HEAD_DIM = 64
A_ROT = A_HEAD_DIM // 4
A_REP = A_HEADS // A_KV_HEADS
IDX_HEADS = 8
IDX_DIM = 32
IDX_ROT = IDX_DIM // 4
IDX_TOPK_MAX = 256
SC_W = 512
SC_KERNEL = 3
GM_W = 512
GM_GROUPS = 4
GM_GROUP_W = GM_W // GM_GROUPS
GM_CHUNK = 128
M_DINNER = 512
M_HEAD_DIM = 64
M_HEADS = M_DINNER // M_HEAD_DIM
M_GROUPS = 2
M_STATE = 64
M_CONV = 4
M_CHUNK = 128
M_BC = M_GROUPS * M_STATE
M_XBC = M_DINNER + 2 * M_BC
N_BRANCH = 4
BRANCH_W = 512
N_EXPERTS = 16
N_EXPERT_GROUPS = 4
EXPERTS_PER_GROUP = N_EXPERTS // N_EXPERT_GROUPS
D_FF_EXPERT = 512
A_Q = A_HEADS * A_HEAD_DIM
A_KV = A_KV_HEADS * A_HEAD_DIM
IDX_Q = IDX_HEADS * IDX_DIM

LANES = 128
SUBLANES = 8
VMEM_LIMIT = 56 * 1024 * 1024

P_GATES = 0
P_GUV = 4096
P_Q = 5120
P_SCB = 5632
P_SCC = 6144
P_SCX = 6656
P_MZ = 7168
P_XBC = 7680
P_K = 8448
P_V = 8576
P_COLS = 8704
S_QI = 0
S_KW = 256
S_DT = 384
S_COLS = 512

INT_MIN = -2147483648
NEG_INF = float("-inf")
NEG_INF_KEY = -2139095041


def _dot(a, b):
    return jnp.dot(a, b, preferred_element_type=F32)


def _dot_nt(a, b):
    return lax.dot_general(a, b, (((1,), (1,)), ((), ())), preferred_element_type=F32)


def _dot_f32(a, b):
    return jnp.dot(a, b, preferred_element_type=F32, precision=lax.Precision.HIGHEST)


def _split(x):
    hi = x.astype(BF16)
    lo = (x - hi.astype(F32)).astype(BF16)
    return hi, lo


def _dot3(a_hi, a_lo, b_hi, b_lo):
    return _dot(a_hi, b_hi) + _dot(a_lo, b_hi) + _dot(a_hi, b_lo)


def _sigmoid(x):
    return jax.nn.sigmoid(x)


def _silu(x):
    return x * _sigmoid(x)


def _softplus(x):
    return jnp.maximum(x, 0.0) + jnp.log1p(jnp.exp(-jnp.abs(x)))


def _gelu(x):
    return 0.5 * x * (1.0 + lax.erf(x * np.float32(np.sqrt(0.5))))


def _rms(x, g_row):
    ms = jnp.mean(x * x, axis=-1, keepdims=True)
    return x * lax.rsqrt(ms + RMS_EPS) * g_row


def _params(*sem):
    return pltpu.CompilerParams(dimension_semantics=sem, vmem_limit_bytes=VMEM_LIMIT)


def _mod_kernel(c_ref, w_ref, b_ref, o_ref):
    o_ref[0] = _dot_f32(_silu(c_ref[...]), w_ref[0]) + b_ref[0]


def _modulation(c, ada_w, ada_b):
    depth, d, n = ada_w.shape
    bsz = c.shape[0]
    tn = 768
    return pl.pallas_call(
        _mod_kernel,
        grid=(depth, n // tn),
        in_specs=[pl.BlockSpec((bsz, d), lambda l, j: (0, 0)),
                  pl.BlockSpec((1, d, tn), lambda l, j: (l, 0, j)),
                  pl.BlockSpec((1, 1, tn), lambda l, j: (l, 0, j))],
        out_specs=pl.BlockSpec((1, bsz, tn), lambda l, j: (l, 0, j)),
        out_shape=jax.ShapeDtypeStruct((depth, bsz, n), F32),
        compiler_params=_params("arbitrary", "arbitrary"),
        name="modulation",
    )(c, ada_w, ada_b.reshape(depth, 1, n))


def _inproj_kernel(x_ref, g_ref, sc_ref, sh_ref, w_ref, wsh_ref, wsl_ref, o_ref, os_ref, hh_ref, hl_ref):
    @pl.when(pl.program_id(1) == 0)
    def _():
        h = _rms(x_ref[...], g_ref[...]) * (1.0 + sc_ref[0]) + sh_ref[0]
        hh, hl = _split(h)
        hh_ref[...] = hh
        hl_ref[...] = hl
        os_ref[...] = _dot3(hh, hl, wsh_ref[...], wsl_ref[...])

    o_ref[...] = _dot(hh_ref[...], w_ref[...]).astype(BF16)


def _inproj(x2, seq, g, scale, shift, w_main, ws_hi, ws_lo):
    ntok, d = x2.shape
    tm = min(1024, seq)
    tn = P_COLS // 4
    bsz = scale.shape[0]
    row = lambda i, j: (i, 0)
    per_b = lambda i, j: ((i * tm) // seq, 0, 0)
    return pl.pallas_call(
        _inproj_kernel,
        grid=(ntok // tm, P_COLS // tn),
        in_specs=[pl.BlockSpec((tm, d), row),
                  pl.BlockSpec((1, d), lambda i, j: (0, 0)),
                  pl.BlockSpec((1, 1, d), per_b),
                  pl.BlockSpec((1, 1, d), per_b),
                  pl.BlockSpec((d, tn), lambda i, j: (0, j)),
                  pl.BlockSpec((d, S_COLS), lambda i, j: (0, 0)),
                  pl.BlockSpec((d, S_COLS), lambda i, j: (0, 0))],
        out_specs=[pl.BlockSpec((tm, tn), lambda i, j: (i, j)),
                   pl.BlockSpec((tm, S_COLS), row)],
        out_shape=[jax.ShapeDtypeStruct((ntok, P_COLS), BF16),
                   jax.ShapeDtypeStruct((ntok, S_COLS), F32)],
        scratch_shapes=[pltpu.VMEM((tm, d), BF16), pltpu.VMEM((tm, d), BF16)],
        compiler_params=_params("arbitrary", "arbitrary"),
        name="inproj",
    )(x2, g.reshape(1, d), scale.reshape(bsz, 1, d), shift.reshape(bsz, 1, d), w_main, ws_hi, ws_lo)


ROPE_ZERO_LANE = A_ROT // 2 + IDX_ROT // 2


def _rope_angles(pos_f):
    half_a, half_i = A_ROT // 2, IDX_ROT // 2
    lane = lax.broadcasted_iota(I32, (1, LANES), 1)
    k = jnp.where(lane < half_a, lane, lane - half_a).astype(F32)
    rot = jnp.where(lane < half_a, np.float32(A_ROT), np.float32(IDX_ROT))
    inv_freq = jnp.exp(np.float32(-math.log(ROPE_THETA)) * k * 2.0 / rot)
    ang = pos_f * jnp.where(lane < half_a + half_i, inv_freq, 0.0)
    return jnp.cos(ang), jnp.sin(ang)


def _rope_tables(cos_sin, place_ref, period, rot, limit=LANES):
    half = rot // 2

    def expand(x):
        hi, lo = _split(x)
        return _dot(hi, place_ref[...]) + _dot(lo, place_ref[...])

    lane_all = lax.broadcasted_iota(I32, (1, LANES), 1)
    lane = lane_all % period
    cos = jnp.where(lane_all < limit, expand(cos_sin[0]), 1.0)
    sin = jnp.where(lane_all < limit, expand(cos_sin[1]), 0.0)
    sin_lo = jnp.where(lane < half, -sin, 0.0)
    sin_hi = jnp.where((lane >= half) & (lane < rot), sin, 0.0)
    return cos, sin_lo, sin_hi


def _rope(x, tables, half):
    cos, sin_lo, sin_hi = tables
    width = x.shape[-1]
    return (x * cos + pltpu.roll(x, width - half, 1) * sin_lo + pltpu.roll(x, half, 1) * sin_hi)


def _prep_kernel(q_ref, k_ref, v_ref, qi_ref, kw_ref, pos_ref, qg_ref, kg_ref, bdq_ref, bdk_ref,
                 pq_ref, pk_ref, pqi1_ref, pqi2_ref, pki1_ref, pki2_ref, pra_ref, pri_ref,
                 qo_ref, ko_ref, vto_ref, qio_ref, kio_ref):
    cos_sin = _rope_angles(pos_ref[...].astype(F32))
    att_tab = _rope_tables(cos_sin, pra_ref, A_HEAD_DIM, A_ROT)
    idx_tab = _rope_tables(cos_sin, pri_ref, IDX_DIM, IDX_ROT)
    kw_tab = _rope_tables(cos_sin, pri_ref, IDX_DIM, IDX_ROT, limit=IDX_DIM)

    def head_norm(x, bd_ref, g_ref):
        sq_hi, sq_lo = _split(x * x)
        ms = (_dot(sq_hi, bd_ref[...]) + _dot(sq_lo, bd_ref[...])) * np.float32(1.0 / A_HEAD_DIM)
        return x * lax.rsqrt(ms + RMS_EPS) * g_ref[...]

    def rope_wide(x, tab, half):
        pieces = [_rope(x[:, c * LANES:(c + 1) * LANES], tab, half) for c in range(x.shape[-1] // LANES)]
        return pieces[0] if len(pieces) == 1 else jnp.concatenate(pieces, axis=-1)

    def store_head_rows(o_ref, x):
        for blk in range(x.shape[0] // Q_BLOCK):
            for h in range(x.shape[1] // LANES):
                r0 = (blk * (x.shape[1] // LANES) + h) * Q_BLOCK
                o_ref[r0:r0 + Q_BLOCK, :] = x[blk * Q_BLOCK:(blk + 1) * Q_BLOCK, h * LANES:(h + 1) * LANES]

    q = rope_wide(head_norm(q_ref[...].astype(F32), bdq_ref, qg_ref), att_tab, A_ROT // 2)
    q = (q * np.float32(A_HEAD_DIM ** -0.5)).astype(BF16)
    store_head_rows(qo_ref, _dot(q, pq_ref[...]).astype(BF16))
    k = rope_wide(head_norm(k_ref[...].astype(F32), bdk_ref, kg_ref), att_tab, A_ROT // 2).astype(BF16)
    ko_ref[...] = _dot(k, pk_ref[...]).astype(BF16)
    v_pad = _dot(v_ref[...], pk_ref[...])
    ones_lane = (lax.broadcasted_iota(I32, (1, A_KV_HEADS * LANES), 1) % LANES) == A_HEAD_DIM
    vto_ref[...] = jnp.transpose(jnp.where(ones_lane, 1.0, v_pad)).astype(BF16)

    qi_hi, qi_lo = _split(rope_wide(qi_ref[...], idx_tab, IDX_ROT // 2))
    store_head_rows(qio_ref, (_dot(qi_hi, pqi1_ref[...]) + _dot(qi_lo, pqi2_ref[...])).astype(BF16))
    ki_hi, ki_lo = _split(_rope(kw_ref[...], kw_tab, IDX_ROT // 2))
    kio_ref[...] = (_dot(ki_hi, pki1_ref[...]) + _dot(ki_lo, pki2_ref[...])).astype(BF16)


@functools.lru_cache(maxsize=None)
def _prep_constants():
    def block_diag(width):
        i = np.arange(width)
        return (i[:, None] // A_HEAD_DIM == i[None, :] // A_HEAD_DIM).astype(np.float32)

    def pad_heads(n_heads):
        p = np.zeros((n_heads * A_HEAD_DIM, n_heads * LANES), np.float32)
        for h in range(n_heads):
            for d in range(A_HEAD_DIM):
                p[h * A_HEAD_DIM + d, h * LANES + d] = 1.0
        return p

    pqi1 = np.zeros((IDX_Q, IDX_HEADS * LANES), np.float32)
    pqi2 = np.zeros((IDX_Q, IDX_HEADS * LANES), np.float32)
    for h in range(IDX_HEADS):
        for d in range(IDX_DIM):
            pqi1[h * IDX_DIM + d, h * LANES + d] = 1.0
            pqi1[h * IDX_DIM + d, h * LANES + IDX_DIM + d] = 1.0
            pqi2[h * IDX_DIM + d, h * LANES + 2 * IDX_DIM + d] = 1.0
    pki1 = np.zeros((LANES, LANES), np.float32)
    pki2 = np.zeros((LANES, LANES), np.float32)
    for d in range(IDX_DIM):
        pki1[d, d] = 1.0
        pki1[d, 2 * IDX_DIM + d] = 1.0
        pki2[d, IDX_DIM + d] = 1.0
    def rope_place(period, rot, first_src):
        p = np.zeros((LANES, LANES), np.float32)
        for j in range(LANES):
            jj = j % period
            p[first_src + jj % (rot // 2) if jj < rot else ROPE_ZERO_LANE, j] = 1.0
        return p

    mats = (block_diag(A_Q), block_diag(A_KV), pad_heads(A_HEADS), pad_heads(A_KV_HEADS), pqi1, pqi2, pki1, pki2,
            rope_place(A_HEAD_DIM, A_ROT, 0), rope_place(IDX_DIM, IDX_ROT, A_ROT // 2))
    return tuple(np.asarray(m) for m in mats)


def _prep(proj, small, pos2, seq, q_g, k_g):
    ntok = proj.shape[0]
    tm = min(512, seq)
    consts = [jnp.asarray(m, BF16) for m in _prep_constants()]
    row = lambda w, c: pl.BlockSpec((tm, w), lambda i: (i, c))
    full = lambda a: pl.BlockSpec(a.shape, lambda i: (0, 0))
    qg = jnp.tile(q_g, A_HEADS).reshape(1, A_Q)
    kg = jnp.tile(k_g, A_KV_HEADS).reshape(1, A_KV)
    return pl.pallas_call(
        _prep_kernel,
        grid=(ntok // tm,),
        in_specs=[row(A_Q, P_Q // A_Q), row(A_KV, P_K // A_KV), row(A_KV, P_V // A_KV), row(IDX_Q, S_QI // IDX_Q),
                  row(LANES, S_KW // LANES), row(1, 0), full(qg), full(kg)] + [full(m) for m in consts],
        out_specs=[pl.BlockSpec((tm * A_HEADS, LANES), lambda i: (i, 0)), row(A_KV_HEADS * LANES, 0),
                   pl.BlockSpec((A_KV_HEADS * LANES, tm), lambda i: (0, i)),
                   pl.BlockSpec((tm * IDX_HEADS, LANES), lambda i: (i, 0)), row(LANES, 0)],
        out_shape=[jax.ShapeDtypeStruct((ntok * A_HEADS, LANES), BF16),
                   jax.ShapeDtypeStruct((ntok, A_KV_HEADS * LANES), BF16),
                   jax.ShapeDtypeStruct((A_KV_HEADS * LANES, ntok), BF16),
                   jax.ShapeDtypeStruct((ntok * IDX_HEADS, LANES), BF16),
                   jax.ShapeDtypeStruct((ntok, LANES), BF16)],
        compiler_params=_params("arbitrary"),
        name="attn_prep",
    )(proj, proj, proj, small, small, pos2, qg, kg, *consts)


def _sortable(x):
    bits = pltpu.bitcast(jnp.where(x == 0.0, 0.0, x), I32)
    return jnp.where(bits < 0, bits ^ jnp.int32(0x7FFFFFFF), bits)


KEY_STEP = 512
KEY_SUBS = KEY_STEP // Q_BLOCK


def _key_row0(step, sub):
    return pl.multiple_of(step * KEY_STEP + sub * Q_BLOCK, Q_BLOCK)


MXU_LOOP_WIDTHS = (4, 2, 1)


def _loop_steps(nstep, body, init, widths=(2, 1)):
    carry, start = init, jnp.int32(0)
    for w in widths:
        count = lax.div(nstep - start, jnp.int32(w))

        def multi(i, c, w=w, start=start):
            for k in range(w):
                c = body(start + i * w + k, c)
            return c
        carry = lax.fori_loop(0, count, multi, carry)
        start = start + count * w
    return carry


def _count_rows(pred_fn, nstep):
    def body(c, acc):
        for sub in range(KEY_SUBS):
            ones = jnp.where(pred_fn(_key_row0(c, sub)), jnp.int32(1), jnp.int32(0))
            acc = acc + jnp.sum(ones.reshape(Q_BLOCK // SUBLANES, SUBLANES, LANES), axis=0)
        return acc
    acc = _loop_steps(nstep, body, jnp.zeros((SUBLANES, LANES), I32), MXU_LOOP_WIDTHS)
    return jnp.sum(acc, axis=0, keepdims=True)


RADIX_HEAD = 26
PAIR = 2 * LANES
N_PAIRS = A_HEADS // 2


def _dsa_kernel(qi_ref, kw_ref, q_ref, ki_ref, k_ref, vt_ref, o_ref, key_ref, s_ref, lim_ref, oacc_ref,
                *, topk, idx_bits):
    qb = pl.program_id(1)
    nstep = lax.div(qb, jnp.int32(KEY_SUBS)) + 1
    row_i = lax.broadcasted_iota(I32, (Q_BLOCK, LANES), 0)
    q_pos = qb * Q_BLOCK + lax.broadcasted_iota(I32, (Q_BLOCK, LANES), 1)

    w_rows = jnp.transpose(kw_ref[...])[IDX_DIM:IDX_DIM + IDX_HEADS, :]
    w_rows = w_rows * np.float32(IDX_HEADS ** -0.5) * np.float32(IDX_DIM ** -0.5)

    def rows(r0):
        return pl.ds(r0, Q_BLOCK)

    def score_body(c, carry):
        for sub in range(KEY_SUBS):
            r0 = _key_row0(c, sub)
            ki = ki_ref[rows(r0), :]
            s = jnp.zeros((Q_BLOCK, LANES), F32)
            for pair in range(IDX_HEADS // 2):
                d = _dot_nt(ki, qi_ref[pair * PAIR:(pair + 1) * PAIR, :])
                for j in range(2):
                    h = 2 * pair + j
                    s = s + jnp.maximum(d[:, j * LANES:(j + 1) * LANES], 0.0) * w_rows[h:h + 1, :]
            key_ref[rows(r0), :] = _sortable(jnp.where((r0 + row_i) <= q_pos, s, NEG_INF))
        return carry
    _loop_steps(nstep, score_body, 0, MXU_LOOP_WIDTHS)

    def bit_body(i, carry):
        cand, n_ge, n_rej = carry
        trial = cand | lax.shift_left(jnp.int32(1), 31 - i)
        trial_s = trial ^ jnp.int32(INT_MIN)
        cnt = _count_rows(lambda r0: key_ref[rows(r0), :] >= trial_s, nstep)
        take = cnt >= topk
        return jnp.where(take, trial, cand), jnp.where(take, cnt, n_ge), jnp.where(take, n_rej, cnt)

    def count_above(cand):
        thr = cand ^ jnp.int32(INT_MIN)
        return _count_rows(lambda r0: key_ref[rows(r0), :] > thr, nstep)

    n_head = jnp.where((qb + 1) * Q_BLOCK > topk, RADIX_HEAD, 0)
    cand, n_ge, n_rej = lax.fori_loop(
        0, n_head, bit_body,
        (jnp.zeros((1, LANES), I32), jnp.broadcast_to(nstep * KEY_STEP, (1, LANES)), jnp.zeros((1, LANES), I32)))
    n_gt = count_above(cand)
    settled = (n_ge == topk) | (n_rej == n_gt)
    lim_ref[1:2, :] = cand
    lim_ref[2:3, :] = n_ge
    lim_ref[3:4, :] = n_gt

    @pl.when(jnp.logical_and(n_head > 0, jnp.min(jnp.where(settled, 1, 0)) == 0))
    def _():
        cand_t, n_ge_t, _ = lax.fori_loop(RADIX_HEAD, 32, bit_body, (cand, n_ge, n_rej))
        lim_ref[1:2, :] = cand_t
        lim_ref[2:3, :] = n_ge_t
        lim_ref[3:4, :] = count_above(cand_t)
    thr = lim_ref[1:2, :] ^ jnp.int32(INT_MIN)
    n_ge = lim_ref[2:3, :]
    n_gt = lim_ref[3:4, :]

    n_eq = n_ge - n_gt
    need = topk - n_gt
    excess = (n_eq > need) & (thr > jnp.int32(NEG_INF_KEY))
    lim_ref[0:1, :] = jnp.full((1, LANES), 2 ** 30, I32)

    @pl.when(jnp.max(jnp.where(excess, 1, 0)) > 0)
    def _():
        def idx_body(i, lim):
            trial = lim | lax.shift_left(jnp.int32(1), idx_bits - 1 - i)
            cnt = _count_rows(lambda r0: (key_ref[rows(r0), :] == thr) & ((r0 + row_i) < trial), nstep)
            return jnp.where(cnt < need, trial, lim)
        lim = lax.fori_loop(0, idx_bits, idx_body, jnp.zeros((1, LANES), I32))
        lim_ref[0:1, :] = jnp.where(excess, lim, 2 ** 30)
    tie_lim = lim_ref[0:1, :]

    def max_body(c, m_all):
        m_all = list(m_all)
        for sub in range(KEY_SUBS):
            r0 = _key_row0(c, sub)
            keys = key_ref[rows(r0), :]
            kpos = r0 + row_i
            sel = ((keys > thr) | ((keys == thr) & (kpos <= tie_lim))) & (kpos <= q_pos)
            bias = jnp.where(sel, 0.0, NEG_INF)
            bias2 = jnp.concatenate([bias, bias], axis=-1)
            for pair in range(N_PAIRS):
                g = (2 * pair) // A_REP
                s = _dot_nt(k_ref[rows(r0), g * LANES:(g + 1) * LANES], q_ref[pair * PAIR:(pair + 1) * PAIR, :]) + bias2
                s_ref[rows(r0), pair * PAIR:(pair + 1) * PAIR] = s
                m_all[pair] = jnp.maximum(m_all[pair], jnp.max(s, axis=0, keepdims=True))
        return tuple(m_all)
    m_fin = _loop_steps(nstep, max_body, tuple(jnp.full((1, PAIR), -1e30, F32) for _ in range(N_PAIRS)),
                        MXU_LOOP_WIDTHS)

    oacc_ref[...] = jnp.zeros(oacc_ref.shape, F32)

    def pv_body(c, carry):
        cols = pl.ds(pl.multiple_of(c * KEY_STEP, KEY_STEP), KEY_STEP)
        for pair in range(N_PAIRS):
            g = (2 * pair) // A_REP
            p = jnp.exp(s_ref[cols, pair * PAIR:(pair + 1) * PAIR] - m_fin[pair]).astype(BF16)
            oacc_ref[pair] += _dot(vt_ref[g * LANES:(g + 1) * LANES, cols], p)
        return carry
    _loop_steps(nstep, pv_body, 0, MXU_LOOP_WIDTHS)

    outs = []
    for pair in range(N_PAIRS):
        o_p = oacc_ref[pair]
        for j in range(2):
            o_h = o_p[:, j * LANES:(j + 1) * LANES]
            outs.append(o_h[0:A_HEAD_DIM, :] / o_h[A_HEAD_DIM:A_HEAD_DIM + 1, :])
    o_ref[...] = jnp.transpose(jnp.concatenate(outs, axis=0)).astype(BF16)


def _dsa(qi_cat, small, q_pad, ki_cat, k_pad, v_t, bsz, seq):
    assert seq % KEY_STEP == 0
    nb = seq // Q_BLOCK
    topk = min(IDX_TOPK_MAX, seq // 4)
    idx_bits = max(1, int(math.ceil(math.log2(seq))))
    qrow = lambda w, c: pl.BlockSpec((Q_BLOCK, w), lambda b, j: (b * nb + j, c))
    brow = lambda w: pl.BlockSpec((seq, w), lambda b, j: (b, 0))
    return pl.pallas_call(
        functools.partial(_dsa_kernel, topk=topk, idx_bits=idx_bits),
        grid=(bsz, nb),
        in_specs=[pl.BlockSpec((IDX_HEADS * Q_BLOCK, LANES), lambda b, j: (b * nb + j, 0)),
                  qrow(LANES, S_KW // LANES),
                  pl.BlockSpec((A_HEADS * Q_BLOCK, LANES), lambda b, j: (b * nb + j, 0)),
                  brow(LANES), brow(A_KV_HEADS * LANES),
                  pl.BlockSpec((A_KV_HEADS * LANES, seq), lambda b, j: (0, b))],
        out_specs=qrow(A_Q, 0),
        out_shape=jax.ShapeDtypeStruct((bsz * seq, A_Q), BF16),
        scratch_shapes=[pltpu.VMEM((seq, LANES), I32), pltpu.VMEM((seq, A_HEADS * LANES), F32),
                        pltpu.VMEM((SUBLANES, LANES), I32), pltpu.VMEM((N_PAIRS, LANES, PAIR), F32)],
        compiler_params=_params("arbitrary", "arbitrary"),
        name="sparse_attention",
    )(qi_cat, small, q_pad, ki_cat, k_pad, v_t)


HALO = SUBLANES


def _causal_conv(ext_ref, u, w_ref, taps, rows):
    ext_ref[HALO:HALO + rows, :] = u
    y = u * w_ref[taps - 1:taps, :]
    for j in range(taps - 1):
        shift = taps - 1 - j
        y = y + ext_ref[HALO - shift:HALO - shift + rows, :] * w_ref[j:j + 1, :]
    ext_ref[0:HALO, :] = ext_ref[rows:rows + HALO, :]
    return y


def _mix_kernel(x_ref, g0_ref, g1_ref, g2_ref, g3_ref, guv_ref, scb_ref, scc_ref, scx_ref, mz_ref, xbc_ref,
                dt_ref, oa_ref, gate_ref, scw_ref, gvg_ref, ws_ref, wsb_ref, mcw_ref, mcb_ref, dtb_ref, alog_ref,
                md_ref, mng_ref, wb_ref, wo_ref, o_ref, sc_ext, xbc_ext, state_ref, y_ref, *, rows):
    @pl.when(pl.program_id(1) == 0)
    def _():
        sc_ext[0:HALO, :] = jnp.zeros((HALO, SC_W), F32)
        xbc_ext[0:HALO, :] = jnp.zeros((HALO, M_XBC), F32)
        state_ref[...] = jnp.zeros(state_ref.shape, F32)

    nchunk = rows // M_CHUNK
    row_i = lax.broadcasted_iota(I32, (M_CHUNK, M_CHUNK), 0)
    col_i = lax.broadcasted_iota(I32, (M_CHUNK, M_CHUNK), 1)
    tri = row_i >= col_i

    o_b = scb_ref[...].astype(F32) * _causal_conv(
        sc_ext, scc_ref[...].astype(F32) * scx_ref[...].astype(F32), scw_ref, SC_KERNEL, rows)

    guv = _gelu(guv_ref[...].astype(F32))
    gu = guv[:, :GM_W]
    gv = _rms(guv[:, GM_W:], gvg_ref[...]).astype(BF16)
    mixed_rows = []
    for ci in range(nchunk):
        r0 = ci * GM_CHUNK
        cols = []
        for gi in range(GM_GROUPS):
            vb = gv[r0:r0 + GM_CHUNK, gi * GM_GROUP_W:(gi + 1) * GM_GROUP_W]
            cols.append(_dot(ws_ref[gi], vb) + wsb_ref[:, gi:gi + 1])
        mixed_rows.append(jnp.concatenate(cols, axis=-1))
    o_c = gu * jnp.concatenate(mixed_rows, axis=0)

    xbc = _silu(_causal_conv(xbc_ext, xbc_ref[...].astype(F32), mcw_ref, M_CONV, rows) + mcb_ref[...])
    head_lane = lax.broadcasted_iota(I32, (1, LANES), 1) < M_HEADS
    dt = jnp.where(head_lane, _softplus(dt_ref[...] + dtb_ref[...]), 0.0)
    a_row = jnp.where(head_lane, -jnp.exp(alog_ref[...]), 0.0)
    tri_f = jnp.where(tri, 1.0, 0.0).astype(F32)
    for ci in range(nchunk):
        r0 = ci * M_CHUNK
        dtc = dt[r0:r0 + M_CHUNK, :]
        acs = _dot_f32(tri_f, dtc * a_row)
        acs_t = jnp.transpose(acs)
        b_t = jnp.transpose(xbc[r0:r0 + M_CHUNK, M_DINNER:M_DINNER + M_BC]).astype(BF16)
        cm = xbc[r0:r0 + M_CHUNK, M_DINNER + M_BC:M_XBC].astype(BF16)
        for g in range(M_GROUPS):
            bg_t = b_t[g * M_STATE:(g + 1) * M_STATE, :]
            cg = cm[:, g * M_STATE:(g + 1) * M_STATE]
            cb = _dot(cg, bg_t)
            for h in range(g * (M_HEADS // M_GROUPS), (g + 1) * (M_HEADS // M_GROUPS)):
                col = acs[:, h:h + 1]
                row = acs_t[h:h + 1, :]
                last = acs[M_CHUNK - 1:M_CHUNK, h:h + 1]
                decay = jnp.exp(jnp.where(tri, col - row, NEG_INF))
                xs_h = xbc[r0:r0 + M_CHUNK, h * M_HEAD_DIM:(h + 1) * M_HEAD_DIM]
                xdt = xs_h * dtc[:, h:h + 1]
                y = _dot((cb * decay).astype(BF16), xdt.astype(BF16))
                prev_t = state_ref[h]
                y = y + _dot(cg, prev_t.astype(BF16)) * jnp.exp(col)
                st_t = _dot(bg_t, (xdt * jnp.exp(last - col)).astype(BF16))
                state_ref[h] = prev_t * jnp.exp(last) + st_t
                y_ref[r0:r0 + M_CHUNK, h * M_HEAD_DIM:(h + 1) * M_HEAD_DIM] = y + md_ref[:, h:h + 1] * xs_h
    o_d = _rms(y_ref[...] * _silu(mz_ref[...].astype(F32)), mng_ref[...])

    branches = (oa_ref[...], o_b.astype(BF16), o_c.astype(BF16), o_d.astype(BF16))
    gate_refs = (g0_ref, g1_ref, g2_ref, g3_ref)
    merged = None
    for i in range(N_BRANCH):
        term = _sigmoid(gate_refs[i][...].astype(F32)) * _dot(branches[i], wb_ref[i])
        merged = term if merged is None else merged + term
    o_ref[...] = x_ref[...] + gate_ref[0] * _dot(merged.astype(BF16), wo_ref[...])


def _mix(x2, proj, small, o_a, gate1, lw, bsz, seq):
    d = x2.shape[1]
    rows = min(512, seq)
    ns = seq // rows
    tok = lambda w, c: pl.BlockSpec((rows, w), lambda b, s: (b * ns + s, c))
    full = lambda a: pl.BlockSpec(a.shape, lambda b, s: (0,) * a.ndim)
    weights = [lw["sc_conv_w"], lw["g_v_norm_g"], lw["ws"], lw["ws_b"], lw["m_conv_w"], lw["m_conv_b"],
               lw["m_dt_bias"], lw["m_a_log"], lw["m_d"], lw["m_norm_g"], lw["w_branch"], lw["w_out"]]
    return pl.pallas_call(
        functools.partial(_mix_kernel, rows=rows),
        grid=(bsz, ns),
        in_specs=[tok(d, 0)] + [tok(D_MODEL, i) for i in range(N_BRANCH)]
        + [tok(2 * GM_W, P_GUV // (2 * GM_W)), tok(SC_W, P_SCB // SC_W), tok(SC_W, P_SCC // SC_W),
           tok(SC_W, P_SCX // SC_W), tok(M_DINNER, P_MZ // M_DINNER), tok(M_XBC, P_XBC // M_XBC),
           tok(LANES, S_DT // LANES), tok(A_Q, 0),
           pl.BlockSpec((1, 1, d), lambda b, s: (b, 0, 0))]
        + [full(w) for w in weights],
        out_specs=tok(d, 0),
        out_shape=jax.ShapeDtypeStruct(x2.shape, F32),
        scratch_shapes=[pltpu.VMEM((HALO + rows, SC_W), F32), pltpu.VMEM((HALO + rows, M_XBC), F32),
                        pltpu.VMEM((M_HEADS, M_STATE, M_HEAD_DIM), F32), pltpu.VMEM((rows, M_DINNER), F32)],
        compiler_params=_params("arbitrary", "arbitrary"),
        name="mixers_merge",
    )(x2, proj, proj, proj, proj, proj, proj, proj, proj, proj, proj, small, o_a,
      gate1.reshape(bsz, 1, d), *weights)


ROUTE_ROWS = EXPERTS_PER_GROUP * SUBLANES
BEST_ROW = ROUTE_ROWS


def _route_row(e):
    return (e % EXPERTS_PER_GROUP) * SUBLANES + e // EXPERTS_PER_GROUP


def _route(scores_t, bias_col):
    ntok = scores_t.shape[1]
    row = lax.broadcasted_iota(I32, (SUBLANES, ntok), 0)
    members = range(EXPERTS_PER_GROUP)
    sc = [scores_t[i * SUBLANES:(i + 1) * SUBLANES, :] for i in members]
    x = [sc[i] + bias_col[i * SUBLANES:(i + 1) * SUBLANES, :] for i in members]
    hi1, lo1 = jnp.maximum(x[0], x[1]), jnp.minimum(x[0], x[1])
    hi2, lo2 = jnp.maximum(x[2], x[3]), jnp.minimum(x[2], x[3])
    grp = jnp.maximum(hi1, hi2) + jnp.maximum(jnp.minimum(hi1, hi2), jnp.maximum(lo1, lo2))
    grp = jnp.where(row < N_EXPERT_GROUPS, grp, NEG_INF)
    best_val = jnp.max(grp, axis=0, keepdims=True)
    best = jnp.min(jnp.where(grp == best_val, row, SUBLANES), axis=0, keepdims=True)
    in_best = row == best
    y = [jnp.max(jnp.where(in_best, x[i], NEG_INF), axis=0, keepdims=True) for i in members]
    s = [jnp.sum(jnp.where(in_best, sc[i], 0.0), axis=0, keepdims=True) for i in members]

    def first_argmax(vals):
        top = functools.reduce(jnp.maximum, vals)
        idx = jnp.full(top.shape, len(vals) - 1, I32)
        for i in reversed(range(len(vals) - 1)):
            idx = jnp.where(vals[i] == top, i, idx)
        return idx

    i1 = first_argmax(y)
    i2 = first_argmax([jnp.where(i1 == i, NEG_INF, y[i]) for i in members])
    w1 = sum(jnp.where(i1 == i, s[i], 0.0) for i in members)
    w2 = sum(jnp.where(i2 == i, s[i], 0.0) for i in members)
    tot = w1 + w2
    parts = [jnp.where(in_best, jnp.where(i1 == i, w1 / tot, 0.0) + jnp.where(i2 == i, w2 / tot, 0.0), 0.0)
             for i in members]
    parts.append(jnp.broadcast_to(best.astype(F32), (SUBLANES, ntok)))
    parts.append(jnp.zeros((LANES - (ROUTE_ROWS + SUBLANES), ntok), F32))
    return jnp.concatenate(parts, axis=0)


MOE_CHUNK = 128


def _dot_tn(a, b):
    return lax.dot_general(a, b, (((0,), (0,)), ((), ())), preferred_element_type=F32)


def _moe_kernel(x_ref, g_ref, sc_ref, sh_ref, gate_ref, rwh_ref, rwl_ref, rb_ref, tri_ref, wgu_ref, wd_ref, o_ref,
                h_ref, c2_ref, grp_ref, rank_ref, tot_ref):
    grp_id = pl.program_id(1)
    tm = x_ref.shape[0]
    lane = lax.broadcasted_iota(I32, (tm, LANES), 1)

    @pl.when(grp_id == 0)
    def _():
        h = _rms(x_ref[...], g_ref[...]) * (1.0 + sc_ref[0]) + sh_ref[0]
        hh, hl = _split(h)
        h_ref[...] = hh
        logits_t = (_dot_nt(rwh_ref[...], hh) + _dot_nt(rwh_ref[...], hl) + _dot_nt(rwl_ref[...], hh))
        routed = jnp.transpose(_route(_sigmoid(logits_t[:ROUTE_ROWS, :]), rb_ref[...]))
        best = routed[:, BEST_ROW:BEST_ROW + 1].astype(I32)
        comb = jnp.where(lane < ROUTE_ROWS, routed, 0.0)
        c_hi, c_lo = _split(comb)
        c2_ref[:, :LANES] = c_hi
        c2_ref[:, LANES:] = c_lo
        grp_ref[...] = jnp.broadcast_to(best, (tm, LANES))
        onehot = jnp.where(lane == best, 1.0, 0.0)
        offs = jnp.zeros((1, LANES), F32)
        for b in range(tm // MOE_CHUNK):
            blk = onehot[b * MOE_CHUNK:(b + 1) * MOE_CHUNK, :]
            ranks = _dot(tri_ref[...], blk.astype(BF16)) + offs
            own = jnp.sum(blk * ranks, axis=-1, keepdims=True)
            rank_ref[b * MOE_CHUNK:(b + 1) * MOE_CHUNK, :] = jnp.broadcast_to(own, (MOE_CHUNK, LANES))
            offs = offs + jnp.sum(blk, axis=0, keepdims=True)
        tot_ref[0:1, :] = offs
        o_ref[...] = jnp.zeros(o_ref.shape, F32)

    lane1 = lax.broadcasted_iota(I32, (1, LANES), 1)
    n_tok = jnp.sum(jnp.where(lane1 == grp_id, tot_ref[0:1, :], 0.0)).astype(I32)
    n_chunks = lax.div(n_tok + (MOE_CHUNK - 1), jnp.int32(MOE_CHUNK))
    lane_f = lane.astype(F32)
    lane_c = lax.broadcasted_iota(I32, (MOE_CHUNK, LANES), 1)

    def chunk_body(c, carry):
        base = (c * MOE_CHUNK).astype(F32)
        pick = ((rank_ref[...] - base) == lane_f) & (grp_ref[...] == grp_id)
        st = jnp.where(pick, 1.0, 0.0).astype(BF16)
        xc = _dot_tn(st, h_ref[...]).astype(BF16)
        wc2 = _dot_tn(st, c2_ref[...])
        wc = wc2[:, :LANES] + wc2[:, LANES:]
        y = jnp.zeros((MOE_CHUNK, x_ref.shape[1]), F32)
        for el in range(EXPERTS_PER_GROUP):
            gu = _dot(xc, wgu_ref[el])
            hid = _silu(gu[:, :D_FF_EXPERT]) * gu[:, D_FF_EXPERT:]
            w_e = jnp.sum(jnp.where(lane_c == el * SUBLANES + grp_id, wc, 0.0), axis=-1, keepdims=True)
            y = y + w_e * _dot(hid.astype(BF16), wd_ref[el])
        y_hi, y_lo = _split(y)
        o_ref[...] += _dot(jnp.concatenate([st, st], axis=1), jnp.concatenate([y_hi, y_lo], axis=0))
        return carry
    _loop_steps(n_chunks, chunk_body, 0)

    @pl.when(grp_id == pl.num_programs(1) - 1)
    def _():
        o_ref[...] = x_ref[...] + gate_ref[0] * o_ref[...]


def _moe(x2, seq, g, scale, shift, gate, rw_hi, rw_lo, rb, wgu, wd):
    ntok, d = x2.shape
    bsz = scale.shape[0]
    tm = min(1024, seq)
    row = lambda i, e: (i, 0)
    per_b = lambda i, e: ((i * tm) // seq, 0, 0)
    vec = pl.BlockSpec((1, 1, d), per_b)
    const2 = lambda a: pl.BlockSpec(a.shape, lambda i, e: (0, 0))
    tri = jnp.asarray(np.tril(np.ones((MOE_CHUNK, MOE_CHUNK), np.float32), -1), BF16)
    return pl.pallas_call(
        _moe_kernel,
        grid=(ntok // tm, N_EXPERT_GROUPS),
        in_specs=[pl.BlockSpec((tm, d), row), pl.BlockSpec((1, d), lambda i, e: (0, 0)), vec, vec, vec,
                  const2(rw_hi), const2(rw_lo), const2(rb), const2(tri),
                  pl.BlockSpec((EXPERTS_PER_GROUP, d, 2 * D_FF_EXPERT), lambda i, e: (e, 0, 0)),
                  pl.BlockSpec((EXPERTS_PER_GROUP, D_FF_EXPERT, d), lambda i, e: (e, 0, 0))],
        out_specs=pl.BlockSpec((tm, d), row),
        out_shape=jax.ShapeDtypeStruct(x2.shape, F32),
        scratch_shapes=[pltpu.VMEM((tm, d), BF16), pltpu.VMEM((tm, 2 * LANES), BF16),
                        pltpu.VMEM((tm, LANES), I32), pltpu.VMEM((tm, LANES), F32), pltpu.VMEM((SUBLANES, LANES), F32)],
        compiler_params=_params("arbitrary", "arbitrary"),
        name="moe",
    )(x2, g.reshape(1, d), scale.reshape(bsz, 1, d), shift.reshape(bsz, 1, d), gate.reshape(bsz, 1, d),
      rw_hi, rw_lo, rb, tri, wgu, wd)


def _pad_lanes(v, width=LANES):
    v = v.reshape(1, -1)
    return jnp.pad(v, ((0, 0), (0, width - v.shape[1])))


def _layer_weights(w_in, sc_conv_w, g_v_norm_g, g_spatial_w, g_spatial_b, m_conv_w, m_conv_b, m_dt_bias,
                   m_a_log, m_d, m_norm_g, w_branch, w_out):
    splits = (A_Q, A_KV, A_KV, IDX_Q, IDX_DIM, IDX_HEADS, SC_W, SC_W, SC_W, 2 * GM_W, M_DINNER, M_XBC, M_HEADS,
              N_BRANCH * D_MODEL)
    pts = [int(p) for p in np.cumsum(splits)[:-1]]
    (q, k, v, qi, ki, wi, sc_b, sc_c, sc_x, g_uv, m_z, m_xbc, m_dt, gates) = jnp.split(w_in, pts, axis=-1)
    d = w_in.shape[0]
    w_main = jnp.concatenate([gates, g_uv, q, sc_b, sc_c, sc_x, m_z, m_xbc, k, v], axis=-1).astype(BF16)
    w_small = jnp.concatenate(
        [qi, ki, wi, jnp.zeros((d, LANES - IDX_DIM - IDX_HEADS), F32), m_dt, jnp.zeros((d, LANES - M_HEADS), F32)],
        axis=-1)
    ws_hi = w_small.astype(BF16)
    ws_lo = (w_small - ws_hi.astype(F32)).astype(BF16)
    tri = np.tril(np.ones((GM_CHUNK, GM_CHUNK), dtype=bool))
    return dict(
        w_main=w_main, ws_hi=ws_hi, ws_lo=ws_lo,
        sc_conv_w=sc_conv_w, g_v_norm_g=g_v_norm_g.reshape(1, GM_W),
        ws=jnp.where(tri[None], g_spatial_w, 0.0).astype(BF16), ws_b=g_spatial_b.T,
        m_conv_w=m_conv_w, m_conv_b=m_conv_b.reshape(1, M_XBC),
        m_dt_bias=_pad_lanes(m_dt_bias), m_a_log=_pad_lanes(m_a_log), m_d=_pad_lanes(m_d),
        m_norm_g=m_norm_g.reshape(1, M_DINNER), w_branch=w_branch.astype(BF16), w_out=w_out.astype(BF16))


def kernel(x, c, positions, ada_w, ada_b, norm1_g, w_in, q_norm_g, k_norm_g, sc_conv_w, g_v_norm_g, g_spatial_w,
           g_spatial_b, m_conv_w, m_conv_b, m_dt_bias, m_a_log, m_d, m_norm_g, w_branch, w_out, norm2_g, router_w,
           router_bias, exp_w_gate, exp_w_up, exp_w_down):
    bsz, seq, d = x.shape
    depth = ada_w.shape[0]
    mod = _modulation(c, ada_w, ada_b)
    route_rows = np.array([_route_row(e) for e in range(N_EXPERTS)])
    rw = jnp.zeros((LANES, d), F32).at[route_rows].set(router_w.T)
    rw_hi = rw.astype(BF16)
    rw_lo = (rw - rw_hi.astype(F32)).astype(BF16)
    rb = jnp.zeros((ROUTE_ROWS, 1), F32).at[route_rows, 0].set(router_bias)
    pos2 = positions.reshape(bsz * seq, 1)
    x2 = x.reshape(bsz * seq, d)
    for l in range(depth):
        shift1, scale1, gate1, shift2, scale2, gate2 = jnp.split(mod[l], 6, axis=-1)
        lw = _layer_weights(w_in[l], sc_conv_w[l], g_v_norm_g[l], g_spatial_w[l], g_spatial_b[l], m_conv_w[l],
                            m_conv_b[l], m_dt_bias[l], m_a_log[l], m_d[l], m_norm_g[l], w_branch[l], w_out[l])
        proj, small = _inproj(x2, seq, norm1_g[l], scale1, shift1, lw["w_main"], lw["ws_hi"], lw["ws_lo"])
        q_pad, k_pad, v_t, qi_cat, ki_cat = _prep(proj, small, pos2, seq, q_norm_g[l], k_norm_g[l])
        o_a = _dsa(qi_cat, small, q_pad, ki_cat, k_pad, v_t, bsz, seq)
        x2 = _mix(x2, proj, small, o_a, gate1, lw, bsz, seq)
        wgu = jnp.concatenate([exp_w_gate[l], exp_w_up[l]], axis=-1).astype(BF16)
        x2 = _moe(x2, seq, norm2_g[l], scale2, shift2, gate2, rw_hi, rw_lo, rb, wgu, exp_w_down[l].astype(BF16))
    return x2.reshape(bsz, seq, d)
```
